```python
import jax
import jax.numpy as jnp
from jax import lax
import numpy as np

D_MODEL = 1024
BATCH = 4
SEQ = 8192
DEPTH = 2
DEC_BATCH = 32
DEC_SEQ = 8
PAST_LEN = 16384
PAGE_SIZE = 128

N_GLA_LAYERS = (DEPTH + 1) // 2
N_DSA_LAYERS = DEPTH // 2
DEEPNORM_ALPHA = (2.0 * DEPTH) ** 0.25
DEEPNORM_BETA = (8.0 * DEPTH) ** -0.25
LN_EPS = 1e-5

GLA_HEADS = 4
GLA_DK = D_MODEL // 2 // GLA_HEADS
GLA_DV = D_MODEL // GLA_HEADS
GLA_GATE_RANK = 16
GLA_TAU = 16.0
GLA_CHUNK = 64
GLA_IN = 2 * GLA_HEADS * GLA_DK + 2 * GLA_HEADS * GLA_DV + GLA_GATE_RANK

DSA_HEADS = 8
DSA_HEAD_DIM = D_MODEL // DSA_HEADS
DSA_KV_HEADS = 2
IDX_HEADS = 8
IDX_DIM = 64
TOPK_MAX = 256
Q_BLOCK = 128
DSA_IN = DSA_HEADS * DSA_HEAD_DIM + 2 * DSA_KV_HEADS * DSA_HEAD_DIM + IDX_HEADS * IDX_DIM + IDX_DIM + IDX_HEADS

D_FF = 7 * D_MODEL // 2
N_EXPERTS = 8
TOP_K_EXPERTS = 2
MOE_BLOCK = 128

PLE_DIM = 256

kernel_name = 'gla_dsa_hybrid_decoder_step'


def layer_norm(x, g, b):
    xf = x.astype(jnp.float32)
    mu = jnp.mean(xf, -1, keepdims=True)
    var = jnp.mean(jnp.square(xf - mu), -1, keepdims=True)
    y = (xf - mu) * lax.rsqrt(var + LN_EPS) * g.astype(jnp.float32) + b.astype(jnp.float32)
    return y.astype(x.dtype)


def gla_recurrence(q, k, v, log_a, s0):
    B, T = q.shape[:2]
    c = GLA_CHUNK if T % GLA_CHUNK == 0 else T
    nc = T // c

    def to_chunks(a):
        return a.reshape(B, nc, c, GLA_HEADS, a.shape[-1]).transpose(1, 0, 3, 2, 4)

    q, k, v, log_a = to_chunks(q), to_chunks(k), to_chunks(v), to_chunks(log_a)
    b = jnp.cumsum(log_a, axis=3)
    b_last = b[:, :, :, -1:, :]
    q_dec = q * jnp.exp(b)
    k_inv = k * jnp.exp(-b)
    k_dec = k * jnp.exp(b_last - b)
    causal = jnp.tril(jnp.ones((c, c), dtype=bool))
    att = jnp.where(causal, jnp.einsum('nbhtd,nbhsd->nbhts', q_dec, k_inv), 0.0)
    o_intra = jnp.einsum('nbhts,nbhsv->nbhtv', att, v)
    decay = jnp.exp(b_last[:, :, :, 0, :])

    def step(s, inp):
        qd, kd, vv, dc = inp
        o = jnp.einsum('bhtd,bhdv->bhtv', qd, s)
        s = dc[..., None] * s + jnp.einsum('bhtd,bhtv->bhdv', kd, vv)
        return s, o

    s_final, o_inter = lax.scan(step, s0, (q_dec, k_dec, v, decay))
    o = (o_intra + o_inter).transpose(1, 0, 3, 2, 4).reshape(B, T, GLA_HEADS, GLA_DV)
    return o, s_final


def gla_mixer(x, s0, w_in, w_alpha_up, b_alpha, norm_g, w_out):
    B, T, _ = x.shape
    f32 = jnp.float32
    hk, hv = GLA_HEADS * GLA_DK, GLA_HEADS * GLA_DV
    h = x @ w_in
    q, k, v, g, a = jnp.split(h, [hk, 2 * hk, 2 * hk + hv, 2 * hk + 2 * hv], axis=-1)
    q = q.astype(f32).reshape(B, T, GLA_HEADS, GLA_DK) * GLA_DK ** -0.5
    k = k.astype(f32).reshape(B, T, GLA_HEADS, GLA_DK)
    v = v.astype(f32).reshape(B, T, GLA_HEADS, GLA_DV)
    log_a = jax.nn.log_sigmoid((a @ w_alpha_up + b_alpha).astype(f32)).reshape(B, T, GLA_HEADS, GLA_DK) / GLA_TAU
    o, s = gla_recurrence(q, k, v, log_a, s0.astype(f32))
    mu = jnp.mean(o, -1, keepdims=True)
    var = jnp.mean(jnp.square(o - mu), -1, keepdims=True)
    o = (o - mu) * lax.rsqrt(var + LN_EPS) * norm_g.astype(f32)
    o = o.reshape(B, T, hv).astype(x.dtype) * jax.nn.silu(g)
    return o @ w_out, s


def dsa_project(x, w_in):
    B, T, _ = x.shape
    hq, hkv, hiq = DSA_HEADS * DSA_HEAD_DIM, DSA_KV_HEADS * DSA_HEAD_DIM, IDX_HEADS * IDX_DIM
    cuts = [hq, hq + hkv, hq + 2 * hkv, hq + 2 * hkv + hiq, hq + 2 * hkv + hiq + IDX_DIM]
    q, k, v, iq, ik, iw = jnp.split(x @ w_in, cuts, axis=-1)
    return (q.reshape(B, T, DSA_HEADS, DSA_HEAD_DIM),
            k.reshape(B, T, DSA_KV_HEADS, DSA_HEAD_DIM),
            v.reshape(B, T, DSA_KV_HEADS, DSA_HEAD_DIM),
            iq.reshape(B, T, IDX_HEADS, IDX_DIM), ik, iw)


def indexer_scores(iq, iw, ik):
    f32 = jnp.float32
    dots = jnp.einsum('bqhd,bsd->bqhs', iq.astype(f32), ik.astype(f32)) * IDX_DIM ** -0.5
    return jnp.einsum('bqh,bqhs->bqs', iw.astype(f32) * IDX_HEADS ** -0.5, jax.nn.relu(dots))


def sparse_attend(q, k_sel, v_sel, valid):
    B, Q = q.shape[:2]
    qg = q.reshape(B, Q, DSA_KV_HEADS, DSA_HEADS // DSA_KV_HEADS, DSA_HEAD_DIM)
    logits = jnp.einsum('bqcgd,bqncd->bqcgn', qg, k_sel).astype(jnp.float32) * DSA_HEAD_DIM ** -0.5
    logits = jnp.where(valid[:, :, None, None, :], logits, -jnp.inf)
    p = jax.nn.softmax(logits, axis=-1).astype(v_sel.dtype)
    o = jnp.einsum('bqcgn,bqncd->bqcgd', p, v_sel)
    return o.reshape(B, Q, DSA_HEADS * DSA_HEAD_DIM)


def dsa_prompt(x, w_in, w_out):
    B, S, _ = x.shape
    q, k, v, iq, ik, iw = dsa_project(x, w_in)
    topk = min(TOPK_MAX, S // 4)
    b_idx = jnp.arange(B)[:, None, None]
    key_pos = jnp.arange(S)

    def one_block(j):
        start = j * Q_BLOCK
        sl = lambda a: lax.dynamic_slice_in_dim(a, start, Q_BLOCK, axis=1)
        t_pos = start + jnp.arange(Q_BLOCK)
        sc = indexer_scores(sl(iq), sl(iw), ik)
        sc = jnp.where(key_pos[None, None, :] <= t_pos[None, :, None], sc, -jnp.inf)
        _, sel = lax.top_k(sc, topk)
        valid = sel <= t_pos[None, :, None]
        return sparse_attend(sl(q), k[b_idx, sel], v[b_idx, sel], valid)

    o = lax.map(one_block, jnp.arange(S // Q_BLOCK))
    o = o.transpose(1, 0, 2, 3).reshape(B, S, DSA_HEADS * DSA_HEAD_DIM)
    return o @ w_out, k, v, ik


def dsa_sample(x, cache_k, cache_v, cache_ik, page_table, w_in, w_out):
    B, T, _ = x.shape
    past = page_table.shape[1] * PAGE_SIZE
    q, k, v, iq, ik, iw = dsa_project(x, w_in)
    topk = min(TOPK_MAX, (past + T) // 4)
    b_idx = jnp.arange(B)[:, None, None]
    t_loc = jnp.arange(T)
    past_ik = cache_ik[page_table].reshape(B, past, IDX_DIM)
    sc_new = jnp.where(t_loc[None, :] <= t_loc[:, None], indexer_scores(iq, iw, ik), -jnp.inf)
    sc = jnp.concatenate([indexer_scores(iq, iw, past_ik), sc_new], axis=-1)
    _, sel = lax.top_k(sc, topk)
    valid = sel <= (past + t_loc)[None, :, None]
    is_new = sel >= past
    s_past = jnp.minimum(sel, past - 1)
    phys = page_table[b_idx, s_past // PAGE_SIZE] * PAGE_SIZE + s_past % PAGE_SIZE
    s_new = jnp.clip(sel - past, 0, T - 1)

    def gather(pool, new):
        flat = pool.reshape(-1, DSA_KV_HEADS, DSA_HEAD_DIM)
        return jnp.where(is_new[..., None, None], new[b_idx, s_new], flat[phys])

    o = sparse_attend(q, gather(cache_k, k), gather(cache_v, v), valid)
    return o @ w_out, k, v, ik


def swiglu(x, wg, wu, wd):
    return (jax.nn.silu(x @ wg) * (x @ wu)) @ wd


def moe_ffn(x, w_router, wg, wu, wd):
    B, T, D = x.shape
    n_tok = B * T
    xt = x.reshape(n_tok, D)
    logits = (xt @ w_router).astype(jnp.float32)
    top_logit, top_e = lax.top_k(logits, TOP_K_EXPERTS)
    gate = jax.nn.softmax(top_logit, axis=-1)
    n_assign = n_tok * TOP_K_EXPERTS
    flat_e = top_e.reshape(-1)
    flat_tok = jnp.arange(n_assign) // TOP_K_EXPERTS
    order = jnp.argsort(flat_e * n_assign + jnp.arange(n_assign))
    sorted_e = flat_e[order]
    counts = jnp.bincount(flat_e, length=N_EXPERTS)
    padded = (counts + MOE_BLOCK - 1) // MOE_BLOCK * MOE_BLOCK
    start = jnp.cumsum(counts) - counts
    pad_end = jnp.cumsum(padded)
    pad_start = pad_end - padded
    dest_sorted = pad_start[sorted_e] + jnp.arange(n_assign) - start[sorted_e]
    n_blocks = -(-n_assign // MOE_BLOCK) + N_EXPERTS
    rows = n_blocks * MOE_BLOCK
    row_tok = jnp.full((rows,), n_tok, jnp.int32).at[dest_sorted].set(flat_tok[order])
    x_rows = jnp.concatenate([xt, jnp.zeros((1, D), xt.dtype)])[row_tok].reshape(n_blocks, MOE_BLOCK, D)
    block_start = jnp.arange(n_blocks) * MOE_BLOCK
    block_e = jnp.minimum(jnp.sum(pad_end[None, :] <= block_start[:, None], axis=1), N_EXPERTS - 1)
    y_rows = lax.map(lambda a: swiglu(a[0], wg[a[1]], wu[a[1]], wd[a[1]]), (x_rows, block_e)).reshape(rows, D)
    dest = jnp.zeros((n_assign,), jnp.int32).at[order].set(dest_sorted)
    y = jnp.einsum('tk,tkd->td', gate.astype(x.dtype), y_rows[dest].reshape(n_tok, TOP_K_EXPERTS, D))
    return y.reshape(B, T, D)


def per_layer_embed(x, p, w_proj, w_gate):
    return x + jax.nn.sigmoid(x @ w_gate) * (p.astype(x.dtype) @ w_proj)


def setup_inputs(seed: int = 0) -> dict:
    key = jax.random.key(seed)
    keys = list(jax.random.split(key, 40))

    def nrm(shape, scale):
        return jax.random.normal(keys.pop(), shape, jnp.float32) * scale

    def gain(shape):
        return 1.0 + nrm(shape, 0.02)

    n_pages = PAST_LEN // PAGE_SIZE
    n_used = DEC_BATCH * n_pages
    n_pool = n_used + (n_used + 3) // 4
    page_table = jax.random.permutation(keys.pop(), n_pool)[:n_used].reshape(DEC_BATCH, n_pages).astype(jnp.int32)
    hv_gla = GLA_HEADS * GLA_DV
    h_dsa = DSA_HEADS * DSA_HEAD_DIM
    return {
        'x_prompt': nrm((BATCH, SEQ, D_MODEL), 1.0),
        'x_sample': nrm((DEC_BATCH, DEC_SEQ, D_MODEL), 1.0),
        'state_gla': nrm((N_GLA_LAYERS, DEC_BATCH, GLA_HEADS, GLA_DK, GLA_DV), 1.0),
        'cache_k': nrm((N_DSA_LAYERS, n_pool, PAGE_SIZE, DSA_KV_HEADS, DSA_HEAD_DIM), 1.0),
        'cache_v': nrm((N_DSA_LAYERS, n_pool, PAGE_SIZE, DSA_KV_HEADS, DSA_HEAD_DIM), 1.0),
        'cache_idx_k': nrm((N_DSA_LAYERS, n_pool, PAGE_SIZE, IDX_DIM), 1.0),
        'page_table': page_table,
        'p_prompt': nrm((DEPTH, BATCH, SEQ, PLE_DIM), 1.0),
        'p_sample': nrm((DEPTH, DEC_BATCH, DEC_SEQ, PLE_DIM), 1.0),
        'ln_mix_g': gain((DEPTH, D_MODEL)),
        'ln_mix_b': nrm((DEPTH, D_MODEL), 0.02),
        'ln_ffn_g': gain((DEPTH, D_MODEL)),
        'ln_ffn_b': nrm((DEPTH, D_MODEL), 0.02),
        'w_in_gla': nrm((N_GLA_LAYERS, D_MODEL, GLA_IN), D_MODEL ** -0.5),
        'w_alpha_up': nrm((N_GLA_LAYERS, GLA_GATE_RANK, GLA_HEADS * GLA_DK), GLA_GATE_RANK ** -0.5),
        'b_alpha': nrm((N_GLA_LAYERS, GLA_HEADS * GLA_DK), 0.1),
        'gla_norm_g': gain((N_GLA_LAYERS, GLA_HEADS, GLA_DV)),
        'w_out_gla': nrm((N_GLA_LAYERS, hv_gla, D_MODEL), DEEPNORM_BETA * hv_gla ** -0.5),
        'w_in_dsa': nrm((N_DSA_LAYERS, D_MODEL, DSA_IN), D_MODEL ** -0.5),
        'w_out_dsa': nrm((N_DSA_LAYERS, h_dsa, D_MODEL), DEEPNORM_BETA * h_dsa ** -0.5),
        'w_ffn_gate': nrm((N_GLA_LAYERS, D_MODEL, D_FF), D_MODEL ** -0.5),
        'w_ffn_up': nrm((N_GLA_LAYERS, D_MODEL, D_FF), D_MODEL ** -0.5),
        'w_ffn_down': nrm((N_GLA_LAYERS, D_FF, D_MODEL), DEEPNORM_BETA * D_FF ** -0.5),
        'w_router': nrm((N_DSA_LAYERS, D_MODEL, N_EXPERTS), D_MODEL ** -0.5),
        'w_exp_gate': nrm((N_DSA_LAYERS, N_EXPERTS, D_MODEL, D_FF), D_MODEL ** -0.5),
        'w_exp_up': nrm((N_DSA_LAYERS, N_EXPERTS, D_MODEL, D_FF), D_MODEL ** -0.5),
        'w_exp_down': nrm((N_DSA_LAYERS, N_EXPERTS, D_FF, D_MODEL), DEEPNORM_BETA * D_FF ** -0.5),
        'w_ple_proj': nrm((DEPTH, PLE_DIM, D_MODEL), PLE_DIM ** -0.5),
        'w_ple_gate': nrm((DEPTH, D_MODEL, D_MODEL), D_MODEL ** -0.5),
    }


def reference(x_prompt, x_sample, state_gla, cache_k, cache_v, cache_idx_k, page_table, p_prompt, p_sample,
              ln_mix_g, ln_mix_b, ln_ffn_g, ln_ffn_b, w_in_gla, w_alpha_up, b_alpha, gla_norm_g, w_out_gla,
              w_in_dsa, w_out_dsa, w_ffn_gate, w_ffn_up, w_ffn_down, w_router, w_exp_gate, w_exp_up,
              w_exp_down, w_ple_proj, w_ple_gate):
    xp, xs = x_prompt, x_sample
    gla_p, gla_s, kp, vp, ikp, ksm, vsm, iksm = [], [], [], [], [], [], [], []
    for i in range(DEPTH):
        j = i // 2
        if i % 2 == 0:
            gla_w = (w_in_gla[j], w_alpha_up[j], b_alpha[j], gla_norm_g[j], w_out_gla[j])
            s0 = jnp.zeros((xp.shape[0], GLA_HEADS, GLA_DK, GLA_DV), jnp.float32)
            mp, sp = gla_mixer(xp, s0, *gla_w)
            ms, ss = gla_mixer(xs, state_gla[j], *gla_w)
            gla_p.append(sp)
            gla_s.append(ss)
        else:
            mp, k_p, v_p, ik_p = dsa_prompt(xp, w_in_dsa[j], w_out_dsa[j])
            ms, k_s, v_s, ik_s = dsa_sample(xs, cache_k[j], cache_v[j], cache_idx_k[j], page_table,
                                            w_in_dsa[j], w_out_dsa[j])
            kp.append(k_p)
            vp.append(v_p)
            ikp.append(ik_p)
            ksm.append(k_s)
            vsm.append(v_s)
            iksm.append(ik_s)
        xp = layer_norm(DEEPNORM_ALPHA * xp + mp, ln_mix_g[i], ln_mix_b[i])
        xs = layer_norm(DEEPNORM_ALPHA * xs + ms, ln_mix_g[i], ln_mix_b[i])
        if i % 2 == 0:
            fp = swiglu(xp, w_ffn_gate[j], w_ffn_up[j], w_ffn_down[j])
            fs = swiglu(xs, w_ffn_gate[j], w_ffn_up[j], w_ffn_down[j])
        else:
            fp = moe_ffn(xp, w_router[j], w_exp_gate[j], w_exp_up[j], w_exp_down[j])
            fs = moe_ffn(xs, w_router[j], w_exp_gate[j], w_exp_up[j], w_exp_down[j])
        xp = layer_norm(DEEPNORM_ALPHA * xp + fp, ln_ffn_g[i], ln_ffn_b[i])
        xs = layer_norm(DEEPNORM_ALPHA * xs + fs, ln_ffn_g[i], ln_ffn_b[i])
        xp = per_layer_embed(xp, p_prompt[i], w_ple_proj[i], w_ple_gate[i])
        xs = per_layer_embed(xs, p_sample[i], w_ple_proj[i], w_ple_gate[i])
    return (xp, xs, jnp.stack(gla_p), jnp.stack(gla_s), jnp.stack(kp), jnp.stack(vp), jnp.stack(ikp),
            jnp.stack(ksm), jnp.stack(vsm), jnp.stack(iksm))
```

```python
import functools

import jax
import jax.numpy as jnp
from jax import lax
from jax.experimental import pallas as pl
from jax.experimental.pallas import tpu as pltpu

F32 = jnp.float32
BF16 = jnp.bfloat16
I32 = jnp.int32

D_MODEL = 1024
LN_EPS = 1e-5
LANE = 128

GLA_HEADS = 4
GLA_DK = 128
GLA_DV = 256
GLA_GATE_RANK = 16
GLA_TAU = 16.0
GLA_CHUNK = 64
GLA_HK = GLA_HEADS * GLA_DK
GLA_HV = GLA_HEADS * GLA_DV
GLA_COLS = 2 * GLA_HK + 2 * GLA_HV + LANE

DSA_HEADS = 8
DSA_HEAD_DIM = 128
DSA_KV_HEADS = 2
DSA_GROUP = DSA_HEADS // DSA_KV_HEADS
DSA_HKV = DSA_KV_HEADS * DSA_HEAD_DIM
IDX_HEADS = 8
IDX_DIM = 64
TOPK_MAX = 256
PAGE_SIZE = 128
DSA_Q0 = 0
DSA_IQ0 = DSA_HEADS * DSA_HEAD_DIM
DSA_K0 = DSA_IQ0 + IDX_HEADS * LANE
DSA_V0 = DSA_K0 + DSA_HKV
DSA_IKW0 = DSA_V0 + DSA_HKV
DSA_COLS = DSA_IKW0 + LANE

N_EXPERTS = 8

NEG_INF_KEY = -2139095041
INT_MIN = -2147483648
MASKED = -1e30


def _cparams(sem, vmem_mb):
    return pltpu.CompilerParams(dimension_semantics=sem, vmem_limit_bytes=vmem_mb * 2 ** 20)


def _dot(a, b):
    return jnp.dot(a, b, preferred_element_type=F32)


def _dot_nt(a, b):
    return lax.dot_general(a, b, (((1,), (1,)), ((), ())), preferred_element_type=F32)


def _dot_tn(a, b):
    return lax.dot_general(a, b, (((0,), (0,)), ((), ())), preferred_element_type=F32)


def _sigmoid(x):
    return 1.0 / (1.0 + jnp.exp(-x))


def _layer_norm(y, g, b):
    mu = jnp.mean(y, axis=-1, keepdims=True)
    yc = y - mu
    var = jnp.mean(yc * yc, axis=-1, keepdims=True)
    return yc * lax.rsqrt(var + LN_EPS) * g + b


def _row_tile(n, pref):
    t = min(n, pref)
    assert n % t == 0
    return t


def _proj_kernel(x_ref, w_ref, o_ref, ob_ref, *, tn):
    xb = x_ref[...].astype(BF16)
    for c in range(w_ref.shape[1] // tn):
        r = _dot(xb, w_ref[:, c * tn:(c + 1) * tn])
        o_ref[:, c * tn:(c + 1) * tn] = r
        ob_ref[:, c * tn:(c + 1) * tn] = r.astype(BF16)


def _proj(x, w, *, tn):
    n, d = x.shape
    cols = w.shape[1]
    assert cols % tn == 0
    tm = _row_tile(n, 512)
    return pl.pallas_call(
        functools.partial(_proj_kernel, tn=tn),
        grid=(n // tm,),
        in_specs=[pl.BlockSpec((tm, d), lambda i: (i, 0)), pl.BlockSpec((d, cols), lambda i: (0, 0))],
        out_specs=[pl.BlockSpec((tm, cols), lambda i: (i, 0)), pl.BlockSpec((tm, cols), lambda i: (i, 0))],
        out_shape=[jax.ShapeDtypeStruct((n, cols), F32), jax.ShapeDtypeStruct((n, cols), BF16)],
        compiler_params=_cparams(("arbitrary",), 56),
        name="proj",
    )(x, w)


def _mm_res_ln_kernel(x_ref, a_ref, w_ref, g_ref, b_ref, o_ref, *, alpha):
    m = _dot(a_ref[...].astype(BF16), w_ref[...])
    o_ref[...] = _layer_norm(alpha * x_ref[...] + m, g_ref[...], b_ref[...])


def _mm_res_ln(x, a, w, g, b, *, alpha):
    n, d = x.shape
    ka = a.shape[1]
    tm = _row_tile(n, 512)
    row = lambda i: (i, 0)
    fixed = lambda i: (0, 0)
    return pl.pallas_call(
        functools.partial(_mm_res_ln_kernel, alpha=alpha),
        grid=(n // tm,),
        in_specs=[pl.BlockSpec((tm, d), row), pl.BlockSpec((tm, ka), row), pl.BlockSpec((ka, d), fixed),
                  pl.BlockSpec((1, d), fixed), pl.BlockSpec((1, d), fixed)],
        out_specs=pl.BlockSpec((tm, d), row),
        out_shape=jax.ShapeDtypeStruct((n, d), F32),
        compiler_params=_cparams(("arbitrary",), 40),
        name="mm_res_ln",
    )(x, a, w, g, b)


def _ffn_ln_kernel(x_ref, wg_ref, wu_ref, wd_ref, g_ref, b_ref, o_ref, xb_ref, acc_ref, *, alpha):
    f = pl.program_id(1)

    @pl.when(f == 0)
    def _():
        xb_ref[...] = x_ref[...].astype(BF16)
        acc_ref[...] = jnp.zeros_like(acc_ref)

    xb = xb_ref[...]
    hg = _dot(xb, wg_ref[...])
    hu = _dot(xb, wu_ref[...])
    h = (hg * _sigmoid(hg) * hu).astype(BF16)
    acc_ref[...] += _dot(h, wd_ref[...])

    @pl.when(f == pl.num_programs(1) - 1)
    def _():
        o_ref[...] = _layer_norm(alpha * x_ref[...] + acc_ref[...], g_ref[...], b_ref[...])


def _ffn_ln(x, wg, wu, wd, g, b, *, alpha, tf=512):
    n, d = x.shape
    ff = wg.shape[1]
    assert ff % tf == 0
    tm = _row_tile(n, 1024)
    return pl.pallas_call(
        functools.partial(_ffn_ln_kernel, alpha=alpha),
        grid=(n // tm, ff // tf),
        in_specs=[pl.BlockSpec((tm, d), lambda i, f: (i, 0)),
                  pl.BlockSpec((d, tf), lambda i, f: (0, f)),
                  pl.BlockSpec((d, tf), lambda i, f: (0, f)),
                  pl.BlockSpec((tf, d), lambda i, f: (f, 0)),
                  pl.BlockSpec((1, d), lambda i, f: (0, 0)),
                  pl.BlockSpec((1, d), lambda i, f: (0, 0))],
        out_specs=pl.BlockSpec((tm, d), lambda i, f: (i, 0)),
        out_shape=jax.ShapeDtypeStruct((n, d), F32),
        scratch_shapes=[pltpu.VMEM((tm, d), BF16), pltpu.VMEM((tm, d), F32)],
        compiler_params=_cparams(("arbitrary", "arbitrary"), 48),
        name="ffn_ln",
    )(x, wg, wu, wd, g, b)


def _ple_kernel(x_ref, p_ref, wg_ref, wp_ref, o_ref):
    x = x_ref[...]
    gate = _sigmoid(_dot(x.astype(BF16), wg_ref[...]))
    o_ref[...] = x + gate * _dot(p_ref[...].astype(BF16), wp_ref[...])


def _ple(x, p, wg, wp):
    n, d = x.shape
    pd = p.shape[1]
    tm = _row_tile(n, 512)
    row = lambda i: (i, 0)
    fixed = lambda i: (0, 0)
    return pl.pallas_call(
        _ple_kernel,
        grid=(n // tm,),
        in_specs=[pl.BlockSpec((tm, d), row), pl.BlockSpec((tm, pd), row),
                  pl.BlockSpec((d, d), fixed), pl.BlockSpec((pd, d), fixed)],
        out_specs=pl.BlockSpec((tm, d), row),
        out_shape=jax.ShapeDtypeStruct((n, d), F32),
        compiler_params=_cparams(("arbitrary",), 40),
        name="ple",
    )(x, p, wg, wp)


def _gla_kernel(q_ref, k_ref, v_ref, g_ref, a_ref, wup_ref, ba_ref, ng_ref, s0_ref, o_ref, st_out_ref, st_ref,
                *, c, valid):
    t = pl.program_id(1)

    @pl.when(t == 0)
    def _():
        st_ref[...] = s0_ref[0]

    row = lax.broadcasted_iota(I32, (c, c), 0)
    col = lax.broadcasted_iota(I32, (c, c), 1)
    causal = row >= col
    logit = _dot(a_ref[0].astype(BF16), wup_ref[...]) + ba_ref[...]
    log_a = (jnp.minimum(logit, 0.0) - jnp.log(1.0 + jnp.exp(-jnp.abs(logit)))) * (1.0 / GLA_TAU)
    if valid < c:
        log_a = jnp.where(lax.broadcasted_iota(I32, log_a.shape, 0) < valid, log_a, 0.0)
    bcum = jnp.dot(causal.astype(F32), log_a, preferred_element_type=F32, precision=lax.Precision.HIGHEST)
    blast = bcum[c - 1:c, :]
    q = q_ref[0] * (GLA_DK ** -0.5)
    k = k_ref[0]
    q_dec = (q * jnp.exp(bcum)).astype(BF16)
    k_inv = (k * jnp.exp(-bcum)).astype(BF16)
    k_dec = (k * jnp.exp(blast - bcum)).astype(BF16)
    decay = jnp.exp(blast)
    v = v_ref[0].astype(BF16)
    for h in range(GLA_HEADS):
        sk = slice(h * GLA_DK, (h + 1) * GLA_DK)
        sv = slice(h * GLA_DV, (h + 1) * GLA_DV)
        att = jnp.where(causal, _dot_nt(q_dec[:, sk], k_inv[:, sk]), 0.0).astype(BF16)
        st = st_ref[h]
        o = _dot(att, v[:, sv]) + _dot_nt(q_dec[:, sk], st.astype(BF16))
        st_ref[h] = st * decay[:, sk] + _dot_tn(v[:, sv], k_dec[:, sk])
        mu = jnp.mean(o, axis=-1, keepdims=True)
        oc = o - mu
        var = jnp.mean(oc * oc, axis=-1, keepdims=True)
        on = oc * lax.rsqrt(var + LN_EPS) * ng_ref[:, sv]
        gg = g_ref[0, :, sv]
        o_ref[0, :, sv] = (on * (gg * _sigmoid(gg))).astype(BF16)

    @pl.when(t == pl.num_programs(1) - 1)
    def _():
        st_out_ref[0] = st_ref[...]


def _gla_core(h, s0_t, wup, ba, ng, *, c, valid):
    bsz, tlen, _ = h.shape
    assert tlen % c == 0
    v_blk = (2 * GLA_HK) // GLA_HV
    return pl.pallas_call(
        functools.partial(_gla_kernel, c=c, valid=valid),
        grid=(bsz, tlen // c),
        in_specs=[pl.BlockSpec((1, c, GLA_HK), lambda b, t: (b, t, 0)),
                  pl.BlockSpec((1, c, GLA_HK), lambda b, t: (b, t, 1)),
                  pl.BlockSpec((1, c, GLA_HV), lambda b, t: (b, t, v_blk)),
                  pl.BlockSpec((1, c, GLA_HV), lambda b, t: (b, t, v_blk + 1)),
                  pl.BlockSpec((1, c, LANE), lambda b, t: (b, t, (2 * GLA_HK + 2 * GLA_HV) // LANE)),
                  pl.BlockSpec((LANE, GLA_HK), lambda b, t: (0, 0)),
                  pl.BlockSpec((1, GLA_HK), lambda b, t: (0, 0)),
                  pl.BlockSpec((1, GLA_HV), lambda b, t: (0, 0)),
                  pl.BlockSpec((1, GLA_HEADS, GLA_DV, GLA_DK), lambda b, t: (b, 0, 0, 0))],
        out_specs=[pl.BlockSpec((1, c, GLA_HV), lambda b, t: (b, t, 0)),
                   pl.BlockSpec((1, GLA_HEADS, GLA_DV, GLA_DK), lambda b, t: (b, 0, 0, 0))],
        out_shape=[jax.ShapeDtypeStruct((bsz, tlen, GLA_HV), BF16),
                   jax.ShapeDtypeStruct((bsz, GLA_HEADS, GLA_DV, GLA_DK), F32)],
        scratch_shapes=[pltpu.VMEM((GLA_HEADS, GLA_DV, GLA_DK), F32)],
        compiler_params=_cparams(("arbitrary", "arbitrary"), 32),
        name="gla_core",
    )(h, h, h, h, h, wup, ba, ng, s0_t)


def _sortable(s):
    bits = lax.bitcast_convert_type(s, I32)
    return bits ^ (lax.shift_right_arithmetic(bits, 31) & 0x7FFFFFFF)


def _lane_fold(m):
    acc = m[:, 0:LANE]
    for i in range(1, m.shape[1] // LANE):
        acc = acc + m[:, i * LANE:(i + 1) * LANE]
    return acc


def _kth_largest_key(count_ge, rows, k):
    kf = float(k)
    thr0 = jnp.where(count_ge(jnp.zeros((rows, 1), I32)) >= kf, 0, INT_MIN).astype(I32)

    def bit_body(i, thr):
        cand = thr | lax.shift_left(jnp.int32(1), 30 - i)
        return jnp.where(count_ge(cand) >= kf, cand, thr)

    return lax.fori_loop(0, 31, bit_body, thr0)


def _strict_upper(n):
    return (lax.broadcasted_iota(I32, (n, n), 0) < lax.broadcasted_iota(I32, (n, n), 1)).astype(BF16)


def _dsa_prompt_kernel(q_ref, iq_ref, iw_ref, ikw_ref, k_ref, v_ref, o_ref,
                       keys_ref, m_ref, l_ref, acc_ref, *, tq, tk, topk):
    j = pl.program_id(1)
    nkt = ((j + 1) * tq + tk - 1) // tk
    iq_stack = jnp.concatenate([iq_ref[0, :, h * LANE:(h + 1) * LANE] for h in range(IDX_HEADS)], axis=0)
    ws = iw_ref[0][:, IDX_DIM:IDX_DIM + IDX_HEADS] * (IDX_HEADS ** -0.5)
    ws_cols = [ws[:, h:h + 1] for h in range(IDX_HEADS)]
    qpos = j * tq + lax.broadcasted_iota(I32, (tq, tk), 0)
    lane = lax.broadcasted_iota(I32, (tq, tk), 1)

    def score_body(kt, carry):
        ikt = ikw_ref[0, pl.ds(pl.multiple_of(kt * tk, tk), tk), :]
        d = _dot_nt(iq_stack, ikt)
        s = jnp.zeros((tq, tk), F32)
        for h in range(IDX_HEADS):
            s = s + ws_cols[h] * jnp.maximum(d[h * tq:(h + 1) * tq] * (IDX_DIM ** -0.5), 0.0)
        s = s + 0.0
        s = jnp.where(kt * tk + lane <= qpos, s, -jnp.inf)
        keys_ref[kt] = _sortable(s)
        return carry

    lax.fori_loop(0, nkt, score_body, 0)

    def counts(pred):
        def body(kt, acc):
            return acc + _lane_fold(jnp.where(pred(keys_ref[kt]), 1.0, 0.0))
        return jnp.sum(lax.fori_loop(0, nkt, body, jnp.zeros((tq, LANE), F32)), axis=1, keepdims=True)

    thr = _kth_largest_key(lambda cand: counts(lambda kk: kk >= cand), tq, topk)
    n_gt = counts(lambda kk: kk > thr)
    n_eq = counts(lambda kk: kk == thr)
    need = float(topk) - n_gt
    tie_row = jnp.logical_and(n_eq > need, thr > NEG_INF_KEY)
    any_tie = jnp.max(jnp.where(tie_row, 1.0, 0.0)) > 0.0

    @pl.when(jnp.logical_not(any_tie))
    def _():
        thr_eff = jnp.maximum(thr, NEG_INF_KEY + 1)

        def body(kt, carry):
            keys_ref[kt] = jnp.where(keys_ref[kt] >= thr_eff, 1, 0).astype(I32)
            return carry
        lax.fori_loop(0, nkt, body, 0)

    @pl.when(any_tie)
    def _():
        upper = _strict_upper(tk)

        def body(kt, seen):
            kk = keys_ref[kt]
            eq = kk == thr
            before = seen + _dot(jnp.where(eq, 1.0, 0.0).astype(BF16), upper)
            sel = jnp.logical_and(kk > NEG_INF_KEY, jnp.logical_or(kk > thr, jnp.logical_and(eq, before < need)))
            keys_ref[kt] = jnp.where(sel, 1, 0).astype(I32)
            return seen + jnp.sum(jnp.where(eq, 1.0, 0.0), axis=1, keepdims=True)
        lax.fori_loop(0, nkt, body, jnp.zeros((tq, 1), F32))

    gq = DSA_GROUP * tq
    q_groups = [jnp.concatenate([q_ref[0, :, (c * DSA_GROUP + g) * DSA_HEAD_DIM:(c * DSA_GROUP + g + 1) * DSA_HEAD_DIM]
                                 for g in range(DSA_GROUP)], axis=0) for c in range(DSA_KV_HEADS)]
    m_ref[...] = jnp.full(m_ref.shape, MASKED, F32)
    l_ref[...] = jnp.zeros_like(l_ref)
    acc_ref[...] = jnp.zeros_like(acc_ref)

    def attend_body(kt, carry):
        rows = pl.ds(pl.multiple_of(kt * tk, tk), tk)
        sel = keys_ref[kt] != 0
        for c in range(DSA_KV_HEADS):
            kc = k_ref[0, rows, c * DSA_HEAD_DIM:(c + 1) * DSA_HEAD_DIM]
            vc = v_ref[0, rows, c * DSA_HEAD_DIM:(c + 1) * DSA_HEAD_DIM]
            logits = _dot_nt(q_groups[c], kc) * (DSA_HEAD_DIM ** -0.5)
            logits = logits.reshape(DSA_GROUP, tq, tk)
            masked = jnp.where(sel[None], logits, MASKED).reshape(gq, tk)
            m_old = m_ref[c]
            m_new = jnp.maximum(m_old, jnp.max(masked, axis=1, keepdims=True))
            p = jnp.where(sel[None], jnp.exp(masked - m_new).reshape(DSA_GROUP, tq, tk), 0.0).reshape(gq, tk)
            alpha = jnp.exp(m_old - m_new)
            l_ref[c] = alpha * l_ref[c] + jnp.sum(p, axis=1, keepdims=True)
            acc_ref[c] = alpha * acc_ref[c] + _dot(p.astype(BF16), vc)
            m_ref[c] = m_new
        return carry

    lax.fori_loop(0, nkt, attend_body, 0)
    for c in range(DSA_KV_HEADS):
        oc = acc_ref[c] / l_ref[c]
        for g in range(DSA_GROUP):
            hh = c * DSA_GROUP + g
            o_ref[0, :, hh * DSA_HEAD_DIM:(hh + 1) * DSA_HEAD_DIM] = oc[g * tq:(g + 1) * tq].astype(BF16)


def _dsa_prompt(hf, hb, *, tq=128, tk=512):
    bsz, s, _ = hf.shape
    tk = min(tk, s)
    assert s % tq == 0 and s % tk == 0 and tk % tq == 0
    topk = min(TOPK_MAX, s // 4)
    hq = DSA_HEADS * DSA_HEAD_DIM
    gq = DSA_GROUP * tq
    return pl.pallas_call(
        functools.partial(_dsa_prompt_kernel, tq=tq, tk=tk, topk=topk),
        grid=(bsz, s // tq),
        in_specs=[pl.BlockSpec((1, tq, hq), lambda b, j: (b, j, DSA_Q0 // hq)),
                  pl.BlockSpec((1, tq, IDX_HEADS * LANE), lambda b, j: (b, j, DSA_IQ0 // (IDX_HEADS * LANE))),
                  pl.BlockSpec((1, tq, LANE), lambda b, j: (b, j, DSA_IKW0 // LANE)),
                  pl.BlockSpec((1, s, LANE), lambda b, j: (b, 0, DSA_IKW0 // LANE)),
                  pl.BlockSpec((1, s, DSA_HKV), lambda b, j: (b, 0, DSA_K0 // DSA_HKV)),
                  pl.BlockSpec((1, s, DSA_HKV), lambda b, j: (b, 0, DSA_V0 // DSA_HKV))],
        out_specs=pl.BlockSpec((1, tq, hq), lambda b, j: (b, j, 0)),
        out_shape=jax.ShapeDtypeStruct((bsz, s, hq), BF16),
        scratch_shapes=[pltpu.VMEM((s // tk, tq, tk), I32),
                        pltpu.VMEM((DSA_KV_HEADS, gq, 1), F32),
                        pltpu.VMEM((DSA_KV_HEADS, gq, 1), F32),
                        pltpu.VMEM((DSA_KV_HEADS, gq, DSA_HEAD_DIM), F32)],
        compiler_params=_cparams(("arbitrary", "arbitrary"), 56),
        name="dsa_prompt",
    )(hb, hb, hf, hb, hb, hb)


def _sample_ws(iw_ref):
    return iw_ref[0][:, IDX_DIM:IDX_DIM + IDX_HEADS] * (IDX_HEADS ** -0.5)


def _sample_scores(d, ws, t):
    s = jnp.zeros((t, d.shape[1]), F32)
    for h in range(IDX_HEADS):
        s = s + ws[:, h:h + 1] * jnp.maximum(d[h * t:(h + 1) * t] * (IDX_DIM ** -0.5), 0.0)
    return s + 0.0


def _dsa_sample_scores_kernel(pt_ref, iq_ref, iw_ref, *refs, pp, t):
    page_refs, o_ref = refs[:pp], refs[pp]
    iq_stack = jnp.concatenate([iq_ref[0, :, h * LANE:h * LANE + IDX_DIM] for h in range(IDX_HEADS)], axis=0)
    ik = jnp.concatenate([r[0].astype(BF16) for r in page_refs], axis=0)
    o_ref[0] = _sample_scores(_dot_nt(iq_stack.astype(BF16), ik), _sample_ws(iw_ref), t)


def _dsa_sample_scores(hf, cache_ik, page_table, *, pp):
    bsz, t, _ = hf.shape
    n_pages = page_table.shape[1]
    assert n_pages % pp == 0
    iq_w = IDX_HEADS * LANE

    def page_spec(i):
        return pl.BlockSpec((1, PAGE_SIZE, IDX_DIM), lambda b, j, pt: (pt[b, j * pp + i], 0, 0))

    grid_spec = pltpu.PrefetchScalarGridSpec(
        num_scalar_prefetch=1, grid=(bsz, n_pages // pp),
        in_specs=[pl.BlockSpec((1, t, iq_w), lambda b, j, pt: (b, 0, DSA_IQ0 // iq_w)),
                  pl.BlockSpec((1, t, LANE), lambda b, j, pt: (b, 0, DSA_IKW0 // LANE))]
                 + [page_spec(i) for i in range(pp)],
        out_specs=pl.BlockSpec((1, t, pp * PAGE_SIZE), lambda b, j, pt: (b, 0, j)))
    return pl.pallas_call(
        functools.partial(_dsa_sample_scores_kernel, pp=pp, t=t),
        grid_spec=grid_spec,
        out_shape=jax.ShapeDtypeStruct((bsz, t, n_pages * PAGE_SIZE), F32),
        compiler_params=_cparams(("arbitrary", "arbitrary"), 32),
        name="dsa_sample_scores",
    )(page_table, hf, hf, *([cache_ik] * pp))


def _dsa_sample_select_kernel(sp_ref, iq_ref, iw_ref, ikw_ref, o_ref, keys_ref, *, t, past, new_w, topk, ck):
    iq_stack = jnp.concatenate([iq_ref[0, :, h * LANE:(h + 1) * LANE] for h in range(IDX_HEADS)], axis=0)
    ik_new = jnp.concatenate([ikw_ref[0], jnp.zeros((new_w - t, LANE), F32)], axis=0)
    d_new = _dot_nt(iq_stack.astype(BF16), ik_new.astype(BF16))
    s_new = _sample_scores(d_new, _sample_ws(iw_ref), t)
    row = lax.broadcasted_iota(I32, (t, new_w), 0)
    lane = lax.broadcasted_iota(I32, (t, new_w), 1)
    s_new = jnp.where(lane <= row, s_new, -jnp.inf)
    keys_ref[:, 0:past] = _sortable(sp_ref[0])
    keys_ref[:, past:past + new_w] = _sortable(s_new)
    n_ck = (past + new_w) // ck

    def counts(pred):
        acc = jnp.zeros((t, LANE), F32)
        for i in range(n_ck):
            acc = acc + _lane_fold(jnp.where(pred(keys_ref[:, i * ck:(i + 1) * ck]), 1.0, 0.0))
        return jnp.sum(acc, axis=1, keepdims=True)

    thr = _kth_largest_key(lambda cand: counts(lambda kk: kk >= cand), t, topk)
    n_gt = counts(lambda kk: kk > thr)
    n_eq = counts(lambda kk: kk == thr)
    need = float(topk) - n_gt
    tie_row = jnp.logical_and(n_eq > need, thr > NEG_INF_KEY)
    any_tie = jnp.max(jnp.where(tie_row, 1.0, 0.0)) > 0.0

    @pl.when(jnp.logical_not(any_tie))
    def _():
        thr_eff = jnp.maximum(thr, NEG_INF_KEY + 1)
        o_ref[0] = jnp.where(keys_ref[...] >= thr_eff, 1.0, 0.0)

    @pl.when(any_tie)
    def _():
        upper = _strict_upper(ck)
        seen = jnp.zeros((t, 1), F32)
        for i in range(n_ck):
            kk = keys_ref[:, i * ck:(i + 1) * ck]
            eq = kk == thr
            before = seen + _dot(jnp.where(eq, 1.0, 0.0).astype(BF16), upper)
            sel = jnp.logical_and(kk > NEG_INF_KEY, jnp.logical_or(kk > thr, jnp.logical_and(eq, before < need)))
            o_ref[0, :, i * ck:(i + 1) * ck] = jnp.where(sel, 1.0, 0.0)
            seen = seen + jnp.sum(jnp.where(eq, 1.0, 0.0), axis=1, keepdims=True)


def _dsa_sample_select(scores_past, hf, *, new_w=512, ck=512):
    bsz, t, past = scores_past.shape
    assert past % ck == 0 and new_w % ck == 0
    topk = min(TOPK_MAX, (past + t) // 4)
    iq_w = IDX_HEADS * LANE
    total = past + new_w
    return pl.pallas_call(
        functools.partial(_dsa_sample_select_kernel, t=t, past=past, new_w=new_w, topk=topk, ck=ck),
        grid=(bsz,),
        in_specs=[pl.BlockSpec((1, t, past), lambda b: (b, 0, 0)),
                  pl.BlockSpec((1, t, iq_w), lambda b: (b, 0, DSA_IQ0 // iq_w)),
                  pl.BlockSpec((1, t, LANE), lambda b: (b, 0, DSA_IKW0 // LANE)),
                  pl.BlockSpec((1, t, LANE), lambda b: (b, 0, DSA_IKW0 // LANE))],
        out_specs=pl.BlockSpec((1, t, total), lambda b: (b, 0, 0)),
        out_shape=jax.ShapeDtypeStruct((bsz, t, total), F32),
        scratch_shapes=[pltpu.VMEM((t, total), I32)],
        compiler_params=_cparams(("arbitrary",), 32),
        name="dsa_sample_select",
    )(scores_past, hf, hf, hf)


def _dsa_sample_attend_kernel(pt_ref, q_ref, kn_ref, vn_ref, selp_ref, seln_ref, *refs, pp, t, new_w):
    k_pages, v_pages = refs[:pp], refs[pp:2 * pp]
    o_ref, m_ref, l_ref, acc_ref = refs[2 * pp:]
    j = pl.program_id(1)
    gt = DSA_GROUP * t

    @pl.when(j == 0)
    def _():
        m_ref[...] = jnp.full(m_ref.shape, MASKED, F32)
        l_ref[...] = jnp.zeros_like(l_ref)
        acc_ref[...] = jnp.zeros_like(acc_ref)

    q_groups = [jnp.concatenate([q_ref[0, :, (c * DSA_GROUP + g) * DSA_HEAD_DIM:(c * DSA_GROUP + g + 1) * DSA_HEAD_DIM]
                                 for g in range(DSA_GROUP)], axis=0).astype(BF16) for c in range(DSA_KV_HEADS)]

    def attend(kall, vall, sel):
        n = kall.shape[0]
        for c in range(DSA_KV_HEADS):
            kc = kall[:, c * DSA_HEAD_DIM:(c + 1) * DSA_HEAD_DIM]
            vc = vall[:, c * DSA_HEAD_DIM:(c + 1) * DSA_HEAD_DIM]
            logits = (_dot_nt(q_groups[c], kc) * (DSA_HEAD_DIM ** -0.5)).reshape(DSA_GROUP, t, n)
            masked = jnp.where(sel[None], logits, MASKED).reshape(gt, n)
            m_old = m_ref[c]
            m_new = jnp.maximum(m_old, jnp.max(masked, axis=1, keepdims=True))
            p = jnp.where(sel[None], jnp.exp(masked - m_new).reshape(DSA_GROUP, t, n), 0.0).reshape(gt, n)
            alpha = jnp.exp(m_old - m_new)
            l_ref[c] = alpha * l_ref[c] + jnp.sum(p, axis=1, keepdims=True)
            acc_ref[c] = alpha * acc_ref[c] + _dot(p.astype(BF16), vc)
            m_ref[c] = m_new

    attend(jnp.concatenate([r[0].astype(BF16) for r in k_pages], axis=0),
           jnp.concatenate([r[0].astype(BF16) for r in v_pages], axis=0),
           selp_ref[0] != 0.0)

    @pl.when(j == pl.num_programs(1) - 1)
    def _():
        pad = jnp.zeros((new_w - t, DSA_HKV), F32)
        attend(jnp.concatenate([kn_ref[0], pad], axis=0).astype(BF16),
               jnp.concatenate([vn_ref[0], pad], axis=0).astype(BF16), seln_ref[0] != 0.0)
        for c in range(DSA_KV_HEADS):
            oc = acc_ref[c] / l_ref[c]
            for g in range(DSA_GROUP):
                hh = c * DSA_GROUP + g
                o_ref[0, :, hh * DSA_HEAD_DIM:(hh + 1) * DSA_HEAD_DIM] = oc[g * t:(g + 1) * t].astype(BF16)


def _dsa_sample_attend(hf, sel, cache_k, cache_v, page_table, *, pp, new_w=512):
    bsz, t, _ = hf.shape
    n_pages = page_table.shape[1]
    past = n_pages * PAGE_SIZE
    assert n_pages % pp == 0 and past % new_w == 0
    hq = DSA_HEADS * DSA_HEAD_DIM
    gt = DSA_GROUP * t

    def page_spec(i):
        return pl.BlockSpec((1, PAGE_SIZE, DSA_HKV), lambda b, j, pt: (pt[b, j * pp + i], 0, 0))

    grid_spec = pltpu.PrefetchScalarGridSpec(
        num_scalar_prefetch=1, grid=(bsz, n_pages // pp),
        in_specs=[pl.BlockSpec((1, t, hq), lambda b, j, pt: (b, 0, DSA_Q0 // hq)),
                  pl.BlockSpec((1, t, DSA_HKV), lambda b, j, pt: (b, 0, DSA_K0 // DSA_HKV)),
                  pl.BlockSpec((1, t, DSA_HKV), lambda b, j, pt: (b, 0, DSA_V0 // DSA_HKV)),
                  pl.BlockSpec((1, t, pp * PAGE_SIZE), lambda b, j, pt: (b, 0, j)),
                  pl.BlockSpec((1, t, new_w), lambda b, j, pt: (b, 0, past // new_w))]
                 + [page_spec(i) for i in range(pp)] * 2,
        out_specs=pl.BlockSpec((1, t, hq), lambda b, j, pt: (b, 0, 0)),
        scratch_shapes=[pltpu.VMEM((DSA_KV_HEADS, gt, 1), F32),
                        pltpu.VMEM((DSA_KV_HEADS, gt, 1), F32),
                        pltpu.VMEM((DSA_KV_HEADS, gt, DSA_HEAD_DIM), F32)])
    return pl.pallas_call(
        functools.partial(_dsa_sample_attend_kernel, pp=pp, t=t, new_w=new_w),
        grid_spec=grid_spec,
        out_shape=jax.ShapeDtypeStruct((bsz, t, hq), BF16),
        compiler_params=_cparams(("arbitrary", "arbitrary"), 48),
        name="dsa_sample_attend",
    )(page_table, hf, hf, hf, sel, sel, *([cache_k] * pp), *([cache_v] * pp))


def _router_kernel(x_ref, w_ref, o_ref):
    x = x_ref[...]
    w = w_ref[...]
    xh = x.astype(BF16)
    xl = (x - xh.astype(F32)).astype(BF16)
    wh = w.astype(BF16)
    wl = (w - wh.astype(F32)).astype(BF16)
    logits = _dot(xh, wh) + (_dot(xh, wl) + _dot(xl, wh))
    lane = lax.broadcasted_iota(I32, logits.shape, 1)
    logits = jnp.where(lane < N_EXPERTS, logits, -jnp.inf)
    l1 = jnp.max(logits, axis=1, keepdims=True)
    i1 = jnp.min(jnp.where(logits == l1, lane, LANE), axis=1, keepdims=True)
    rest = jnp.where(lane == i1, -jnp.inf, logits)
    l2 = jnp.max(rest, axis=1, keepdims=True)
    i2 = jnp.min(jnp.where(rest == l2, lane, LANE), axis=1, keepdims=True)
    e2 = jnp.exp(l2 - l1)
    denom = 1.0 + e2
    o_ref[...] = jnp.where(lane == i1, 1.0 / denom, jnp.where(lane == i2, e2 / denom, 0.0))


def _router(x, w_pad):
    n, d = x.shape
    tm = _row_tile(n, 512)
    return pl.pallas_call(
        _router_kernel,
        grid=(n // tm,),
        in_specs=[pl.BlockSpec((tm, d), lambda i: (i, 0)), pl.BlockSpec((d, LANE), lambda i: (0, 0))],
        out_specs=pl.BlockSpec((tm, LANE), lambda i: (i, 0)),
        out_shape=jax.ShapeDtypeStruct((n, LANE), F32),
        compiler_params=_cparams(("arbitrary",), 32),
        name="router",
    )(x, w_pad)


def _moe_ln_kernel(cnt_ref, x_ref, gt_ref, wg_ref, wu_ref, wd_ref, g_ref, b_ref, o_ref, xb_ref, acc_ref, y_ref,
                   *, alpha):
    i, e, f = pl.program_id(0), pl.program_id(1), pl.program_id(2)
    last_f = f == pl.num_programs(2) - 1
    active = cnt_ref[i * N_EXPERTS + e] > 0

    @pl.when(jnp.logical_and(e == 0, f == 0))
    def _():
        xb_ref[...] = x_ref[...].astype(BF16)
        y_ref[...] = jnp.zeros_like(y_ref)

    @pl.when(jnp.logical_and(active, f == 0))
    def _():
        acc_ref[...] = jnp.zeros_like(acc_ref)

    @pl.when(active)
    def _():
        xb = xb_ref[...]
        hg = _dot(xb, wg_ref[0])
        hu = _dot(xb, wu_ref[0])
        h = (hg * _sigmoid(hg) * hu).astype(BF16)
        acc_ref[...] += _dot(h, wd_ref[0])

    @pl.when(jnp.logical_and(active, last_f))
    def _():
        gates = gt_ref[...]
        lane = lax.broadcasted_iota(I32, gates.shape, 1)
        ge = jnp.sum(jnp.where(lane == e, gates, 0.0), axis=1, keepdims=True)
        y_ref[...] += ge * acc_ref[...]

    @pl.when(jnp.logical_and(e == N_EXPERTS - 1, last_f))
    def _():
        o_ref[...] = _layer_norm(alpha * x_ref[...] + y_ref[...], g_ref[...], b_ref[...])


def _moe_ln(x, gates, wg, wu, wd, g, b, *, alpha, tf=512):
    n, d = x.shape
    ff = wg.shape[2]
    assert ff % tf == 0
    tm = _row_tile(n, 1024)
    counts = jnp.sum((gates[:, :N_EXPERTS] > 0.0).reshape(n // tm, tm, N_EXPERTS), axis=1).astype(I32).reshape(-1)
    grid_spec = pltpu.PrefetchScalarGridSpec(
        num_scalar_prefetch=1, grid=(n // tm, N_EXPERTS, ff // tf),
        in_specs=[pl.BlockSpec((tm, d), lambda i, e, f, c: (i, 0)),
                  pl.BlockSpec((tm, LANE), lambda i, e, f, c: (i, 0)),
                  pl.BlockSpec((1, d, tf), lambda i, e, f, c: (e, 0, f)),
                  pl.BlockSpec((1, d, tf), lambda i, e, f, c: (e, 0, f)),
                  pl.BlockSpec((1, tf, d), lambda i, e, f, c: (e, f, 0)),
                  pl.BlockSpec((1, d), lambda i, e, f, c: (0, 0)),
                  pl.BlockSpec((1, d), lambda i, e, f, c: (0, 0))],
        out_specs=pl.BlockSpec((tm, d), lambda i, e, f, c: (i, 0)),
        scratch_shapes=[pltpu.VMEM((tm, d), BF16), pltpu.VMEM((tm, d), F32), pltpu.VMEM((tm, d), F32)])
    return pl.pallas_call(
        functools.partial(_moe_ln_kernel, alpha=alpha),
        grid_spec=grid_spec,
        out_shape=jax.ShapeDtypeStruct((n, d), F32),
        compiler_params=_cparams(("arbitrary", "arbitrary", "arbitrary"), 56),
        name="moe_ln",
    )(counts, x, gates, wg, wu, wd, g, b)


def _pad_cols(w, cols):
    return jnp.pad(w, ((0, 0), (0, cols - w.shape[1])))


def _gla_layer(x, s0, w_in, w_up, b_a, norm_g, w_out, ln_g, ln_b, *, alpha):
    bsz, t, d = x.shape
    x2 = x.reshape(bsz * t, d)
    hf, _ = _proj(x2, _pad_cols(w_in, GLA_COLS).astype(BF16), tn=640)
    hf = hf.reshape(bsz, t, GLA_COLS)
    c = GLA_CHUNK if t % GLA_CHUNK == 0 else t
    valid = c
    if c < GLA_CHUNK:
        hf = jnp.pad(hf, ((0, 0), (0, GLA_CHUNK - c), (0, 0)))
        c = GLA_CHUNK
    w_up_pad = jnp.pad(w_up, ((0, LANE - GLA_GATE_RANK), (0, 0))).astype(BF16)
    o, st = _gla_core(hf, jnp.swapaxes(s0, 2, 3), w_up_pad, b_a.reshape(1, GLA_HK), norm_g.reshape(1, GLA_HV),
                      c=c, valid=valid)
    o = o[:, :t].reshape(bsz * t, GLA_HV)
    y = _mm_res_ln(x2, o, w_out.astype(BF16), ln_g.reshape(1, d), ln_b.reshape(1, d), alpha=alpha)
    return y.reshape(bsz, t, d), jnp.swapaxes(st, 2, 3)


def _dsa_weight(w_in):
    hq = DSA_HEADS * DSA_HEAD_DIM
    iq0 = hq + 2 * DSA_HKV
    ik0 = iq0 + IDX_HEADS * IDX_DIM
    w_iq = w_in[:, iq0:ik0].reshape(-1, IDX_HEADS, IDX_DIM)
    w_iq = jnp.pad(w_iq, ((0, 0), (0, 0), (0, LANE - IDX_DIM))).reshape(-1, IDX_HEADS * LANE)
    w = jnp.concatenate([w_in[:, :hq], w_iq, w_in[:, hq:iq0], _pad_cols(w_in[:, ik0:], LANE)], axis=1)
    return w.astype(BF16)


def _dsa_project(x, w_in):
    bsz, t, d = x.shape
    hf, hb = _proj(x.reshape(bsz * t, d), _dsa_weight(w_in), tn=DSA_COLS // 3)
    hf = hf.reshape(bsz, t, DSA_COLS)
    k = hf[:, :, DSA_K0:DSA_V0].reshape(bsz, t, DSA_KV_HEADS, DSA_HEAD_DIM)
    v = hf[:, :, DSA_V0:DSA_IKW0].reshape(bsz, t, DSA_KV_HEADS, DSA_HEAD_DIM)
    ik = hf[:, :, DSA_IKW0:DSA_IKW0 + IDX_DIM]
    return hf, hb.reshape(bsz, t, DSA_COLS), k, v, ik


def kernel(x_prompt, x_sample, state_gla, cache_k, cache_v, cache_idx_k, page_table, p_prompt, p_sample,
           ln_mix_g, ln_mix_b, ln_ffn_g, ln_ffn_b, w_in_gla, w_alpha_up, b_alpha, gla_norm_g, w_out_gla,
           w_in_dsa, w_out_dsa, w_ffn_gate, w_ffn_up, w_ffn_down, w_router, w_exp_gate, w_exp_up,
           w_exp_down, w_ple_proj, w_ple_gate):
    depth = ln_mix_g.shape[0]
    alpha = (2.0 * depth) ** 0.25
    d = x_prompt.shape[-1]
    xp, xs = x_prompt, x_sample
    gla_p, gla_s, kp, vp, ikp, ksm, vsm, iksm = [], [], [], [], [], [], [], []
    n_pages = page_table.shape[1]
    pp = 16 if n_pages % 16 == 0 else n_pages
    for i in range(depth):
        j = i // 2
        lg, lb = ln_mix_g[i], ln_mix_b[i]
        if i % 2 == 0:
            gw = (w_in_gla[j], w_alpha_up[j], b_alpha[j], gla_norm_g[j], w_out_gla[j], lg, lb)
            s0 = jnp.zeros((xp.shape[0], GLA_HEADS, GLA_DK, GLA_DV), F32)
            xp, sp = _gla_layer(xp, s0, *gw, alpha=alpha)
            xs, ss = _gla_layer(xs, state_gla[j], *gw, alpha=alpha)
            gla_p.append(sp)
            gla_s.append(ss)
        else:
            w_out = w_out_dsa[j].astype(BF16)
            hf, hb, k_p, v_p, ik_p = _dsa_project(xp, w_in_dsa[j])
            op = _dsa_prompt(hf, hb)
            bp, tp, _ = xp.shape
            xp = _mm_res_ln(xp.reshape(bp * tp, d), op.reshape(bp * tp, -1), w_out, lg.reshape(1, d),
                            lb.reshape(1, d), alpha=alpha).reshape(bp, tp, d)
            hf, hb, k_s, v_s, ik_s = _dsa_project(xs, w_in_dsa[j])
            sc = _dsa_sample_scores(hf, cache_idx_k[j], page_table, pp=pp)
            sel = _dsa_sample_select(sc, hf)
            n_pool = cache_k.shape[1]
            os_ = _dsa_sample_attend(hf, sel, cache_k[j].reshape(n_pool, PAGE_SIZE, DSA_HKV),
                                     cache_v[j].reshape(n_pool, PAGE_SIZE, DSA_HKV), page_table, pp=pp)
            bs, ts, _ = xs.shape
            xs = _mm_res_ln(xs.reshape(bs * ts, d), os_.reshape(bs * ts, -1), w_out, lg.reshape(1, d),
                            lb.reshape(1, d), alpha=alpha).reshape(bs, ts, d)
            kp.append(k_p)
            vp.append(v_p)
            ikp.append(ik_p)
            ksm.append(k_s)
            vsm.append(v_s)
            iksm.append(ik_s)
        fg, fb = ln_ffn_g[i].reshape(1, d), ln_ffn_b[i].reshape(1, d)
        outs = []
        for x, p in ((xp, p_prompt[i]), (xs, p_sample[i])):
            bsz, t, _ = x.shape
            x2 = x.reshape(bsz * t, d)
            if i % 2 == 0:
                y2 = _ffn_ln(x2, w_ffn_gate[j].astype(BF16), w_ffn_up[j].astype(BF16), w_ffn_down[j].astype(BF16),
                             fg, fb, alpha=alpha)
            else:
                gates = _router(x2, _pad_cols(w_router[j], LANE))
                y2 = _moe_ln(x2, gates, w_exp_gate[j].astype(BF16), w_exp_up[j].astype(BF16),
                             w_exp_down[j].astype(BF16), fg, fb, alpha=alpha)
            y2 = _ple(y2, p.reshape(bsz * t, -1), w_ple_gate[i].astype(BF16), w_ple_proj[i].astype(BF16))
            outs.append(y2.reshape(bsz, t, d))
        xp, xs = outs
    return (xp, xs, jnp.stack(gla_p), jnp.stack(gla_s), jnp.stack(kp), jnp.stack(vp), jnp.stack(ikp),
            jnp.stack(ksm), jnp.stack(vsm), jnp.stack(iksm))
```

```python
import functools

import jax
import jax.numpy as jnp
from jax import lax
from jax.experimental import pallas as pl
from jax.experimental.pallas import tpu as pltpu

F32 = jnp.float32
BF16 = jnp.bfloat16
I32 = jnp.int32

D_MODEL = 1024
LN_EPS = 1e-5
LANE = 128

GLA_HEADS = 4
GLA_DK = 128
GLA_DV = 256
GLA_GATE_RANK = 16
GLA_TAU = 16.0
GLA_CHUNK = 64
GLA_HK = GLA_HEADS * GLA_DK
GLA_HV = GLA_HEADS * GLA_DV
GLA_COLS = 2 * GLA_HK + 2 * GLA_HV + LANE

DSA_HEADS = 8
DSA_HEAD_DIM = 128
DSA_KV_HEADS = 2
DSA_GROUP = DSA_HEADS // DSA_KV_HEADS
DSA_HKV = DSA_KV_HEADS * DSA_HEAD_DIM
IDX_HEADS = 8
IDX_DIM = 64
TOPK_MAX = 256
PAGE_SIZE = 128
DSA_Q0 = 0
DSA_IQ0 = DSA_HEADS * DSA_HEAD_DIM
DSA_K0 = DSA_IQ0 + IDX_HEADS * LANE
DSA_V0 = DSA_K0 + DSA_HKV
DSA_V1_0 = DSA_V0 + DSA_HKV
DSA_IKW0 = DSA_V1_0 + 2 * DSA_HKV
DSA_COLS = DSA_IKW0 + LANE
DSA_ONES = tuple(DSA_V1_0 + (2 * c + 1) * DSA_HEAD_DIM for c in range(DSA_KV_HEADS))

N_EXPERTS = 8

NEG_INF_KEY = -2139095041
INT_MIN = -2147483648
MASKED = -1e30


def _cparams(sem, vmem_mb):
    return pltpu.CompilerParams(dimension_semantics=sem, vmem_limit_bytes=vmem_mb * 2 ** 20)


def _dot(a, b):
    return jnp.dot(a, b, preferred_element_type=F32)


def _dot_nt(a, b):
    return lax.dot_general(a, b, (((1,), (1,)), ((), ())), preferred_element_type=F32)


def _dot_tn(a, b):
    return lax.dot_general(a, b, (((0,), (0,)), ((), ())), preferred_element_type=F32)


def _sigmoid(x):
    return 1.0 / (1.0 + jnp.exp(-x))


def _layer_norm(y, g, b):
    mu = jnp.mean(y, axis=-1, keepdims=True)
    yc = y - mu
    var = jnp.mean(yc * yc, axis=-1, keepdims=True)
    return yc * lax.rsqrt(var + LN_EPS) * g + b


def _row_tile(n, pref):
    t = min(n, pref)
    assert n % t == 0
    return t


def _proj_kernel(x_ref, w_ref, o_ref, ob_ref, *, tn, ones_at):
    xb = x_ref[...].astype(BF16)
    for c in range(w_ref.shape[1] // tn):
        r = _dot(xb, w_ref[:, c * tn:(c + 1) * tn])
        o_ref[:, c * tn:(c + 1) * tn] = r
        ob_ref[:, c * tn:(c + 1) * tn] = r.astype(BF16)
    for c0 in ones_at:
        ob_ref[:, c0:c0 + LANE] = jnp.ones((x_ref.shape[0], LANE), BF16)


def _proj(x, w, *, tn, ones_at=()):
    n, d = x.shape
    cols = w.shape[1]
    assert cols % tn == 0
    tm = _row_tile(n, 512)
    return pl.pallas_call(
        functools.partial(_proj_kernel, tn=tn, ones_at=ones_at),
        grid=(n // tm,),
        in_specs=[pl.BlockSpec((tm, d), lambda i: (i, 0)), pl.BlockSpec((d, cols), lambda i: (0, 0))],
        out_specs=[pl.BlockSpec((tm, cols), lambda i: (i, 0)), pl.BlockSpec((tm, cols), lambda i: (i, 0))],
        out_shape=[jax.ShapeDtypeStruct((n, cols), F32), jax.ShapeDtypeStruct((n, cols), BF16)],
        compiler_params=_cparams(("arbitrary",), 56),
        name="proj",
    )(x, w)


def _mm_res_ln_kernel(x_ref, a_ref, w_ref, g_ref, b_ref, o_ref, *, alpha):
    m = _dot(a_ref[...].astype(BF16), w_ref[...])
    o_ref[...] = _layer_norm(alpha * x_ref[...] + m, g_ref[...], b_ref[...])


def _mm_res_ln(x, a, w, g, b, *, alpha):
    n, d = x.shape
    ka = a.shape[1]
    tm = _row_tile(n, 512)
    row = lambda i: (i, 0)
    fixed = lambda i: (0, 0)
    return pl.pallas_call(
        functools.partial(_mm_res_ln_kernel, alpha=alpha),
        grid=(n // tm,),
        in_specs=[pl.BlockSpec((tm, d), row), pl.BlockSpec((tm, ka), row), pl.BlockSpec((ka, d), fixed),
                  pl.BlockSpec((1, d), fixed), pl.BlockSpec((1, d), fixed)],
        out_specs=pl.BlockSpec((tm, d), row),
        out_shape=jax.ShapeDtypeStruct((n, d), F32),
        compiler_params=_cparams(("arbitrary",), 40),
        name="mm_res_ln",
    )(x, a, w, g, b)


def _ffn_ln_kernel(x_ref, wg_ref, wu_ref, wd_ref, g_ref, b_ref, o_ref, xb_ref, acc_ref, *, alpha):
    f = pl.program_id(1)

    @pl.when(f == 0)
    def _():
        xb_ref[...] = x_ref[...].astype(BF16)
        acc_ref[...] = jnp.zeros_like(acc_ref)

    xb = xb_ref[...]
    hg = _dot(xb, wg_ref[...])
    hu = _dot(xb, wu_ref[...])
    h = (hg * _sigmoid(hg) * hu).astype(BF16)
    acc_ref[...] += _dot(h, wd_ref[...])

    @pl.when(f == pl.num_programs(1) - 1)
    def _():
        o_ref[...] = _layer_norm(alpha * x_ref[...] + acc_ref[...], g_ref[...], b_ref[...])


def _ffn_ln(x, wg, wu, wd, g, b, *, alpha, tf=512):
    n, d = x.shape
    ff = wg.shape[1]
    assert ff % tf == 0
    tm = _row_tile(n, 1024)
    return pl.pallas_call(
        functools.partial(_ffn_ln_kernel, alpha=alpha),
        grid=(n // tm, ff // tf),
        in_specs=[pl.BlockSpec((tm, d), lambda i, f: (i, 0)),
                  pl.BlockSpec((d, tf), lambda i, f: (0, f)),
                  pl.BlockSpec((d, tf), lambda i, f: (0, f)),
                  pl.BlockSpec((tf, d), lambda i, f: (f, 0)),
                  pl.BlockSpec((1, d), lambda i, f: (0, 0)),
                  pl.BlockSpec((1, d), lambda i, f: (0, 0))],
        out_specs=pl.BlockSpec((tm, d), lambda i, f: (i, 0)),
        out_shape=jax.ShapeDtypeStruct((n, d), F32),
        scratch_shapes=[pltpu.VMEM((tm, d), BF16), pltpu.VMEM((tm, d), F32)],
        compiler_params=_cparams(("arbitrary", "arbitrary"), 48),
        name="ffn_ln",
    )(x, wg, wu, wd, g, b)


def _ple_kernel(x_ref, p_ref, wg_ref, wp_ref, o_ref):
    x = x_ref[...]
    gate = _sigmoid(_dot(x.astype(BF16), wg_ref[...]))
    o_ref[...] = x + gate * _dot(p_ref[...].astype(BF16), wp_ref[...])


def _ple(x, p, wg, wp):
    n, d = x.shape
    pd = p.shape[1]
    tm = _row_tile(n, 512)
    row = lambda i: (i, 0)
    fixed = lambda i: (0, 0)
    return pl.pallas_call(
        _ple_kernel,
        grid=(n // tm,),
        in_specs=[pl.BlockSpec((tm, d), row), pl.BlockSpec((tm, pd), row),
                  pl.BlockSpec((d, d), fixed), pl.BlockSpec((pd, d), fixed)],
        out_specs=pl.BlockSpec((tm, d), row),
        out_shape=jax.ShapeDtypeStruct((n, d), F32),
        compiler_params=_cparams(("arbitrary",), 40),
        name="ple",
    )(x, p, wg, wp)


def _gla_kernel(q_ref, k_ref, v_ref, g_ref, a_ref, wup_ref, ba_ref, ng_ref, s0_ref, o_ref, st_out_ref, st_ref,
                *, c, valid):
    t = pl.program_id(1)

    @pl.when(t == 0)
    def _():
        st_ref[...] = s0_ref[0]

    row = lax.broadcasted_iota(I32, (c, c), 0)
    col = lax.broadcasted_iota(I32, (c, c), 1)
    causal = row >= col
    logit = _dot(a_ref[0].astype(BF16), wup_ref[...]) + ba_ref[...]
    log_a = (jnp.minimum(logit, 0.0) - jnp.log(1.0 + jnp.exp(-jnp.abs(logit)))) * (1.0 / GLA_TAU)
    if valid < c:
        log_a = jnp.where(lax.broadcasted_iota(I32, log_a.shape, 0) < valid, log_a, 0.0)
    bcum = jnp.dot(causal.astype(F32), log_a, preferred_element_type=F32, precision=lax.Precision.HIGHEST)
    blast = bcum[c - 1:c, :]
    q = q_ref[0] * (GLA_DK ** -0.5)
    k = k_ref[0]
    q_dec = (q * jnp.exp(bcum)).astype(BF16)
    k_inv = (k * jnp.exp(-bcum)).astype(BF16)
    k_dec = (k * jnp.exp(blast - bcum)).astype(BF16)
    decay = jnp.exp(blast)
    v = v_ref[0].astype(BF16)
    for h in range(GLA_HEADS):
        sk = slice(h * GLA_DK, (h + 1) * GLA_DK)
        sv = slice(h * GLA_DV, (h + 1) * GLA_DV)
        att = jnp.where(causal, _dot_nt(q_dec[:, sk], k_inv[:, sk]), 0.0).astype(BF16)
        st = st_ref[h]
        o = _dot(att, v[:, sv]) + _dot_nt(q_dec[:, sk], st.astype(BF16))
        st_ref[h] = st * decay[:, sk] + _dot_tn(v[:, sv], k_dec[:, sk])
        mu = jnp.mean(o, axis=-1, keepdims=True)
        oc = o - mu
        var = jnp.mean(oc * oc, axis=-1, keepdims=True)
        on = oc * lax.rsqrt(var + LN_EPS) * ng_ref[:, sv]
        gg = g_ref[0, :, sv]
        o_ref[0, :, sv] = (on * (gg * _sigmoid(gg))).astype(BF16)

    @pl.when(t == pl.num_programs(1) - 1)
    def _():
        st_out_ref[0] = st_ref[...]


def _gla_core(h, s0_t, wup, ba, ng, *, c, valid):
    bsz, tlen, _ = h.shape
    assert tlen % c == 0
    v_blk = (2 * GLA_HK) // GLA_HV
    return pl.pallas_call(
        functools.partial(_gla_kernel, c=c, valid=valid),
        grid=(bsz, tlen // c),
        in_specs=[pl.BlockSpec((1, c, GLA_HK), lambda b, t: (b, t, 0)),
                  pl.BlockSpec((1, c, GLA_HK), lambda b, t: (b, t, 1)),
                  pl.BlockSpec((1, c, GLA_HV), lambda b, t: (b, t, v_blk)),
                  pl.BlockSpec((1, c, GLA_HV), lambda b, t: (b, t, v_blk + 1)),
                  pl.BlockSpec((1, c, LANE), lambda b, t: (b, t, (2 * GLA_HK + 2 * GLA_HV) // LANE)),
                  pl.BlockSpec((LANE, GLA_HK), lambda b, t: (0, 0)),
                  pl.BlockSpec((1, GLA_HK), lambda b, t: (0, 0)),
                  pl.BlockSpec((1, GLA_HV), lambda b, t: (0, 0)),
                  pl.BlockSpec((1, GLA_HEADS, GLA_DV, GLA_DK), lambda b, t: (b, 0, 0, 0))],
        out_specs=[pl.BlockSpec((1, c, GLA_HV), lambda b, t: (b, t, 0)),
                   pl.BlockSpec((1, GLA_HEADS, GLA_DV, GLA_DK), lambda b, t: (b, 0, 0, 0))],
        out_shape=[jax.ShapeDtypeStruct((bsz, tlen, GLA_HV), BF16),
                   jax.ShapeDtypeStruct((bsz, GLA_HEADS, GLA_DV, GLA_DK), F32)],
        scratch_shapes=[pltpu.VMEM((GLA_HEADS, GLA_DV, GLA_DK), F32)],
        compiler_params=_cparams(("arbitrary", "arbitrary"), 32),
        name="gla_core",
    )(h, h, h, h, h, wup, ba, ng, s0_t)


def _sortable(s):
    bits = lax.bitcast_convert_type(s, I32)
    return bits ^ (lax.shift_right_arithmetic(bits, 31) & 0x7FFFFFFF)


def _lane_fold(m):
    acc = m[:, 0:LANE]
    for i in range(1, m.shape[1] // LANE):
        acc = acc + m[:, i * LANE:(i + 1) * LANE]
    return acc


def _lane_fold_max(m):
    acc = m[:, 0:LANE]
    for i in range(1, m.shape[1] // LANE):
        acc = jnp.maximum(acc, m[:, i * LANE:(i + 1) * LANE])
    return acc


def _bias_bits(sel):
    return lax.bitcast_convert_type(jnp.where(sel, 0.0, MASKED).astype(F32), I32)


def _kth_largest_key(count_ge, rows, k):
    kf = float(k)
    thr0 = jnp.where(count_ge(jnp.zeros((rows, 1), I32)) >= kf, 0, INT_MIN).astype(I32)

    def bit_body(i, thr):
        cand = thr | lax.shift_left(jnp.int32(1), 30 - i)
        return jnp.where(count_ge(cand) >= kf, cand, thr)

    return lax.fori_loop(0, 31, bit_body, thr0)


def _strict_upper(n):
    return (lax.broadcasted_iota(I32, (n, n), 0) < lax.broadcasted_iota(I32, (n, n), 1)).astype(BF16)


def _dsa_prompt_kernel(q_ref, iq_ref, iw_ref, ikw_ref, k_ref, v1_ref, o_ref,
                       keys_ref, iqs_ref, qg_ref, d_ref, m_ref, acc_ref, *, tq, tk, topk):
    j = pl.program_id(1)
    nkt = ((j + 1) * tq + tk - 1) // tk
    gq = DSA_GROUP * tq
    rp = 16
    for h in range(IDX_HEADS):
        iqs_ref[h * tq:(h + 1) * tq, :] = iq_ref[0, :, h * LANE:(h + 1) * LANE]
    for c in range(DSA_KV_HEADS):
        for g in range(DSA_GROUP):
            hh = c * DSA_GROUP + g
            qg_ref[c, g * tq:(g + 1) * tq, :] = q_ref[0, :, hh * DSA_HEAD_DIM:(hh + 1) * DSA_HEAD_DIM]
    ws = iw_ref[0][:, IDX_DIM:IDX_DIM + IDX_HEADS] * (IDX_HEADS ** -0.5) * (IDX_DIM ** -0.5)

    def score_tile(kt, diagonal):
        ikt = ikw_ref[0, pl.ds(pl.multiple_of(kt * tk, tk), tk), :]
        d_ref[...] = _dot_nt(iqs_ref[...], ikt)
        for r in range(tq // rp):
            s = jnp.zeros((rp, tk), F32)
            for h in range(IDX_HEADS):
                dh = d_ref[h * tq + r * rp:h * tq + (r + 1) * rp, :]
                s = s + ws[r * rp:(r + 1) * rp, h:h + 1] * jnp.maximum(dh, 0.0)
            s = s + 0.0
            if diagonal:
                qpos = j * tq + r * rp + lax.broadcasted_iota(I32, (rp, tk), 0)
                kpos = kt * tk + lax.broadcasted_iota(I32, (rp, tk), 1)
                s = jnp.where(kpos <= qpos, s, -jnp.inf)
            keys_ref[kt, r * rp:(r + 1) * rp, :] = _sortable(s)

    def score_body(kt, carry):
        score_tile(kt, False)
        return carry

    lax.fori_loop(0, nkt - 1, score_body, 0)
    score_tile(nkt - 1, True)

    def counts(pred):
        def body(kt, acc):
            return acc + _lane_fold(jnp.where(pred(keys_ref[kt]), 1.0, 0.0))
        return jnp.sum(lax.fori_loop(0, nkt, body, jnp.zeros((tq, LANE), F32)), axis=1, keepdims=True)

    thr = _kth_largest_key(lambda cand: counts(lambda kk: kk >= cand), tq, topk)
    n_gt = counts(lambda kk: kk > thr)
    n_eq = counts(lambda kk: kk == thr)
    need = float(topk) - n_gt
    tie_row = jnp.logical_and(n_eq > need, thr > NEG_INF_KEY)
    any_tie = jnp.max(jnp.where(tie_row, 1.0, 0.0)) > 0.0

    @pl.when(jnp.logical_not(any_tie))
    def _():
        thr_eff = jnp.maximum(thr, NEG_INF_KEY + 1)

        def body(kt, carry):
            keys_ref[kt] = _bias_bits(keys_ref[kt] >= thr_eff)
            return carry
        lax.fori_loop(0, nkt, body, 0)

    @pl.when(any_tie)
    def _():
        upper = _strict_upper(tk)

        def body(kt, seen):
            kk = keys_ref[kt]
            eq = kk == thr
            before = seen + _dot(jnp.where(eq, 1.0, 0.0).astype(BF16), upper)
            sel = jnp.logical_and(kk > NEG_INF_KEY, jnp.logical_or(kk > thr, jnp.logical_and(eq, before < need)))
            keys_ref[kt] = _bias_bits(sel)
            return seen + jnp.sum(jnp.where(eq, 1.0, 0.0), axis=1, keepdims=True)
        lax.fori_loop(0, nkt, body, jnp.zeros((tq, 1), F32))

    m_ref[...] = jnp.full(m_ref.shape, MASKED, F32)
    acc_ref[...] = jnp.zeros_like(acc_ref)
    c2 = (DSA_HEAD_DIM ** -0.5) * 1.4426950408889634
    dh = DSA_HEAD_DIM

    def attend_body(kt, carry):
        rows = pl.ds(pl.multiple_of(kt * tk, tk), tk)
        bias = lax.bitcast_convert_type(keys_ref[kt], F32)[None]
        for c in range(DSA_KV_HEADS):
            kc = k_ref[0, rows, c * dh:(c + 1) * dh]
            v1 = v1_ref[0, rows, c * 2 * dh:(c + 1) * 2 * dh]
            x = (_dot_nt(qg_ref[c], kc).reshape(DSA_GROUP, tq, tk) + bias).reshape(gq, tk)
            m_old = m_ref[c]
            m_new = jnp.maximum(m_old, jnp.max(x, axis=1, keepdims=True))
            p = jnp.exp2((x - jnp.concatenate([m_new] * (tk // LANE), axis=1)) * c2).astype(BF16)
            alpha = jnp.exp2((m_old - m_new) * c2)
            acc_ref[c] = jnp.concatenate([alpha, alpha], axis=1) * acc_ref[c] + _dot(p, v1)
            m_ref[c] = m_new
        return carry

    lax.fori_loop(0, nkt, attend_body, 0)
    for c in range(DSA_KV_HEADS):
        oc = acc_ref[c, :, 0:dh] / acc_ref[c, :, dh:2 * dh]
        for g in range(DSA_GROUP):
            hh = c * DSA_GROUP + g
            o_ref[0, :, hh * dh:(hh + 1) * dh] = oc[g * tq:(g + 1) * tq].astype(BF16)


def _dsa_prompt(hf, hb, *, tq=128, tk=1024):
    bsz, s, _ = hf.shape
    tk = min(tk, s)
    assert s % tq == 0 and s % tk == 0 and tk % tq == 0
    topk = min(TOPK_MAX, s // 4)
    hq = DSA_HEADS * DSA_HEAD_DIM
    gq = DSA_GROUP * tq
    return pl.pallas_call(
        functools.partial(_dsa_prompt_kernel, tq=tq, tk=tk, topk=topk),
        grid=(bsz, s // tq),
        in_specs=[pl.BlockSpec((1, tq, hq), lambda b, j: (b, j, DSA_Q0 // hq)),
                  pl.BlockSpec((1, tq, IDX_HEADS * LANE), lambda b, j: (b, j, DSA_IQ0 // (IDX_HEADS * LANE))),
                  pl.BlockSpec((1, tq, LANE), lambda b, j: (b, j, DSA_IKW0 // LANE)),
                  pl.BlockSpec((1, s, LANE), lambda b, j: (b, 0, DSA_IKW0 // LANE)),
                  pl.BlockSpec((1, s, DSA_HKV), lambda b, j: (b, 0, DSA_K0 // DSA_HKV)),
                  pl.BlockSpec((1, s, 2 * DSA_HKV), lambda b, j: (b, 0, DSA_V1_0 // (2 * DSA_HKV)))],
        out_specs=pl.BlockSpec((1, tq, hq), lambda b, j: (b, j, 0)),
        out_shape=jax.ShapeDtypeStruct((bsz, s, hq), BF16),
        scratch_shapes=[pltpu.VMEM((s // tk, tq, tk), I32),
                        pltpu.VMEM((IDX_HEADS * tq, LANE), BF16),
                        pltpu.VMEM((DSA_KV_HEADS, gq, DSA_HEAD_DIM), BF16),
                        pltpu.VMEM((IDX_HEADS * tq, tk), F32),
                        pltpu.VMEM((DSA_KV_HEADS, gq, LANE), F32),
                        pltpu.VMEM((DSA_KV_HEADS, gq, 2 * DSA_HEAD_DIM), F32)],
        compiler_params=_cparams(("arbitrary", "arbitrary"), 56),
        name="dsa_prompt",
    )(hb, hb, hf, hb, hb, hb)


def _sample_ws(iw_ref):
    return iw_ref[0][:, IDX_DIM:IDX_DIM + IDX_HEADS] * (IDX_HEADS ** -0.5)


def _sample_scores(d, ws, t):
    s = jnp.zeros((t, d.shape[1]), F32)
    for h in range(IDX_HEADS):
        s = s + ws[:, h:h + 1] * jnp.maximum(d[h * t:(h + 1) * t] * (IDX_DIM ** -0.5), 0.0)
    return s + 0.0


def _dsa_sample_scores_kernel(pt_ref, iq_ref, iw_ref, *refs, pp, t):
    page_refs, o_ref = refs[:pp], refs[pp]
    iq_stack = jnp.concatenate([iq_ref[0, :, h * LANE:h * LANE + IDX_DIM] for h in range(IDX_HEADS)], axis=0)
    ik = jnp.concatenate([r[0].astype(BF16) for r in page_refs], axis=0)
    o_ref[0] = _sample_scores(_dot_nt(iq_stack.astype(BF16), ik), _sample_ws(iw_ref), t)


def _dsa_sample_scores(hf, cache_ik, page_table, *, pp):
    bsz, t, _ = hf.shape
    n_pages = page_table.shape[1]
    assert n_pages % pp == 0
    iq_w = IDX_HEADS * LANE

    def page_spec(i):
        return pl.BlockSpec((1, PAGE_SIZE, IDX_DIM), lambda b, j, pt: (pt[b, j * pp + i], 0, 0))

    grid_spec = pltpu.PrefetchScalarGridSpec(
        num_scalar_prefetch=1, grid=(bsz, n_pages // pp),
        in_specs=[pl.BlockSpec((1, t, iq_w), lambda b, j, pt: (b, 0, DSA_IQ0 // iq_w)),
                  pl.BlockSpec((1, t, LANE), lambda b, j, pt: (b, 0, DSA_IKW0 // LANE))]
                 + [page_spec(i) for i in range(pp)],
        out_specs=pl.BlockSpec((1, t, pp * PAGE_SIZE), lambda b, j, pt: (b, 0, j)))
    return pl.pallas_call(
        functools.partial(_dsa_sample_scores_kernel, pp=pp, t=t),
        grid_spec=grid_spec,
        out_shape=jax.ShapeDtypeStruct((bsz, t, n_pages * PAGE_SIZE), F32),
        compiler_params=_cparams(("arbitrary", "arbitrary"), 32),
        name="dsa_sample_scores",
    )(page_table, hf, hf, *([cache_ik] * pp))


def _dsa_sample_select_kernel(sp_ref, iq_ref, iw_ref, ikw_ref, o_ref, keys_ref, *, t, past, new_w, topk, ck):
    iq_stack = jnp.concatenate([iq_ref[0, :, h * LANE:(h + 1) * LANE] for h in range(IDX_HEADS)], axis=0)
    ik_new = jnp.concatenate([ikw_ref[0], jnp.zeros((new_w - t, LANE), F32)], axis=0)
    d_new = _dot_nt(iq_stack.astype(BF16), ik_new.astype(BF16))
    s_new = _sample_scores(d_new, _sample_ws(iw_ref), t)
    row = lax.broadcasted_iota(I32, (t, new_w), 0)
    lane = lax.broadcasted_iota(I32, (t, new_w), 1)
    s_new = jnp.where(lane <= row, s_new, -jnp.inf)
    keys_ref[:, 0:past] = _sortable(sp_ref[0])
    keys_ref[:, past:past + new_w] = _sortable(s_new)
    n_ck = (past + new_w) // ck

    def counts(pred):
        acc = jnp.zeros((t, LANE), F32)
        for i in range(n_ck):
            acc = acc + _lane_fold(jnp.where(pred(keys_ref[:, i * ck:(i + 1) * ck]), 1.0, 0.0))
        return jnp.sum(acc, axis=1, keepdims=True)

    thr = _kth_largest_key(lambda cand: counts(lambda kk: kk >= cand), t, topk)
    n_gt = counts(lambda kk: kk > thr)
    n_eq = counts(lambda kk: kk == thr)
    need = float(topk) - n_gt
    tie_row = jnp.logical_and(n_eq > need, thr > NEG_INF_KEY)
    any_tie = jnp.max(jnp.where(tie_row, 1.0, 0.0)) > 0.0

    @pl.when(jnp.logical_not(any_tie))
    def _():
        thr_eff = jnp.maximum(thr, NEG_INF_KEY + 1)
        o_ref[0] = jnp.where(keys_ref[...] >= thr_eff, 1.0, 0.0)

    @pl.when(any_tie)
    def _():
        upper = _strict_upper(ck)
        seen = jnp.zeros((t, 1), F32)
        for i in range(n_ck):
            kk = keys_ref[:, i * ck:(i + 1) * ck]
            eq = kk == thr
            before = seen + _dot(jnp.where(eq, 1.0, 0.0).astype(BF16), upper)
            sel = jnp.logical_and(kk > NEG_INF_KEY, jnp.logical_or(kk > thr, jnp.logical_and(eq, before < need)))
            o_ref[0, :, i * ck:(i + 1) * ck] = jnp.where(sel, 1.0, 0.0)
            seen = seen + jnp.sum(jnp.where(eq, 1.0, 0.0), axis=1, keepdims=True)


def _dsa_sample_select(scores_past, hf, *, new_w=512, ck=512):
    bsz, t, past = scores_past.shape
    assert past % ck == 0 and new_w % ck == 0
    topk = min(TOPK_MAX, (past + t) // 4)
    iq_w = IDX_HEADS * LANE
    total = past + new_w
    return pl.pallas_call(
        functools.partial(_dsa_sample_select_kernel, t=t, past=past, new_w=new_w, topk=topk, ck=ck),
        grid=(bsz,),
        in_specs=[pl.BlockSpec((1, t, past), lambda b: (b, 0, 0)),
                  pl.BlockSpec((1, t, iq_w), lambda b: (b, 0, DSA_IQ0 // iq_w)),
                  pl.BlockSpec((1, t, LANE), lambda b: (b, 0, DSA_IKW0 // LANE)),
                  pl.BlockSpec((1, t, LANE), lambda b: (b, 0, DSA_IKW0 // LANE))],
        out_specs=pl.BlockSpec((1, t, total), lambda b: (b, 0, 0)),
        out_shape=jax.ShapeDtypeStruct((bsz, t, total), F32),
        scratch_shapes=[pltpu.VMEM((t, total), I32)],
        compiler_params=_cparams(("arbitrary",), 32),
        name="dsa_sample_select",
    )(scores_past, hf, hf, hf)


def _dsa_sample_attend_kernel(pt_ref, q_ref, kn_ref, vn_ref, selp_ref, seln_ref, *refs, pp, t, new_w):
    k_pages, v_pages = refs[:pp], refs[pp:2 * pp]
    o_ref, m_ref, l_ref, acc_ref = refs[2 * pp:]
    j = pl.program_id(1)
    gt = DSA_GROUP * t

    @pl.when(j == 0)
    def _():
        m_ref[...] = jnp.full(m_ref.shape, MASKED, F32)
        l_ref[...] = jnp.zeros_like(l_ref)
        acc_ref[...] = jnp.zeros_like(acc_ref)

    q_groups = [jnp.concatenate([q_ref[0, :, (c * DSA_GROUP + g) * DSA_HEAD_DIM:(c * DSA_GROUP + g + 1) * DSA_HEAD_DIM]
                                 for g in range(DSA_GROUP)], axis=0).astype(BF16) for c in range(DSA_KV_HEADS)]

    def attend(kcs, vcs, sel):
        n = kcs[0].shape[0]
        for c in range(DSA_KV_HEADS):
            kc, vc = kcs[c], vcs[c]
            logits = (_dot_nt(q_groups[c], kc) * (DSA_HEAD_DIM ** -0.5)).reshape(DSA_GROUP, t, n)
            masked = jnp.where(sel[None], logits, MASKED).reshape(gt, n)
            m_old = m_ref[c]
            m_new = jnp.maximum(m_old, jnp.max(masked, axis=1, keepdims=True))
            p = jnp.where(sel[None], jnp.exp(masked - m_new).reshape(DSA_GROUP, t, n), 0.0).reshape(gt, n)
            alpha = jnp.exp(m_old - m_new)
            l_ref[c] = alpha * l_ref[c] + jnp.sum(p, axis=1, keepdims=True)
            acc_ref[c] = alpha * acc_ref[c] + _dot(p.astype(BF16), vc)
            m_ref[c] = m_new

    def head_rows(pages, c):
        return jnp.concatenate([r[0, pl.ds(c, PAGE_SIZE, stride=DSA_KV_HEADS), :].astype(BF16) for r in pages], axis=0)

    attend([head_rows(k_pages, c) for c in range(DSA_KV_HEADS)],
           [head_rows(v_pages, c) for c in range(DSA_KV_HEADS)], selp_ref[0] != 0.0)

    @pl.when(j == pl.num_programs(1) - 1)
    def _():
        pad = jnp.zeros((new_w - t, DSA_HEAD_DIM), F32)

        def new_rows(ref, c):
            return jnp.concatenate([ref[0, :, c * DSA_HEAD_DIM:(c + 1) * DSA_HEAD_DIM], pad], axis=0).astype(BF16)

        attend([new_rows(kn_ref, c) for c in range(DSA_KV_HEADS)],
               [new_rows(vn_ref, c) for c in range(DSA_KV_HEADS)], seln_ref[0] != 0.0)
        for c in range(DSA_KV_HEADS):
            oc = acc_ref[c] / l_ref[c]
            for g in range(DSA_GROUP):
                hh = c * DSA_GROUP + g
                o_ref[0, :, hh * DSA_HEAD_DIM:(hh + 1) * DSA_HEAD_DIM] = oc[g * t:(g + 1) * t].astype(BF16)


def _dsa_sample_attend(hf, sel, cache_k, cache_v, page_table, *, pp, new_w=512):
    bsz, t, _ = hf.shape
    n_pages = page_table.shape[1]
    past = n_pages * PAGE_SIZE
    assert n_pages % pp == 0 and past % new_w == 0
    hq = DSA_HEADS * DSA_HEAD_DIM
    gt = DSA_GROUP * t

    def page_spec(i):
        return pl.BlockSpec((1, PAGE_SIZE * DSA_KV_HEADS, DSA_HEAD_DIM), lambda b, j, pt: (pt[b, j * pp + i], 0, 0))

    grid_spec = pltpu.PrefetchScalarGridSpec(
        num_scalar_prefetch=1, grid=(bsz, n_pages // pp),
        in_specs=[pl.BlockSpec((1, t, hq), lambda b, j, pt: (b, 0, DSA_Q0 // hq)),
                  pl.BlockSpec((1, t, DSA_HKV), lambda b, j, pt: (b, 0, DSA_K0 // DSA_HKV)),
                  pl.BlockSpec((1, t, DSA_HKV), lambda b, j, pt: (b, 0, DSA_V0 // DSA_HKV)),
                  pl.BlockSpec((1, t, pp * PAGE_SIZE), lambda b, j, pt: (b, 0, j)),
                  pl.BlockSpec((1, t, new_w), lambda b, j, pt: (b, 0, past // new_w))]
                 + [page_spec(i) for i in range(pp)] * 2,
        out_specs=pl.BlockSpec((1, t, hq), lambda b, j, pt: (b, 0, 0)),
        scratch_shapes=[pltpu.VMEM((DSA_KV_HEADS, gt, 1), F32),
                        pltpu.VMEM((DSA_KV_HEADS, gt, 1), F32),
                        pltpu.VMEM((DSA_KV_HEADS, gt, DSA_HEAD_DIM), F32)])
    return pl.pallas_call(
        functools.partial(_dsa_sample_attend_kernel, pp=pp, t=t, new_w=new_w),
        grid_spec=grid_spec,
        out_shape=jax.ShapeDtypeStruct((bsz, t, hq), BF16),
        compiler_params=_cparams(("arbitrary", "arbitrary"), 48),
        name="dsa_sample_attend",
    )(page_table, hf, hf, hf, sel, sel, *([cache_k] * pp), *([cache_v] * pp))


def _router_kernel(x_ref, wt_ref, o_ref):
    x = x_ref[...]
    wt = wt_ref[...]
    xh = x.astype(BF16)
    xl = (x - xh.astype(F32)).astype(BF16)
    wh = wt.astype(BF16)
    wl = (wt - wh.astype(F32)).astype(BF16)
    logits = _dot_nt(wh, xh) + (_dot_nt(wl, xh) + _dot_nt(wh, xl))
    sub = lax.broadcasted_iota(I32, logits.shape, 0)
    l1 = jnp.max(logits, axis=0, keepdims=True)
    i1 = jnp.min(jnp.where(logits == l1, sub, N_EXPERTS), axis=0, keepdims=True)
    rest = jnp.where(sub == i1, -jnp.inf, logits)
    l2 = jnp.max(rest, axis=0, keepdims=True)
    i2 = jnp.min(jnp.where(rest == l2, sub, N_EXPERTS), axis=0, keepdims=True)
    e2 = jnp.exp(l2 - l1)
    denom = 1.0 + e2
    o_ref[...] = jnp.where(sub == i1, 1.0 / denom, jnp.where(sub == i2, e2 / denom, 0.0))


def _router(x, w_t):
    n, d = x.shape
    tm = _row_tile(n, 512)
    return pl.pallas_call(
        _router_kernel,
        grid=(n // tm,),
        in_specs=[pl.BlockSpec((tm, d), lambda i: (i, 0)), pl.BlockSpec((N_EXPERTS, d), lambda i: (0, 0))],
        out_specs=pl.BlockSpec((N_EXPERTS, tm), lambda i: (0, i)),
        out_shape=jax.ShapeDtypeStruct((N_EXPERTS, n), F32),
        compiler_params=_cparams(("arbitrary",), 32),
        name="router",
    )(x, w_t)


MOE_ROWS = 128


def _moe_ln_kernel(cnt_ref, x_ref, gt_ref, wg_ref, wu_ref, wd_ref, g_ref, b_ref, o_ref,
                   xb_ref, rank_ref, xg_ref, acc_ref, y_ref, upper_ref, *, alpha, arms):
    i, e, f = pl.program_id(0), pl.program_id(1), pl.program_id(2)
    tm = x_ref.shape[0]
    first_f = f == 0
    last_f = f == pl.num_programs(2) - 1
    nb = (cnt_ref[i * N_EXPERTS + e] + MOE_ROWS - 1) // MOE_ROWS

    @pl.when(jnp.logical_and(i == 0, jnp.logical_and(e == 0, first_f)))
    def _():
        upper_ref[...] = _strict_upper(tm)

    @pl.when(jnp.logical_and(e == 0, first_f))
    def _():
        xb_ref[...] = x_ref[...].astype(BF16)
        y_ref[...] = jnp.zeros_like(y_ref)
        chosen = jnp.where(gt_ref[...] > 0.0, 1.0, 0.0).astype(BF16)
        rank_ref[...] = _dot(chosen, upper_ref[...])

    ge = gt_ref[pl.ds(e, 1), :]
    slot = jnp.where(ge > 0.0, rank_ref[pl.ds(e, 1), :], -1.0)

    def one_hot(start, rows):
        r = (start + lax.broadcasted_iota(I32, (rows, 1), 0)).astype(F32)
        return slot == r

    def for_arms(fn):
        for size in arms:
            start = pl.multiple_of((nb & ~(2 * size - 1)) * MOE_ROWS, MOE_ROWS)

            @pl.when((nb & size) != 0)
            def _():
                fn(start, size * MOE_ROWS)

    @pl.when(first_f)
    def _():
        def gather(start, rows):
            hot = jnp.where(one_hot(start, rows), 1.0, 0.0).astype(BF16)
            xg_ref[pl.ds(start, rows), :] = _dot(hot, xb_ref[...]).astype(BF16)
            acc_ref[pl.ds(start, rows), :] = jnp.zeros((rows, acc_ref.shape[1]), F32)
        for_arms(gather)

    def expert(start, rows):
        xs = xg_ref[pl.ds(start, rows), :]
        hg = _dot(xs, wg_ref[0])
        hu = _dot(xs, wu_ref[0])
        h = (hg * _sigmoid(hg) * hu).astype(BF16)
        acc_ref[pl.ds(start, rows), :] += _dot(h, wd_ref[0])
    for_arms(expert)

    @pl.when(last_f)
    def _():
        def scatter(start, rows):
            hot = one_hot(start, rows)
            gate = jnp.sum(jnp.where(hot, ge, 0.0), axis=1, keepdims=True)
            v = (gate * acc_ref[pl.ds(start, rows), :]).astype(BF16)
            y_ref[...] += _dot_tn(jnp.where(hot, 1.0, 0.0).astype(BF16), v)
        for_arms(scatter)

    @pl.when(jnp.logical_and(e == N_EXPERTS - 1, last_f))
    def _():
        o_ref[...] = _layer_norm(alpha * x_ref[...] + y_ref[...], g_ref[...], b_ref[...])


def _moe_ln(x, gates_t, wg, wu, wd, g, b, *, alpha, tf=512):
    n, d = x.shape
    ff = wg.shape[2]
    assert ff % tf == 0
    tm = _row_tile(n, 1024)
    assert tm % MOE_ROWS == 0
    max_blocks = tm // MOE_ROWS
    arms = tuple(s for s in (64, 32, 16, 8, 4, 2, 1) if s <= max_blocks)
    assert arms[0] == max_blocks
    counts = jnp.sum((gates_t > 0.0).reshape(N_EXPERTS, n // tm, tm), axis=2).astype(I32).T.reshape(-1)
    grid_spec = pltpu.PrefetchScalarGridSpec(
        num_scalar_prefetch=1, grid=(n // tm, N_EXPERTS, ff // tf),
        in_specs=[pl.BlockSpec((tm, d), lambda i, e, f, c: (i, 0)),
                  pl.BlockSpec((N_EXPERTS, tm), lambda i, e, f, c: (0, i)),
                  pl.BlockSpec((1, d, tf), lambda i, e, f, c: (e, 0, f)),
                  pl.BlockSpec((1, d, tf), lambda i, e, f, c: (e, 0, f)),
                  pl.BlockSpec((1, tf, d), lambda i, e, f, c: (e, f, 0)),
                  pl.BlockSpec((1, d), lambda i, e, f, c: (0, 0)),
                  pl.BlockSpec((1, d), lambda i, e, f, c: (0, 0))],
        out_specs=pl.BlockSpec((tm, d), lambda i, e, f, c: (i, 0)),
        scratch_shapes=[pltpu.VMEM((tm, d), BF16), pltpu.VMEM((N_EXPERTS, tm), F32), pltpu.VMEM((tm, d), BF16),
                        pltpu.VMEM((tm, d), F32), pltpu.VMEM((tm, d), F32), pltpu.VMEM((tm, tm), BF16)])
    return pl.pallas_call(
        functools.partial(_moe_ln_kernel, alpha=alpha, arms=arms),
        grid_spec=grid_spec,
        out_shape=jax.ShapeDtypeStruct((n, d), F32),
        compiler_params=_cparams(("arbitrary", "arbitrary", "arbitrary"), 56),
        name="moe_ln",
    )(counts, x, gates_t, wg, wu, wd, g, b)


def _pad_cols(w, cols):
    return jnp.pad(w, ((0, 0), (0, cols - w.shape[1])))


def _gla_layer(x, s0, w_in, w_up, b_a, norm_g, w_out, ln_g, ln_b, *, alpha):
    bsz, t, d = x.shape
    x2 = x.reshape(bsz * t, d)
    hf, _ = _proj(x2, _pad_cols(w_in, GLA_COLS).astype(BF16), tn=640)
    hf = hf.reshape(bsz, t, GLA_COLS)
    c = GLA_CHUNK if t % GLA_CHUNK == 0 else t
    valid = c
    if c < GLA_CHUNK:
        hf = jnp.pad(hf, ((0, 0), (0, GLA_CHUNK - c), (0, 0)))
        c = GLA_CHUNK
    w_up_pad = jnp.pad(w_up, ((0, LANE - GLA_GATE_RANK), (0, 0))).astype(BF16)
    o, st = _gla_core(hf, jnp.swapaxes(s0, 2, 3), w_up_pad, b_a.reshape(1, GLA_HK), norm_g.reshape(1, GLA_HV),
                      c=c, valid=valid)
    o = o[:, :t].reshape(bsz * t, GLA_HV)
    y = _mm_res_ln(x2, o, w_out.astype(BF16), ln_g.reshape(1, d), ln_b.reshape(1, d), alpha=alpha)
    return y.reshape(bsz, t, d), jnp.swapaxes(st, 2, 3)


def _dsa_weight(w_in):
    hq = DSA_HEADS * DSA_HEAD_DIM
    iq0 = hq + 2 * DSA_HKV
    ik0 = iq0 + IDX_HEADS * IDX_DIM
    w_iq = w_in[:, iq0:ik0].reshape(-1, IDX_HEADS, IDX_DIM)
    w_iq = jnp.pad(w_iq, ((0, 0), (0, 0), (0, LANE - IDX_DIM))).reshape(-1, IDX_HEADS * LANE)
    w_v = w_in[:, hq + DSA_HKV:iq0].reshape(-1, DSA_KV_HEADS, DSA_HEAD_DIM)
    w_v1 = jnp.pad(w_v, ((0, 0), (0, 0), (0, DSA_HEAD_DIM))).reshape(-1, 2 * DSA_HKV)
    w = jnp.concatenate([w_in[:, :hq], w_iq, w_in[:, hq:iq0], w_v1, _pad_cols(w_in[:, ik0:], LANE)], axis=1)
    return w.astype(BF16)


def _dsa_project(x, w_in):
    bsz, t, d = x.shape
    hf, hb = _proj(x.reshape(bsz * t, d), _dsa_weight(w_in), tn=DSA_COLS // 5, ones_at=DSA_ONES)
    hf = hf.reshape(bsz, t, DSA_COLS)
    k = hf[:, :, DSA_K0:DSA_V0].reshape(bsz, t, DSA_KV_HEADS, DSA_HEAD_DIM)
    v = hf[:, :, DSA_V0:DSA_V1_0].reshape(bsz, t, DSA_KV_HEADS, DSA_HEAD_DIM)
    ik = hf[:, :, DSA_IKW0:DSA_IKW0 + IDX_DIM]
    return hf, hb.reshape(bsz, t, DSA_COLS), k, v, ik


def kernel(x_prompt, x_sample, state_gla, cache_k, cache_v, cache_idx_k, page_table, p_prompt, p_sample,
           ln_mix_g, ln_mix_b, ln_ffn_g, ln_ffn_b, w_in_gla, w_alpha_up, b_alpha, gla_norm_g, w_out_gla,
           w_in_dsa, w_out_dsa, w_ffn_gate, w_ffn_up, w_ffn_down, w_router, w_exp_gate, w_exp_up,
           w_exp_down, w_ple_proj, w_ple_gate):
    depth = ln_mix_g.shape[0]
    alpha = (2.0 * depth) ** 0.25
    d = x_prompt.shape[-1]
    xp, xs = x_prompt, x_sample
    gla_p, gla_s, kp, vp, ikp, ksm, vsm, iksm = [], [], [], [], [], [], [], []
    n_pages = page_table.shape[1]
    pp = 16 if n_pages % 16 == 0 else n_pages
    for i in range(depth):
        j = i // 2
        lg, lb = ln_mix_g[i], ln_mix_b[i]
        if i % 2 == 0:
            gw = (w_in_gla[j], w_alpha_up[j], b_alpha[j], gla_norm_g[j], w_out_gla[j], lg, lb)
            s0 = jnp.zeros((xp.shape[0], GLA_HEADS, GLA_DK, GLA_DV), F32)
            xp, sp = _gla_layer(xp, s0, *gw, alpha=alpha)
            xs, ss = _gla_layer(xs, state_gla[j], *gw, alpha=alpha)
            gla_p.append(sp)
            gla_s.append(ss)
        else:
            w_out = w_out_dsa[j].astype(BF16)
            hf, hb, k_p, v_p, ik_p = _dsa_project(xp, w_in_dsa[j])
            op = _dsa_prompt(hf, hb)
            bp, tp, _ = xp.shape
            xp = _mm_res_ln(xp.reshape(bp * tp, d), op.reshape(bp * tp, -1), w_out, lg.reshape(1, d),
                            lb.reshape(1, d), alpha=alpha).reshape(bp, tp, d)
            hf, hb, k_s, v_s, ik_s = _dsa_project(xs, w_in_dsa[j])
            sc = _dsa_sample_scores(hf, cache_idx_k[j], page_table, pp=pp)
            sel = _dsa_sample_select(sc, hf)
            n_pool = cache_k.shape[1]
            rows = PAGE_SIZE * DSA_KV_HEADS
            os_ = _dsa_sample_attend(hf, sel, cache_k[j].reshape(n_pool, rows, DSA_HEAD_DIM),
                                     cache_v[j].reshape(n_pool, rows, DSA_HEAD_DIM), page_table, pp=pp)
            bs, ts, _ = xs.shape
            xs = _mm_res_ln(xs.reshape(bs * ts, d), os_.reshape(bs * ts, -1), w_out, lg.reshape(1, d),
                            lb.reshape(1, d), alpha=alpha).reshape(bs, ts, d)
            kp.append(k_p)
            vp.append(v_p)
            ikp.append(ik_p)
            ksm.append(k_s)
            vsm.append(v_s)
            iksm.append(ik_s)
        fg, fb = ln_ffn_g[i].reshape(1, d), ln_ffn_b[i].reshape(1, d)
        outs = []
        for x, p in ((xp, p_prompt[i]), (xs, p_sample[i])):
            bsz, t, _ = x.shape
            x2 = x.reshape(bsz * t, d)
            if i % 2 == 0:
                y2 = _ffn_ln(x2, w_ffn_gate[j].astype(BF16), w_ffn_up[j].astype(BF16), w_ffn_down[j].astype(BF16),
                             fg, fb, alpha=alpha)
            else:
                gates = _router(x2, w_router[j].T)
                y2 = _moe_ln(x2, gates, w_exp_gate[j].astype(BF16), w_exp_up[j].astype(BF16),
                             w_exp_down[j].astype(BF16), fg, fb, alpha=alpha)
            y2 = _ple(y2, p.reshape(bsz * t, -1), w_ple_gate[i].astype(BF16), w_ple_proj[i].astype(BF16))
            outs.append(y2.reshape(bsz, t, d))
        xp, xs = outs
    return (xp, xs, jnp.stack(gla_p), jnp.stack(gla_s), jnp.stack(kp), jnp.stack(vp), jnp.stack(ikp),
            jnp.stack(ksm), jnp.stack(vsm), jnp.stack(iksm))
```

```python
import functools

import jax
import jax.numpy as jnp
from jax import lax
from jax.experimental import pallas as pl
from jax.experimental.pallas import tpu as pltpu

F32 = jnp.float32
BF16 = jnp.bfloat16
I32 = jnp.int32

D_MODEL = 1024
LN_EPS = 1e-5
LANE = 128

GLA_HEADS = 4
GLA_DK = 128
GLA_DV = 256
GLA_GATE_RANK = 16
GLA_TAU = 16.0
GLA_CHUNK = 64
GLA_HK = GLA_HEADS * GLA_DK
GLA_HV = GLA_HEADS * GLA_DV
GLA_COLS = 2 * GLA_HK + 2 * GLA_HV + LANE

DSA_HEADS = 8
DSA_HEAD_DIM = 128
DSA_KV_HEADS = 2
DSA_GROUP = DSA_HEADS // DSA_KV_HEADS
DSA_HKV = DSA_KV_HEADS * DSA_HEAD_DIM
IDX_HEADS = 8
IDX_DIM = 64
TOPK_MAX = 256
PAGE_SIZE = 128
DSA_Q0 = 0
DSA_IQ0 = DSA_HEADS * DSA_HEAD_DIM
DSA_K0 = DSA_IQ0 + IDX_HEADS * LANE
DSA_V0 = DSA_K0 + DSA_HKV
DSA_V1_0 = DSA_V0 + DSA_HKV
DSA_IKW0 = DSA_V1_0 + 2 * DSA_HKV
DSA_COLS = DSA_IKW0 + LANE
DSA_ONES = tuple(DSA_V1_0 + (2 * c + 1) * DSA_HEAD_DIM for c in range(DSA_KV_HEADS))

N_EXPERTS = 8

I16 = jnp.int16
NEG_INF_KEY = -2139095041
INT_MIN = -2147483648
I16_MIN = -32768
MASKED = -1e30


def _cparams(sem, vmem_mb):
    return pltpu.CompilerParams(dimension_semantics=sem, vmem_limit_bytes=vmem_mb * 2 ** 20)


def _dot(a, b):
    return jnp.dot(a, b, preferred_element_type=F32)


def _dot_nt(a, b):
    return lax.dot_general(a, b, (((1,), (1,)), ((), ())), preferred_element_type=F32)


def _dot_tn(a, b):
    return lax.dot_general(a, b, (((0,), (0,)), ((), ())), preferred_element_type=F32)


def _sigmoid(x):
    return 1.0 / (1.0 + jnp.exp(-x))


def _layer_norm(y, g, b):
    mu = jnp.mean(y, axis=-1, keepdims=True)
    yc = y - mu
    var = jnp.mean(yc * yc, axis=-1, keepdims=True)
    return yc * lax.rsqrt(var + LN_EPS) * g + b


def _row_tile(n, pref):
    t = min(n, pref)
    assert n % t == 0
    return t


def _proj_kernel(x_ref, w_ref, o_ref, ob_ref, *, tn, ones_at):
    xb = x_ref[...].astype(BF16)
    for c in range(w_ref.shape[1] // tn):
        r = _dot(xb, w_ref[:, c * tn:(c + 1) * tn])
        o_ref[:, c * tn:(c + 1) * tn] = r
        ob_ref[:, c * tn:(c + 1) * tn] = r.astype(BF16)
    for c0 in ones_at:
        ob_ref[:, c0:c0 + LANE] = jnp.ones((x_ref.shape[0], LANE), BF16)


def _proj(x, w, *, tn, ones_at=()):
    n, d = x.shape
    cols = w.shape[1]
    assert cols % tn == 0
    tm = _row_tile(n, 512)
    return pl.pallas_call(
        functools.partial(_proj_kernel, tn=tn, ones_at=ones_at),
        grid=(n // tm,),
        in_specs=[pl.BlockSpec((tm, d), lambda i: (i, 0)), pl.BlockSpec((d, cols), lambda i: (0, 0))],
        out_specs=[pl.BlockSpec((tm, cols), lambda i: (i, 0)), pl.BlockSpec((tm, cols), lambda i: (i, 0))],
        out_shape=[jax.ShapeDtypeStruct((n, cols), F32), jax.ShapeDtypeStruct((n, cols), BF16)],
        compiler_params=_cparams(("arbitrary",), 56),
        name="proj",
    )(x, w)


def _mm_res_ln_kernel(x_ref, a_ref, w_ref, g_ref, b_ref, o_ref, *, alpha):
    m = _dot(a_ref[...].astype(BF16), w_ref[...])
    o_ref[...] = _layer_norm(alpha * x_ref[...] + m, g_ref[...], b_ref[...])


def _mm_res_ln(x, a, w, g, b, *, alpha):
    n, d = x.shape
    ka = a.shape[1]
    tm = _row_tile(n, 512)
    row = lambda i: (i, 0)
    fixed = lambda i: (0, 0)
    return pl.pallas_call(
        functools.partial(_mm_res_ln_kernel, alpha=alpha),
        grid=(n // tm,),
        in_specs=[pl.BlockSpec((tm, d), row), pl.BlockSpec((tm, ka), row), pl.BlockSpec((ka, d), fixed),
                  pl.BlockSpec((1, d), fixed), pl.BlockSpec((1, d), fixed)],
        out_specs=pl.BlockSpec((tm, d), row),
        out_shape=jax.ShapeDtypeStruct((n, d), F32),
        compiler_params=_cparams(("arbitrary",), 40),
        name="mm_res_ln",
    )(x, a, w, g, b)


def _ffn_ln_kernel(x_ref, wg_ref, wu_ref, wd_ref, g_ref, b_ref, o_ref, xb_ref, acc_ref, *, alpha):
    f = pl.program_id(1)

    @pl.when(f == 0)
    def _():
        xb_ref[...] = x_ref[...].astype(BF16)
        acc_ref[...] = jnp.zeros_like(acc_ref)

    xb = xb_ref[...]
    hg = _dot(xb, wg_ref[...])
    hu = _dot(xb, wu_ref[...])
    h = (hg * _sigmoid(hg) * hu).astype(BF16)
    acc_ref[...] += _dot(h, wd_ref[...])

    @pl.when(f == pl.num_programs(1) - 1)
    def _():
        o_ref[...] = _layer_norm(alpha * x_ref[...] + acc_ref[...], g_ref[...], b_ref[...])


def _ffn_ln(x, wg, wu, wd, g, b, *, alpha, tf=512):
    n, d = x.shape
    ff = wg.shape[1]
    assert ff % tf == 0
    tm = _row_tile(n, 1024)
    return pl.pallas_call(
        functools.partial(_ffn_ln_kernel, alpha=alpha),
        grid=(n // tm, ff // tf),
        in_specs=[pl.BlockSpec((tm, d), lambda i, f: (i, 0)),
                  pl.BlockSpec((d, tf), lambda i, f: (0, f)),
                  pl.BlockSpec((d, tf), lambda i, f: (0, f)),
                  pl.BlockSpec((tf, d), lambda i, f: (f, 0)),
                  pl.BlockSpec((1, d), lambda i, f: (0, 0)),
                  pl.BlockSpec((1, d), lambda i, f: (0, 0))],
        out_specs=pl.BlockSpec((tm, d), lambda i, f: (i, 0)),
        out_shape=jax.ShapeDtypeStruct((n, d), F32),
        scratch_shapes=[pltpu.VMEM((tm, d), BF16), pltpu.VMEM((tm, d), F32)],
        compiler_params=_cparams(("arbitrary", "arbitrary"), 48),
        name="ffn_ln",
    )(x, wg, wu, wd, g, b)


def _ple_kernel(x_ref, p_ref, wg_ref, wp_ref, o_ref):
    x = x_ref[...]
    gate = _sigmoid(_dot(x.astype(BF16), wg_ref[...]))
    o_ref[...] = x + gate * _dot(p_ref[...].astype(BF16), wp_ref[...])


def _ple(x, p, wg, wp):
    n, d = x.shape
    pd = p.shape[1]
    tm = _row_tile(n, 512)
    row = lambda i: (i, 0)
    fixed = lambda i: (0, 0)
    return pl.pallas_call(
        _ple_kernel,
        grid=(n // tm,),
        in_specs=[pl.BlockSpec((tm, d), row), pl.BlockSpec((tm, pd), row),
                  pl.BlockSpec((d, d), fixed), pl.BlockSpec((pd, d), fixed)],
        out_specs=pl.BlockSpec((tm, d), row),
        out_shape=jax.ShapeDtypeStruct((n, d), F32),
        compiler_params=_cparams(("arbitrary",), 40),
        name="ple",
    )(x, p, wg, wp)


def _gla_kernel(q_ref, k_ref, v_ref, g_ref, a_ref, wup_ref, ba_ref, ng_ref, s0_ref, o_ref, st_out_ref, st_ref,
                *, c, valid, nbs):
    t = pl.program_id(1)

    @pl.when(t == 0)
    def _():
        st_ref[...] = s0_ref[...]

    row = lax.broadcasted_iota(I32, (c, c), 0)
    col = lax.broadcasted_iota(I32, (c, c), 1)
    causal = row >= col
    for bi in range(nbs):
        logit = _dot(a_ref[bi].astype(BF16), wup_ref[...]) + ba_ref[...]
        log_a = (jnp.minimum(logit, 0.0) - jnp.log(1.0 + jnp.exp(-jnp.abs(logit)))) * (1.0 / GLA_TAU)
        if valid < c:
            log_a = jnp.where(lax.broadcasted_iota(I32, log_a.shape, 0) < valid, log_a, 0.0)
        bcum = jnp.dot(causal.astype(F32), log_a, preferred_element_type=F32, precision=lax.Precision.HIGHEST)
        blast = bcum[c - 1:c, :]
        q = q_ref[bi] * (GLA_DK ** -0.5)
        k = k_ref[bi]
        q_dec = (q * jnp.exp(bcum)).astype(BF16)
        k_inv = (k * jnp.exp(-bcum)).astype(BF16)
        k_dec = (k * jnp.exp(blast - bcum)).astype(BF16)
        decay = jnp.exp(blast)
        v = v_ref[bi].astype(BF16)
        for h in range(GLA_HEADS):
            sk = slice(h * GLA_DK, (h + 1) * GLA_DK)
            sv = slice(h * GLA_DV, (h + 1) * GLA_DV)
            att = jnp.where(causal, _dot_nt(q_dec[:, sk], k_inv[:, sk]), 0.0).astype(BF16)
            st = st_ref[bi, h]
            o = _dot(att, v[:, sv]) + _dot_nt(q_dec[:, sk], st.astype(BF16))
            st_ref[bi, h] = st * decay[:, sk] + _dot_tn(v[:, sv], k_dec[:, sk])
            mu = jnp.mean(o, axis=-1, keepdims=True)
            oc = o - mu
            var = jnp.mean(oc * oc, axis=-1, keepdims=True)
            on = oc * lax.rsqrt(var + LN_EPS) * ng_ref[:, sv]
            gg = g_ref[bi, :, sv]
            o_ref[bi, :, sv] = (on * (gg * _sigmoid(gg))).astype(BF16)

    @pl.when(t == pl.num_programs(1) - 1)
    def _():
        st_out_ref[...] = st_ref[...]


def _gla_core(h, s0_t, wup, ba, ng, *, c, valid):
    bsz, tlen, _ = h.shape
    assert tlen % c == 0
    nbs = 4 if bsz % 4 == 0 else 1
    v_blk = (2 * GLA_HK) // GLA_HV
    state_spec = pl.BlockSpec((nbs, GLA_HEADS, GLA_DV, GLA_DK), lambda b, t: (b, 0, 0, 0))
    return pl.pallas_call(
        functools.partial(_gla_kernel, c=c, valid=valid, nbs=nbs),
        grid=(bsz // nbs, tlen // c),
        in_specs=[pl.BlockSpec((nbs, c, GLA_HK), lambda b, t: (b, t, 0)),
                  pl.BlockSpec((nbs, c, GLA_HK), lambda b, t: (b, t, 1)),
                  pl.BlockSpec((nbs, c, GLA_HV), lambda b, t: (b, t, v_blk)),
                  pl.BlockSpec((nbs, c, GLA_HV), lambda b, t: (b, t, v_blk + 1)),
                  pl.BlockSpec((nbs, c, LANE), lambda b, t: (b, t, (2 * GLA_HK + 2 * GLA_HV) // LANE)),
                  pl.BlockSpec((LANE, GLA_HK), lambda b, t: (0, 0)),
                  pl.BlockSpec((1, GLA_HK), lambda b, t: (0, 0)),
                  pl.BlockSpec((1, GLA_HV), lambda b, t: (0, 0)),
                  state_spec],
        out_specs=[pl.BlockSpec((nbs, c, GLA_HV), lambda b, t: (b, t, 0)), state_spec],
        out_shape=[jax.ShapeDtypeStruct((bsz, tlen, GLA_HV), BF16),
                   jax.ShapeDtypeStruct((bsz, GLA_HEADS, GLA_DV, GLA_DK), F32)],
        scratch_shapes=[pltpu.VMEM((nbs, GLA_HEADS, GLA_DV, GLA_DK), F32)],
        compiler_params=_cparams(("arbitrary", "arbitrary"), 32),
        name="gla_core",
    )(h, h, h, h, h, wup, ba, ng, s0_t)


def _sortable(s):
    bits = lax.bitcast_convert_type(s, I32)
    return bits ^ (lax.shift_right_arithmetic(bits, 31) & 0x7FFFFFFF)


def _lane_fold(m):
    acc = m[:, 0:LANE]
    for i in range(1, m.shape[1] // LANE):
        acc = acc + m[:, i * LANE:(i + 1) * LANE]
    return acc


def _lane_fold_max(m):
    acc = m[:, 0:LANE]
    for i in range(1, m.shape[1] // LANE):
        acc = jnp.maximum(acc, m[:, i * LANE:(i + 1) * LANE])
    return acc


def _bias_bits(sel):
    return lax.bitcast_convert_type(jnp.where(sel, 0.0, MASKED).astype(F32), I32)


def _kth_largest_key(count_ge, rows, k):
    kf = float(k)
    thr0 = jnp.where(count_ge(jnp.zeros((rows, 1), I32)) >= kf, 0, INT_MIN).astype(I32)

    def bit_body(i, thr):
        cand = thr | lax.shift_left(jnp.int32(1), 30 - i)
        return jnp.where(count_ge(cand) >= kf, cand, thr)

    return lax.fori_loop(0, 31, bit_body, thr0)


def _split_key(key):
    hi = lax.shift_right_arithmetic(key, 16).astype(I16)
    lo = lax.shift_right_arithmetic(lax.shift_left(key ^ 0x8000, 16), 16).astype(I16)
    return hi, lo


def _kth_largest16(count_ge, rows, k):
    thr0 = jnp.where(count_ge(jnp.zeros((rows, 1), I16)) >= k, 0, I16_MIN).astype(I32)

    def bit_body(i, thr):
        cand = thr | lax.shift_left(jnp.int32(1), 14 - i)
        return jnp.where(count_ge(cand.astype(I16)) >= k, cand, thr)

    return lax.fori_loop(0, 15, bit_body, thr0)


def _strict_upper(n):
    return (lax.broadcasted_iota(I32, (n, n), 0) < lax.broadcasted_iota(I32, (n, n), 1)).astype(BF16)


def _dsa_prompt_kernel(q_ref, iq_ref, iw_ref, ikw_ref, k_ref, v1_ref, o_ref,
                       hi_ref, lo_ref, bias_ref, iqs_ref, qg_ref, d_ref, m_ref, acc_ref, *, tq, tk, topk):
    j = pl.program_id(1)
    nkt = ((j + 1) * tq + tk - 1) // tk
    gq = DSA_GROUP * tq
    rp = 16
    for h in range(IDX_HEADS):
        iqs_ref[h * tq:(h + 1) * tq, :] = iq_ref[0, :, h * LANE:(h + 1) * LANE]
    for c in range(DSA_KV_HEADS):
        for g in range(DSA_GROUP):
            hh = c * DSA_GROUP + g
            qg_ref[c, g * tq:(g + 1) * tq, :] = q_ref[0, :, hh * DSA_HEAD_DIM:(hh + 1) * DSA_HEAD_DIM]
    ws = iw_ref[0][:, IDX_DIM:IDX_DIM + IDX_HEADS] * (IDX_HEADS ** -0.5) * (IDX_DIM ** -0.5)

    def score_tile(kt, diagonal):
        ikt = ikw_ref[0, pl.ds(pl.multiple_of(kt * tk, tk), tk), :]
        d_ref[...] = _dot_nt(iqs_ref[...], ikt)
        for r in range(tq // rp):
            s = jnp.zeros((rp, tk), F32)
            for h in range(IDX_HEADS):
                dh = d_ref[h * tq + r * rp:h * tq + (r + 1) * rp, :]
                s = s + ws[r * rp:(r + 1) * rp, h:h + 1] * jnp.maximum(dh, 0.0)
            key = _sortable(s + 0.0)
            if diagonal:
                qpos = j * tq + r * rp + lax.broadcasted_iota(I32, (rp, tk), 0)
                kpos = kt * tk + lax.broadcasted_iota(I32, (rp, tk), 1)
                key = jnp.where(kpos <= qpos, key, INT_MIN)
            hi, lo = _split_key(key)
            hi_ref[kt, r * rp:(r + 1) * rp, :] = hi
            lo_ref[kt, r * rp:(r + 1) * rp, :] = lo

    def score_body(kt, carry):
        score_tile(kt, False)
        return carry

    lax.fori_loop(0, nkt - 1, score_body, 0)
    score_tile(nkt - 1, True)

    one, zero = jnp.int16(1), jnp.int16(0)

    def counts(indicator):
        def body(kt, acc):
            return acc + _lane_fold(indicator(kt))
        acc = lax.fori_loop(0, nkt, body, jnp.zeros((tq, LANE), I16))
        return jnp.sum(acc.astype(F32), axis=1, keepdims=True)

    kf = float(topk)
    thr_hi = _kth_largest16(lambda c: counts(lambda kt: jnp.where(hi_ref[kt] >= c, one, zero)), tq, kf)
    th = thr_hi.astype(I16)
    n_gt_hi = counts(lambda kt: jnp.where(hi_ref[kt] > th, one, zero))

    def bucket_body(kt, carry):
        lo_ref[kt] = jnp.where(hi_ref[kt] == th, lo_ref[kt], jnp.int16(I16_MIN))
        return carry

    lax.fori_loop(0, nkt, bucket_body, 0)
    thr_lo = _kth_largest16(lambda c: counts(lambda kt: jnp.where(lo_ref[kt] >= c, one, zero)), tq, kf - n_gt_hi)
    tl = thr_lo.astype(I16)

    def is_gt(kt):
        h = hi_ref[kt]
        return jnp.where(h > th, one, jnp.where(h == th, jnp.where(lo_ref[kt] > tl, one, zero), zero))

    def is_eq(kt):
        return jnp.where(hi_ref[kt] == th, jnp.where(lo_ref[kt] == tl, one, zero), zero)

    def is_valid(kt):
        return jnp.where(hi_ref[kt] > jnp.int16(I16_MIN), one, zero)

    n_gt = counts(is_gt)
    n_eq = counts(is_eq)
    need = kf - n_gt
    tie_row = jnp.logical_and(n_eq > need, thr_hi > I16_MIN)
    any_tie = jnp.max(jnp.where(tie_row, 1.0, 0.0)) > 0.0

    def write_bias(kt, sel_f):
        bias_ref[kt] = (sel_f - 1.0) * (-MASKED)

    @pl.when(jnp.logical_not(any_tie))
    def _():
        def body(kt, carry):
            sel = jnp.where(hi_ref[kt] > jnp.int16(I16_MIN), is_gt(kt) + is_eq(kt), zero)
            write_bias(kt, sel.astype(F32))
            return carry
        lax.fori_loop(0, nkt, body, 0)

    @pl.when(any_tie)
    def _():
        upper = _strict_upper(tk)

        def body(kt, seen):
            eq = is_eq(kt).astype(F32)
            before = seen + _dot(eq.astype(BF16), upper)
            take = jnp.where(before < need, eq, 0.0)
            write_bias(kt, (is_gt(kt).astype(F32) + take) * is_valid(kt).astype(F32))
            return seen + jnp.sum(eq, axis=1, keepdims=True)
        lax.fori_loop(0, nkt, body, jnp.zeros((tq, 1), F32))

    m_ref[...] = jnp.full(m_ref.shape, MASKED, F32)
    acc_ref[...] = jnp.zeros_like(acc_ref)
    c2 = (DSA_HEAD_DIM ** -0.5) * 1.4426950408889634
    dh = DSA_HEAD_DIM

    def attend_body(kt, carry):
        rows = pl.ds(pl.multiple_of(kt * tk, tk), tk)
        bias = bias_ref[kt][None]
        for c in range(DSA_KV_HEADS):
            kc = k_ref[0, rows, c * dh:(c + 1) * dh]
            v1 = v1_ref[0, rows, c * 2 * dh:(c + 1) * 2 * dh]
            x = (_dot_nt(qg_ref[c], kc).reshape(DSA_GROUP, tq, tk) + bias).reshape(gq, tk)
            m_old = m_ref[c]
            m_new = jnp.maximum(m_old, jnp.max(x, axis=1, keepdims=True))
            p = jnp.exp2((x - jnp.concatenate([m_new] * (tk // LANE), axis=1)) * c2).astype(BF16)
            alpha = jnp.exp2((m_old - m_new) * c2)
            acc_ref[c] = jnp.concatenate([alpha, alpha], axis=1) * acc_ref[c] + _dot(p, v1)
            m_ref[c] = m_new
        return carry

    lax.fori_loop(0, nkt, attend_body, 0)
    for c in range(DSA_KV_HEADS):
        oc = acc_ref[c, :, 0:dh] / acc_ref[c, :, dh:2 * dh]
        for g in range(DSA_GROUP):
            hh = c * DSA_GROUP + g
            o_ref[0, :, hh * dh:(hh + 1) * dh] = oc[g * tq:(g + 1) * tq].astype(BF16)


def _dsa_prompt(hf, hb, *, tq=128, tk=1024):
    bsz, s, _ = hf.shape
    tk = min(tk, s)
    assert s % tq == 0 and s % tk == 0 and tk % tq == 0
    topk = min(TOPK_MAX, s // 4)
    hq = DSA_HEADS * DSA_HEAD_DIM
    gq = DSA_GROUP * tq
    return pl.pallas_call(
        functools.partial(_dsa_prompt_kernel, tq=tq, tk=tk, topk=topk),
        grid=(bsz, s // tq),
        in_specs=[pl.BlockSpec((1, tq, hq), lambda b, j: (b, j, DSA_Q0 // hq)),
                  pl.BlockSpec((1, tq, IDX_HEADS * LANE), lambda b, j: (b, j, DSA_IQ0 // (IDX_HEADS * LANE))),
                  pl.BlockSpec((1, tq, LANE), lambda b, j: (b, j, DSA_IKW0 // LANE)),
                  pl.BlockSpec((1, s, LANE), lambda b, j: (b, 0, DSA_IKW0 // LANE)),
                  pl.BlockSpec((1, s, DSA_HKV), lambda b, j: (b, 0, DSA_K0 // DSA_HKV)),
                  pl.BlockSpec((1, s, 2 * DSA_HKV), lambda b, j: (b, 0, DSA_V1_0 // (2 * DSA_HKV)))],
        out_specs=pl.BlockSpec((1, tq, hq), lambda b, j: (b, j, 0)),
        out_shape=jax.ShapeDtypeStruct((bsz, s, hq), BF16),
        scratch_shapes=[pltpu.VMEM((s // tk, tq, tk), I16),
                        pltpu.VMEM((s // tk, tq, tk), I16),
                        pltpu.VMEM((s // tk, tq, tk), F32),
                        pltpu.VMEM((IDX_HEADS * tq, LANE), BF16),
                        pltpu.VMEM((DSA_KV_HEADS, gq, DSA_HEAD_DIM), BF16),
                        pltpu.VMEM((IDX_HEADS * tq, tk), F32),
                        pltpu.VMEM((DSA_KV_HEADS, gq, LANE), F32),
                        pltpu.VMEM((DSA_KV_HEADS, gq, 2 * DSA_HEAD_DIM), F32)],
        compiler_params=_cparams(("arbitrary", "arbitrary"), 56),
        name="dsa_prompt",
    )(hb, hb, hf, hb, hb, hb)


def _sample_ws(iw_ref):
    return iw_ref[0][:, IDX_DIM:IDX_DIM + IDX_HEADS] * (IDX_HEADS ** -0.5)


def _sample_scores(d, ws, t):
    s = jnp.zeros((t, d.shape[1]), F32)
    for h in range(IDX_HEADS):
        s = s + ws[:, h:h + 1] * jnp.maximum(d[h * t:(h + 1) * t] * (IDX_DIM ** -0.5), 0.0)
    return s + 0.0


def _dsa_sample_scores_kernel(pt_ref, iq_ref, iw_ref, *refs, pp, t):
    page_refs, o_ref = refs[:pp], refs[pp]
    iq_stack = jnp.concatenate([iq_ref[0, :, h * LANE:h * LANE + IDX_DIM] for h in range(IDX_HEADS)], axis=0)
    ik_t = jnp.concatenate([r[0].astype(BF16) for r in page_refs], axis=1)
    o_ref[0] = _sample_scores(_dot(iq_stack.astype(BF16), ik_t), _sample_ws(iw_ref), t)


def _dsa_sample_scores(hf, cache_ik_t, page_table, *, pp):
    bsz, t, _ = hf.shape
    n_pages = page_table.shape[1]
    assert n_pages % pp == 0
    iq_w = IDX_HEADS * LANE

    def page_spec(i):
        return pl.BlockSpec((1, IDX_DIM, PAGE_SIZE), lambda b, j, pt: (pt[b, j * pp + i], 0, 0))

    grid_spec = pltpu.PrefetchScalarGridSpec(
        num_scalar_prefetch=1, grid=(bsz, n_pages // pp),
        in_specs=[pl.BlockSpec((1, t, iq_w), lambda b, j, pt: (b, 0, DSA_IQ0 // iq_w)),
                  pl.BlockSpec((1, t, LANE), lambda b, j, pt: (b, 0, DSA_IKW0 // LANE))]
                 + [page_spec(i) for i in range(pp)],
        out_specs=pl.BlockSpec((1, t, pp * PAGE_SIZE), lambda b, j, pt: (b, 0, j)))
    return pl.pallas_call(
        functools.partial(_dsa_sample_scores_kernel, pp=pp, t=t),
        grid_spec=grid_spec,
        out_shape=jax.ShapeDtypeStruct((bsz, t, n_pages * PAGE_SIZE), F32),
        compiler_params=_cparams(("arbitrary", "arbitrary"), 32),
        name="dsa_sample_scores",
    )(page_table, hf, hf, *([cache_ik_t] * pp))


def _dsa_sample_select_kernel(sp_ref, iq_ref, iw_ref, ikw_ref, o_ref, keys_ref, *, t, past, new_w, topk, ck):
    iq_stack = jnp.concatenate([iq_ref[0, :, h * LANE:(h + 1) * LANE] for h in range(IDX_HEADS)], axis=0)
    ik_new = jnp.concatenate([ikw_ref[0], jnp.zeros((new_w - t, LANE), F32)], axis=0)
    d_new = _dot_nt(iq_stack.astype(BF16), ik_new.astype(BF16))
    s_new = _sample_scores(d_new, _sample_ws(iw_ref), t)
    row = lax.broadcasted_iota(I32, (t, new_w), 0)
    lane = lax.broadcasted_iota(I32, (t, new_w), 1)
    s_new = jnp.where(lane <= row, s_new, -jnp.inf)
    keys_ref[:, 0:past] = _sortable(sp_ref[0])
    keys_ref[:, past:past + new_w] = _sortable(s_new)
    n_ck = (past + new_w) // ck

    def counts(pred):
        acc = jnp.zeros((t, LANE), F32)
        for i in range(n_ck):
            acc = acc + _lane_fold(jnp.where(pred(keys_ref[:, i * ck:(i + 1) * ck]), 1.0, 0.0))
        return jnp.sum(acc, axis=1, keepdims=True)

    thr = _kth_largest_key(lambda cand: counts(lambda kk: kk >= cand), t, topk)
    n_gt = counts(lambda kk: kk > thr)
    n_eq = counts(lambda kk: kk == thr)
    need = float(topk) - n_gt
    tie_row = jnp.logical_and(n_eq > need, thr > NEG_INF_KEY)
    any_tie = jnp.max(jnp.where(tie_row, 1.0, 0.0)) > 0.0

    @pl.when(jnp.logical_not(any_tie))
    def _():
        thr_eff = jnp.maximum(thr, NEG_INF_KEY + 1)
        o_ref[0] = jnp.where(keys_ref[...] >= thr_eff, 1.0, 0.0)

    @pl.when(any_tie)
    def _():
        upper = _strict_upper(ck)
        seen = jnp.zeros((t, 1), F32)
        for i in range(n_ck):
            kk = keys_ref[:, i * ck:(i + 1) * ck]
            eq = kk == thr
            before = seen + _dot(jnp.where(eq, 1.0, 0.0).astype(BF16), upper)
            sel = jnp.logical_and(kk > NEG_INF_KEY, jnp.logical_or(kk > thr, jnp.logical_and(eq, before < need)))
            o_ref[0, :, i * ck:(i + 1) * ck] = jnp.where(sel, 1.0, 0.0)
            seen = seen + jnp.sum(jnp.where(eq, 1.0, 0.0), axis=1, keepdims=True)


def _dsa_sample_select(scores_past, hf, *, new_w=512, ck=512):
    bsz, t, past = scores_past.shape
    assert past % ck == 0 and new_w % ck == 0
    topk = min(TOPK_MAX, (past + t) // 4)
    iq_w = IDX_HEADS * LANE
    total = past + new_w
    return pl.pallas_call(
        functools.partial(_dsa_sample_select_kernel, t=t, past=past, new_w=new_w, topk=topk, ck=ck),
        grid=(bsz,),
        in_specs=[pl.BlockSpec((1, t, past), lambda b: (b, 0, 0)),
                  pl.BlockSpec((1, t, iq_w), lambda b: (b, 0, DSA_IQ0 // iq_w)),
                  pl.BlockSpec((1, t, LANE), lambda b: (b, 0, DSA_IKW0 // LANE)),
                  pl.BlockSpec((1, t, LANE), lambda b: (b, 0, DSA_IKW0 // LANE))],
        out_specs=pl.BlockSpec((1, t, total), lambda b: (b, 0, 0)),
        out_shape=jax.ShapeDtypeStruct((bsz, t, total), F32),
        scratch_shapes=[pltpu.VMEM((t, total), I32)],
        compiler_params=_cparams(("arbitrary",), 32),
        name="dsa_sample_select",
    )(scores_past, hf, hf, hf)


def _dsa_sample_attend_kernel(pt_ref, q_ref, kn_ref, vn_ref, selp_ref, seln_ref, *refs, pp, t, new_w):
    k_pages, v_pages = refs[:pp], refs[pp:2 * pp]
    o_ref, m_ref, l_ref, acc_ref = refs[2 * pp:]
    j = pl.program_id(1)
    gt = DSA_GROUP * t

    @pl.when(j == 0)
    def _():
        m_ref[...] = jnp.full(m_ref.shape, MASKED, F32)
        l_ref[...] = jnp.zeros_like(l_ref)
        acc_ref[...] = jnp.zeros_like(acc_ref)

    q_groups = [jnp.concatenate([q_ref[0, :, (c * DSA_GROUP + g) * DSA_HEAD_DIM:(c * DSA_GROUP + g + 1) * DSA_HEAD_DIM]
                                 for g in range(DSA_GROUP)], axis=0).astype(BF16) for c in range(DSA_KV_HEADS)]

    def attend(kcs, vcs, sel):
        n = kcs[0].shape[0]
        for c in range(DSA_KV_HEADS):
            kc, vc = kcs[c], vcs[c]
            logits = (_dot_nt(q_groups[c], kc) * (DSA_HEAD_DIM ** -0.5)).reshape(DSA_GROUP, t, n)
            masked = jnp.where(sel[None], logits, MASKED).reshape(gt, n)
            m_old = m_ref[c]
            m_new = jnp.maximum(m_old, jnp.max(masked, axis=1, keepdims=True))
            p = jnp.where(sel[None], jnp.exp(masked - m_new).reshape(DSA_GROUP, t, n), 0.0).reshape(gt, n)
            alpha = jnp.exp(m_old - m_new)
            l_ref[c] = alpha * l_ref[c] + jnp.sum(p, axis=1, keepdims=True)
            acc_ref[c] = alpha * acc_ref[c] + _dot(p.astype(BF16), vc)
            m_ref[c] = m_new

    def head_rows(pages, c):
        return jnp.concatenate([r[0, pl.ds(c, PAGE_SIZE, stride=DSA_KV_HEADS), :].astype(BF16) for r in pages], axis=0)

    attend([head_rows(k_pages, c) for c in range(DSA_KV_HEADS)],
           [head_rows(v_pages, c) for c in range(DSA_KV_HEADS)], selp_ref[0] != 0.0)

    @pl.when(j == pl.num_programs(1) - 1)
    def _():
        pad = jnp.zeros((new_w - t, DSA_HEAD_DIM), F32)

        def new_rows(ref, c):
            return jnp.concatenate([ref[0, :, c * DSA_HEAD_DIM:(c + 1) * DSA_HEAD_DIM], pad], axis=0).astype(BF16)

        attend([new_rows(kn_ref, c) for c in range(DSA_KV_HEADS)],
               [new_rows(vn_ref, c) for c in range(DSA_KV_HEADS)], seln_ref[0] != 0.0)
        for c in range(DSA_KV_HEADS):
            oc = acc_ref[c] / l_ref[c]
            for g in range(DSA_GROUP):
                hh = c * DSA_GROUP + g
                o_ref[0, :, hh * DSA_HEAD_DIM:(hh + 1) * DSA_HEAD_DIM] = oc[g * t:(g + 1) * t].astype(BF16)


def _dsa_sample_attend(hf, sel, cache_k, cache_v, page_table, *, pp, new_w=512):
    bsz, t, _ = hf.shape
    n_pages = page_table.shape[1]
    past = n_pages * PAGE_SIZE
    assert n_pages % pp == 0 and past % new_w == 0
    hq = DSA_HEADS * DSA_HEAD_DIM
    gt = DSA_GROUP * t

    def page_spec(i):
        return pl.BlockSpec((1, PAGE_SIZE * DSA_KV_HEADS, DSA_HEAD_DIM), lambda b, j, pt: (pt[b, j * pp + i], 0, 0))

    grid_spec = pltpu.PrefetchScalarGridSpec(
        num_scalar_prefetch=1, grid=(bsz, n_pages // pp),
        in_specs=[pl.BlockSpec((1, t, hq), lambda b, j, pt: (b, 0, DSA_Q0 // hq)),
                  pl.BlockSpec((1, t, DSA_HKV), lambda b, j, pt: (b, 0, DSA_K0 // DSA_HKV)),
                  pl.BlockSpec((1, t, DSA_HKV), lambda b, j, pt: (b, 0, DSA_V0 // DSA_HKV)),
                  pl.BlockSpec((1, t, pp * PAGE_SIZE), lambda b, j, pt: (b, 0, j)),
                  pl.BlockSpec((1, t, new_w), lambda b, j, pt: (b, 0, past // new_w))]
                 + [page_spec(i) for i in range(pp)] * 2,
        out_specs=pl.BlockSpec((1, t, hq), lambda b, j, pt: (b, 0, 0)),
        scratch_shapes=[pltpu.VMEM((DSA_KV_HEADS, gt, 1), F32),
                        pltpu.VMEM((DSA_KV_HEADS, gt, 1), F32),
                        pltpu.VMEM((DSA_KV_HEADS, gt, DSA_HEAD_DIM), F32)])
    return pl.pallas_call(
        functools.partial(_dsa_sample_attend_kernel, pp=pp, t=t, new_w=new_w),
        grid_spec=grid_spec,
        out_shape=jax.ShapeDtypeStruct((bsz, t, hq), BF16),
        compiler_params=_cparams(("arbitrary", "arbitrary"), 48),
        name="dsa_sample_attend",
    )(page_table, hf, hf, hf, sel, sel, *([cache_k] * pp), *([cache_v] * pp))


def _router_kernel(x_ref, wt_ref, o_ref):
    x = x_ref[...]
    wt = wt_ref[...]
    xh = x.astype(BF16)
    xl = (x - xh.astype(F32)).astype(BF16)
    wh = wt.astype(BF16)
    wl = (wt - wh.astype(F32)).astype(BF16)
    logits = _dot_nt(wh, xh) + (_dot_nt(wl, xh) + _dot_nt(wh, xl))
    sub = lax.broadcasted_iota(I32, logits.shape, 0)
    l1 = jnp.max(logits, axis=0, keepdims=True)
    i1 = jnp.min(jnp.where(logits == l1, sub, N_EXPERTS), axis=0, keepdims=True)
    rest = jnp.where(sub == i1, -jnp.inf, logits)
    l2 = jnp.max(rest, axis=0, keepdims=True)
    i2 = jnp.min(jnp.where(rest == l2, sub, N_EXPERTS), axis=0, keepdims=True)
    e2 = jnp.exp(l2 - l1)
    denom = 1.0 + e2
    o_ref[...] = jnp.where(sub == i1, 1.0 / denom, jnp.where(sub == i2, e2 / denom, 0.0))


def _router(x, w_t):
    n, d = x.shape
    tm = _row_tile(n, 512)
    return pl.pallas_call(
        _router_kernel,
        grid=(n // tm,),
        in_specs=[pl.BlockSpec((tm, d), lambda i: (i, 0)), pl.BlockSpec((N_EXPERTS, d), lambda i: (0, 0))],
        out_specs=pl.BlockSpec((N_EXPERTS, tm), lambda i: (0, i)),
        out_shape=jax.ShapeDtypeStruct((N_EXPERTS, n), F32),
        compiler_params=_cparams(("arbitrary",), 32),
        name="router",
    )(x, w_t)


MOE_ROWS = 128


def _moe_ln_kernel(cnt_ref, x_ref, gt_ref, wg_ref, wu_ref, wd_ref, g_ref, b_ref, o_ref,
                   xb_ref, rank_ref, xg_ref, acc_ref, y_ref, upper_ref, *, alpha, arms):
    i, e, f = pl.program_id(0), pl.program_id(1), pl.program_id(2)
    tm = x_ref.shape[0]
    first_f = f == 0
    last_f = f == pl.num_programs(2) - 1
    nb = (cnt_ref[i * N_EXPERTS + e] + MOE_ROWS - 1) // MOE_ROWS

    @pl.when(jnp.logical_and(i == 0, jnp.logical_and(e == 0, first_f)))
    def _():
        upper_ref[...] = _strict_upper(tm)

    @pl.when(jnp.logical_and(e == 0, first_f))
    def _():
        xb_ref[...] = x_ref[...].astype(BF16)
        y_ref[...] = jnp.zeros_like(y_ref)
        chosen = jnp.where(gt_ref[...] > 0.0, 1.0, 0.0).astype(BF16)
        rank_ref[...] = _dot(chosen, upper_ref[...])

    ge = gt_ref[pl.ds(e, 1), :]
    slot = jnp.where(ge > 0.0, rank_ref[pl.ds(e, 1), :], -1.0)

    def one_hot(start, rows):
        r = (start + lax.broadcasted_iota(I32, (rows, 1), 0)).astype(F32)
        return slot == r

    def for_arms(fn):
        for size in arms:
            start = pl.multiple_of((nb & ~(2 * size - 1)) * MOE_ROWS, MOE_ROWS)

            @pl.when((nb & size) != 0)
            def _():
                fn(start, size * MOE_ROWS)

    @pl.when(first_f)
    def _():
        def gather(start, rows):
            hot = jnp.where(one_hot(start, rows), 1.0, 0.0).astype(BF16)
            xg_ref[pl.ds(start, rows), :] = _dot(hot, xb_ref[...]).astype(BF16)
            acc_ref[pl.ds(start, rows), :] = jnp.zeros((rows, acc_ref.shape[1]), F32)
        for_arms(gather)

    def expert(start, rows):
        xs = xg_ref[pl.ds(start, rows), :]
        hg = _dot(xs, wg_ref[0])
        hu = _dot(xs, wu_ref[0])
        h = (hg * _sigmoid(hg) * hu).astype(BF16)
        acc_ref[pl.ds(start, rows), :] += _dot(h, wd_ref[0])
    for_arms(expert)

    @pl.when(last_f)
    def _():
        def scatter(start, rows):
            hot = one_hot(start, rows)
            gate = jnp.sum(jnp.where(hot, ge, 0.0), axis=1, keepdims=True)
            v = (gate * acc_ref[pl.ds(start, rows), :]).astype(BF16)
            y_ref[...] += _dot_tn(jnp.where(hot, 1.0, 0.0).astype(BF16), v)
        for_arms(scatter)

    @pl.when(jnp.logical_and(e == N_EXPERTS - 1, last_f))
    def _():
        o_ref[...] = _layer_norm(alpha * x_ref[...] + y_ref[...], g_ref[...], b_ref[...])


def _moe_ln(x, gates_t, wg, wu, wd, g, b, *, alpha, tf=896):
    n, d = x.shape
    ff = wg.shape[2]
    assert ff % tf == 0
    tm = _row_tile(n, 1024)
    assert tm % MOE_ROWS == 0
    max_blocks = tm // MOE_ROWS
    arms = tuple(s for s in (64, 32, 16, 8, 4, 2, 1) if s <= max_blocks)
    assert arms[0] == max_blocks
    counts = jnp.sum((gates_t > 0.0).reshape(N_EXPERTS, n // tm, tm), axis=2).astype(I32).T.reshape(-1)
    grid_spec = pltpu.PrefetchScalarGridSpec(
        num_scalar_prefetch=1, grid=(n // tm, N_EXPERTS, ff // tf),
        in_specs=[pl.BlockSpec((tm, d), lambda i, e, f, c: (i, 0)),
                  pl.BlockSpec((N_EXPERTS, tm), lambda i, e, f, c: (0, i)),
                  pl.BlockSpec((1, d, tf), lambda i, e, f, c: (e, 0, f)),
                  pl.BlockSpec((1, d, tf), lambda i, e, f, c: (e, 0, f)),
                  pl.BlockSpec((1, tf, d), lambda i, e, f, c: (e, f, 0)),
                  pl.BlockSpec((1, d), lambda i, e, f, c: (0, 0)),
                  pl.BlockSpec((1, d), lambda i, e, f, c: (0, 0))],
        out_specs=pl.BlockSpec((tm, d), lambda i, e, f, c: (i, 0)),
        scratch_shapes=[pltpu.VMEM((tm, d), BF16), pltpu.VMEM((N_EXPERTS, tm), F32), pltpu.VMEM((tm, d), BF16),
                        pltpu.VMEM((tm, d), F32), pltpu.VMEM((tm, d), F32), pltpu.VMEM((tm, tm), BF16)])
    return pl.pallas_call(
        functools.partial(_moe_ln_kernel, alpha=alpha, arms=arms),
        grid_spec=grid_spec,
        out_shape=jax.ShapeDtypeStruct((n, d), F32),
        compiler_params=_cparams(("arbitrary", "arbitrary", "arbitrary"), 56),
        name="moe_ln",
    )(counts, x, gates_t, wg, wu, wd, g, b)


def _pad_cols(w, cols):
    return jnp.pad(w, ((0, 0), (0, cols - w.shape[1])))


def _gla_layer(x, s0, w_in, w_up, b_a, norm_g, w_out, ln_g, ln_b, *, alpha):
    bsz, t, d = x.shape
    x2 = x.reshape(bsz * t, d)
    hf, _ = _proj(x2, _pad_cols(w_in, GLA_COLS).astype(BF16), tn=640)
    hf = hf.reshape(bsz, t, GLA_COLS)
    c = GLA_CHUNK if t % GLA_CHUNK == 0 else t
    valid = c
    if c < GLA_CHUNK:
        hf = jnp.pad(hf, ((0, 0), (0, GLA_CHUNK - c), (0, 0)))
        c = GLA_CHUNK
    w_up_pad = jnp.pad(w_up, ((0, LANE - GLA_GATE_RANK), (0, 0))).astype(BF16)
    o, st = _gla_core(hf, jnp.swapaxes(s0, 2, 3), w_up_pad, b_a.reshape(1, GLA_HK), norm_g.reshape(1, GLA_HV),
                      c=c, valid=valid)
    o = o[:, :t].reshape(bsz * t, GLA_HV)
    y = _mm_res_ln(x2, o, w_out.astype(BF16), ln_g.reshape(1, d), ln_b.reshape(1, d), alpha=alpha)
    return y.reshape(bsz, t, d), jnp.swapaxes(st, 2, 3)


def _dsa_weight(w_in):
    hq = DSA_HEADS * DSA_HEAD_DIM
    iq0 = hq + 2 * DSA_HKV
    ik0 = iq0 + IDX_HEADS * IDX_DIM
    w_iq = w_in[:, iq0:ik0].reshape(-1, IDX_HEADS, IDX_DIM)
    w_iq = jnp.pad(w_iq, ((0, 0), (0, 0), (0, LANE - IDX_DIM))).reshape(-1, IDX_HEADS * LANE)
    w_v = w_in[:, hq + DSA_HKV:iq0].reshape(-1, DSA_KV_HEADS, DSA_HEAD_DIM)
    w_v1 = jnp.pad(w_v, ((0, 0), (0, 0), (0, DSA_HEAD_DIM))).reshape(-1, 2 * DSA_HKV)
    w = jnp.concatenate([w_in[:, :hq], w_iq, w_in[:, hq:iq0], w_v1, _pad_cols(w_in[:, ik0:], LANE)], axis=1)
    return w.astype(BF16)


def _dsa_project(x, w_in):
    bsz, t, d = x.shape
    hf, hb = _proj(x.reshape(bsz * t, d), _dsa_weight(w_in), tn=DSA_COLS // 5, ones_at=DSA_ONES)
    hf = hf.reshape(bsz, t, DSA_COLS)
    k = hf[:, :, DSA_K0:DSA_V0].reshape(bsz, t, DSA_KV_HEADS, DSA_HEAD_DIM)
    v = hf[:, :, DSA_V0:DSA_V1_0].reshape(bsz, t, DSA_KV_HEADS, DSA_HEAD_DIM)
    ik = hf[:, :, DSA_IKW0:DSA_IKW0 + IDX_DIM]
    return hf, hb.reshape(bsz, t, DSA_COLS), k, v, ik


def kernel(x_prompt, x_sample, state_gla, cache_k, cache_v, cache_idx_k, page_table, p_prompt, p_sample,
           ln_mix_g, ln_mix_b, ln_ffn_g, ln_ffn_b, w_in_gla, w_alpha_up, b_alpha, gla_norm_g, w_out_gla,
           w_in_dsa, w_out_dsa, w_ffn_gate, w_ffn_up, w_ffn_down, w_router, w_exp_gate, w_exp_up,
           w_exp_down, w_ple_proj, w_ple_gate):
    depth = ln_mix_g.shape[0]
    alpha = (2.0 * depth) ** 0.25
    d = x_prompt.shape[-1]
    xp, xs = x_prompt, x_sample
    gla_p, gla_s, kp, vp, ikp, ksm, vsm, iksm = [], [], [], [], [], [], [], []
    n_pages = page_table.shape[1]
    pp = 16 if n_pages % 16 == 0 else n_pages
    for i in range(depth):
        j = i // 2
        lg, lb = ln_mix_g[i], ln_mix_b[i]
        if i % 2 == 0:
            gw = (w_in_gla[j], w_alpha_up[j], b_alpha[j], gla_norm_g[j], w_out_gla[j], lg, lb)
            s0 = jnp.zeros((xp.shape[0], GLA_HEADS, GLA_DK, GLA_DV), F32)
            xp, sp = _gla_layer(xp, s0, *gw, alpha=alpha)
            xs, ss = _gla_layer(xs, state_gla[j], *gw, alpha=alpha)
            gla_p.append(sp)
            gla_s.append(ss)
        else:
            w_out = w_out_dsa[j].astype(BF16)
            hf, hb, k_p, v_p, ik_p = _dsa_project(xp, w_in_dsa[j])
            op = _dsa_prompt(hf, hb)
            bp, tp, _ = xp.shape
            xp = _mm_res_ln(xp.reshape(bp * tp, d), op.reshape(bp * tp, -1), w_out, lg.reshape(1, d),
                            lb.reshape(1, d), alpha=alpha).reshape(bp, tp, d)
            hf, hb, k_s, v_s, ik_s = _dsa_project(xs, w_in_dsa[j])
            sc = _dsa_sample_scores(hf, jnp.swapaxes(cache_idx_k[j], 1, 2), page_table, pp=pp)
            sel = _dsa_sample_select(sc, hf)
            n_pool = cache_k.shape[1]
            rows = PAGE_SIZE * DSA_KV_HEADS
            os_ = _dsa_sample_attend(hf, sel, cache_k[j].reshape(n_pool, rows, DSA_HEAD_DIM),
                                     cache_v[j].reshape(n_pool, rows, DSA_HEAD_DIM), page_table, pp=pp)
            bs, ts, _ = xs.shape
            xs = _mm_res_ln(xs.reshape(bs * ts, d), os_.reshape(bs * ts, -1), w_out, lg.reshape(1, d),
                            lb.reshape(1, d), alpha=alpha).reshape(bs, ts, d)
            kp.append(k_p)
            vp.append(v_p)
            ikp.append(ik_p)
            ksm.append(k_s)
            vsm.append(v_s)
            iksm.append(ik_s)
        fg, fb = ln_ffn_g[i].reshape(1, d), ln_ffn_b[i].reshape(1, d)
        outs = []
        for x, p in ((xp, p_prompt[i]), (xs, p_sample[i])):
            bsz, t, _ = x.shape
            x2 = x.reshape(bsz * t, d)
            if i % 2 == 0:
                y2 = _ffn_ln(x2, w_ffn_gate[j].astype(BF16), w_ffn_up[j].astype(BF16), w_ffn_down[j].astype(BF16),
                             fg, fb, alpha=alpha)
            else:
                gates = _router(x2, w_router[j].T)
                y2 = _moe_ln(x2, gates, w_exp_gate[j].astype(BF16), w_exp_up[j].astype(BF16),
                             w_exp_down[j].astype(BF16), fg, fb, alpha=alpha)
            y2 = _ple(y2, p.reshape(bsz * t, -1), w_ple_gate[i].astype(BF16), w_ple_proj[i].astype(BF16))
            outs.append(y2.reshape(bsz, t, d))
        xp, xs = outs
    return (xp, xs, jnp.stack(gla_p), jnp.stack(gla_s), jnp.stack(kp), jnp.stack(vp), jnp.stack(ikp),
            jnp.stack(ksm), jnp.stack(vsm), jnp.stack(iksm))
```

```python
import functools

import jax
import jax.numpy as jnp
from jax import lax
from jax.experimental import pallas as pl
from jax.experimental.pallas import tpu as pltpu

F32 = jnp.float32
BF16 = jnp.bfloat16
I32 = jnp.int32

D_MODEL = 1024
LN_EPS = 1e-5
LANE = 128

GLA_HEADS = 4
GLA_DK = 128
GLA_DV = 256
GLA_GATE_RANK = 16
GLA_TAU = 16.0
GLA_CHUNK = 64
GLA_HK = GLA_HEADS * GLA_DK
GLA_HV = GLA_HEADS * GLA_DV
GLA_COLS = 2 * GLA_HK + 2 * GLA_HV + LANE

DSA_HEADS = 8
DSA_HEAD_DIM = 128
DSA_KV_HEADS = 2
DSA_GROUP = DSA_HEADS // DSA_KV_HEADS
DSA_HKV = DSA_KV_HEADS * DSA_HEAD_DIM
IDX_HEADS = 8
IDX_DIM = 64
TOPK_MAX = 256
PAGE_SIZE = 128
DSA_Q0 = 0
DSA_IQ0 = DSA_HEADS * DSA_HEAD_DIM
DSA_K0 = DSA_IQ0 + IDX_HEADS * LANE
DSA_V0 = DSA_K0 + DSA_HKV
DSA_V1_0 = DSA_V0 + DSA_HKV
DSA_IKW0 = DSA_V1_0 + 2 * DSA_HKV
DSA_COLS = DSA_IKW0 + LANE
DSA_ONES = tuple(DSA_V1_0 + (2 * c + 1) * DSA_HEAD_DIM for c in range(DSA_KV_HEADS))

N_EXPERTS = 8

NEG_INF_KEY = -2139095041
INT_MIN = -2147483648
MASKED = -1e30


def _cparams(sem, vmem_mb):
    return pltpu.CompilerParams(dimension_semantics=sem, vmem_limit_bytes=vmem_mb * 2 ** 20)


def _dot(a, b):
    return jnp.dot(a, b, preferred_element_type=F32)


def _dot_nt(a, b):
    return lax.dot_general(a, b, (((1,), (1,)), ((), ())), preferred_element_type=F32)


def _dot_tn(a, b):
    return lax.dot_general(a, b, (((0,), (0,)), ((), ())), preferred_element_type=F32)


def _sigmoid(x):
    return 1.0 / (1.0 + jnp.exp(-x))


def _layer_norm(y, g, b):
    mu = jnp.mean(y, axis=-1, keepdims=True)
    yc = y - mu
    var = jnp.mean(yc * yc, axis=-1, keepdims=True)
    return yc * lax.rsqrt(var + LN_EPS) * g + b


def _row_tile(n, pref):
    t = min(n, pref)
    assert n % t == 0
    return t


def _proj_kernel(x_ref, w_ref, o_ref, ob_ref, *, tn, ones_at):
    xb = x_ref[...].astype(BF16)
    for c in range(w_ref.shape[1] // tn):
        r = _dot(xb, w_ref[:, c * tn:(c + 1) * tn])
        o_ref[:, c * tn:(c + 1) * tn] = r
        if ob_ref is not None:
            ob_ref[:, c * tn:(c + 1) * tn] = r.astype(BF16)
    for c0 in ones_at:
        ob_ref[:, c0:c0 + LANE] = jnp.ones((x_ref.shape[0], LANE), BF16)


def _proj_f32_kernel(x_ref, w_ref, o_ref, *, tn):
    _proj_kernel(x_ref, w_ref, o_ref, None, tn=tn, ones_at=())


def _proj(x, w, *, tn, ones_at=(), bf16_copy=True):
    n, d = x.shape
    cols = w.shape[1]
    assert cols % tn == 0 and (bf16_copy or not ones_at)
    tm = _row_tile(n, 512)
    out_spec = pl.BlockSpec((tm, cols), lambda i: (i, 0))
    if bf16_copy:
        body = functools.partial(_proj_kernel, tn=tn, ones_at=ones_at)
        out_specs = [out_spec, out_spec]
        out_shape = [jax.ShapeDtypeStruct((n, cols), F32), jax.ShapeDtypeStruct((n, cols), BF16)]
    else:
        body = functools.partial(_proj_f32_kernel, tn=tn)
        out_specs, out_shape = out_spec, jax.ShapeDtypeStruct((n, cols), F32)
    return pl.pallas_call(
        body,
        grid=(n // tm,),
        in_specs=[pl.BlockSpec((tm, d), lambda i: (i, 0)), pl.BlockSpec((d, cols), lambda i: (0, 0))],
        out_specs=out_specs,
        out_shape=out_shape,
        compiler_params=_cparams(("arbitrary",), 56),
        name="proj",
    )(x, w)


def _mm_res_ln_kernel(x_ref, a_ref, w_ref, g_ref, b_ref, o_ref, *, alpha):
    m = _dot(a_ref[...].astype(BF16), w_ref[...])
    o_ref[...] = _layer_norm(alpha * x_ref[...] + m, g_ref[...], b_ref[...])


def _mm_res_ln(x, a, w, g, b, *, alpha):
    n, d = x.shape
    ka = a.shape[1]
    tm = _row_tile(n, 512)
    row = lambda i: (i, 0)
    fixed = lambda i: (0, 0)
    return pl.pallas_call(
        functools.partial(_mm_res_ln_kernel, alpha=alpha),
        grid=(n // tm,),
        in_specs=[pl.BlockSpec((tm, d), row), pl.BlockSpec((tm, ka), row), pl.BlockSpec((ka, d), fixed),
                  pl.BlockSpec((1, d), fixed), pl.BlockSpec((1, d), fixed)],
        out_specs=pl.BlockSpec((tm, d), row),
        out_shape=jax.ShapeDtypeStruct((n, d), F32),
        compiler_params=_cparams(("arbitrary",), 40),
        name="mm_res_ln",
    )(x, a, w, g, b)


def _ffn_ln_kernel(x_ref, wg_ref, wu_ref, wd_ref, g_ref, b_ref, o_ref, xb_ref, acc_ref, *, alpha):
    f = pl.program_id(1)

    @pl.when(f == 0)
    def _():
        xb_ref[...] = x_ref[...].astype(BF16)
        acc_ref[...] = jnp.zeros_like(acc_ref)

    xb = xb_ref[...]
    hg = _dot(xb, wg_ref[...])
    hu = _dot(xb, wu_ref[...])
    h = (hg * _sigmoid(hg) * hu).astype(BF16)
    acc_ref[...] += _dot(h, wd_ref[...])

    @pl.when(f == pl.num_programs(1) - 1)
    def _():
        o_ref[...] = _layer_norm(alpha * x_ref[...] + acc_ref[...], g_ref[...], b_ref[...])


def _ffn_ln(x, wg, wu, wd, g, b, *, alpha, tf=512):
    n, d = x.shape
    ff = wg.shape[1]
    assert ff % tf == 0
    tm = _row_tile(n, 1024)
    return pl.pallas_call(
        functools.partial(_ffn_ln_kernel, alpha=alpha),
        grid=(n // tm, ff // tf),
        in_specs=[pl.BlockSpec((tm, d), lambda i, f: (i, 0)),
                  pl.BlockSpec((d, tf), lambda i, f: (0, f)),
                  pl.BlockSpec((d, tf), lambda i, f: (0, f)),
                  pl.BlockSpec((tf, d), lambda i, f: (f, 0)),
                  pl.BlockSpec((1, d), lambda i, f: (0, 0)),
                  pl.BlockSpec((1, d), lambda i, f: (0, 0))],
        out_specs=pl.BlockSpec((tm, d), lambda i, f: (i, 0)),
        out_shape=jax.ShapeDtypeStruct((n, d), F32),
        scratch_shapes=[pltpu.VMEM((tm, d), BF16), pltpu.VMEM((tm, d), F32)],
        compiler_params=_cparams(("arbitrary", "arbitrary"), 48),
        name="ffn_ln",
    )(x, wg, wu, wd, g, b)


def _ple_kernel(x_ref, p_ref, wg_ref, wp_ref, o_ref):
    x = x_ref[...]
    gate = _sigmoid(_dot(x.astype(BF16), wg_ref[...]))
    o_ref[...] = x + gate * _dot(p_ref[...].astype(BF16), wp_ref[...])


def _ple(x, p, wg, wp):
    n, d = x.shape
    pd = p.shape[1]
    tm = _row_tile(n, 512)
    row = lambda i: (i, 0)
    fixed = lambda i: (0, 0)
    return pl.pallas_call(
        _ple_kernel,
        grid=(n // tm,),
        in_specs=[pl.BlockSpec((tm, d), row), pl.BlockSpec((tm, pd), row),
                  pl.BlockSpec((d, d), fixed), pl.BlockSpec((pd, d), fixed)],
        out_specs=pl.BlockSpec((tm, d), row),
        out_shape=jax.ShapeDtypeStruct((n, d), F32),
        compiler_params=_cparams(("arbitrary",), 40),
        name="ple",
    )(x, p, wg, wp)


def _gla_kernel(q_ref, k_ref, v_ref, g_ref, a_ref, wup_ref, ba_ref, ng_ref, s0_ref, o_ref, st_out_ref, st_ref,
                *, c, valid, nbs):
    t = pl.program_id(1)

    @pl.when(t == 0)
    def _():
        st_ref[...] = s0_ref[...]

    row = lax.broadcasted_iota(I32, (c, c), 0)
    col = lax.broadcasted_iota(I32, (c, c), 1)
    causal = row >= col
    for bi in range(nbs):
        logit = _dot(a_ref[bi].astype(BF16), wup_ref[...]) + ba_ref[...]
        log_a = (jnp.minimum(logit, 0.0) - jnp.log(1.0 + jnp.exp(-jnp.abs(logit)))) * (1.0 / GLA_TAU)
        if valid < c:
            log_a = jnp.where(lax.broadcasted_iota(I32, log_a.shape, 0) < valid, log_a, 0.0)
        bcum = jnp.dot(causal.astype(F32), log_a, preferred_element_type=F32, precision=lax.Precision.HIGHEST)
        blast = bcum[c - 1:c, :]
        q = q_ref[bi] * (GLA_DK ** -0.5)
        k = k_ref[bi]
        q_dec = (q * jnp.exp(bcum)).astype(BF16)
        k_inv = (k * jnp.exp(-bcum)).astype(BF16)
        k_dec = (k * jnp.exp(blast - bcum)).astype(BF16)
        decay = jnp.exp(blast)
        v = v_ref[bi].astype(BF16)
        for h in range(GLA_HEADS):
            sk = slice(h * GLA_DK, (h + 1) * GLA_DK)
            sv = slice(h * GLA_DV, (h + 1) * GLA_DV)
            att = jnp.where(causal, _dot_nt(q_dec[:, sk], k_inv[:, sk]), 0.0).astype(BF16)
            st = st_ref[bi, h]
            o = _dot(att, v[:, sv]) + _dot_nt(q_dec[:, sk], st.astype(BF16))
            st_ref[bi, h] = st * decay[:, sk] + _dot_tn(v[:, sv], k_dec[:, sk])
            mu = jnp.mean(o, axis=-1, keepdims=True)
            oc = o - mu
            var = jnp.mean(oc * oc, axis=-1, keepdims=True)
            on = oc * lax.rsqrt(var + LN_EPS) * ng_ref[:, sv]
            gg = g_ref[bi, :, sv]
            o_ref[bi, :, sv] = (on * (gg * _sigmoid(gg))).astype(BF16)

    @pl.when(t == pl.num_programs(1) - 1)
    def _():
        st_out_ref[...] = st_ref[...]


def _gla_core(h, s0_t, wup, ba, ng, *, c, valid):
    bsz, tlen, _ = h.shape
    assert tlen % c == 0
    nbs = 4 if bsz % 4 == 0 else 1
    v_blk = (2 * GLA_HK) // GLA_HV
    state_spec = pl.BlockSpec((nbs, GLA_HEADS, GLA_DV, GLA_DK), lambda b, t: (b, 0, 0, 0))
    return pl.pallas_call(
        functools.partial(_gla_kernel, c=c, valid=valid, nbs=nbs),
        grid=(bsz // nbs, tlen // c),
        in_specs=[pl.BlockSpec((nbs, c, GLA_HK), lambda b, t: (b, t, 0)),
                  pl.BlockSpec((nbs, c, GLA_HK), lambda b, t: (b, t, 1)),
                  pl.BlockSpec((nbs, c, GLA_HV), lambda b, t: (b, t, v_blk)),
                  pl.BlockSpec((nbs, c, GLA_HV), lambda b, t: (b, t, v_blk + 1)),
                  pl.BlockSpec((nbs, c, LANE), lambda b, t: (b, t, (2 * GLA_HK + 2 * GLA_HV) // LANE)),
                  pl.BlockSpec((LANE, GLA_HK), lambda b, t: (0, 0)),
                  pl.BlockSpec((1, GLA_HK), lambda b, t: (0, 0)),
                  pl.BlockSpec((1, GLA_HV), lambda b, t: (0, 0)),
                  state_spec],
        out_specs=[pl.BlockSpec((nbs, c, GLA_HV), lambda b, t: (b, t, 0)), state_spec],
        out_shape=[jax.ShapeDtypeStruct((bsz, tlen, GLA_HV), BF16),
                   jax.ShapeDtypeStruct((bsz, GLA_HEADS, GLA_DV, GLA_DK), F32)],
        scratch_shapes=[pltpu.VMEM((nbs, GLA_HEADS, GLA_DV, GLA_DK), F32)],
        compiler_params=_cparams(("arbitrary", "arbitrary"), 32),
        name="gla_core",
    )(h, h, h, h, h, wup, ba, ng, s0_t)


def _sortable(s):
    bits = lax.bitcast_convert_type(s, I32)
    return bits ^ (lax.shift_right_arithmetic(bits, 31) & 0x7FFFFFFF)


def _lane_fold(m):
    acc = m[:, 0:LANE]
    for i in range(1, m.shape[1] // LANE):
        acc = acc + m[:, i * LANE:(i + 1) * LANE]
    return acc


def _lane_fold_max(m):
    acc = m[:, 0:LANE]
    for i in range(1, m.shape[1] // LANE):
        acc = jnp.maximum(acc, m[:, i * LANE:(i + 1) * LANE])
    return acc


def _bias_bits(sel):
    return lax.bitcast_convert_type(jnp.where(sel, 0.0, MASKED).astype(F32), I32)


def _kth_largest_key(count_ge, rows, k):
    kf = float(k)
    thr0 = jnp.where(count_ge(jnp.zeros((rows, 1), I32)) >= kf, 0, INT_MIN).astype(I32)

    def bit_body(i, thr):
        cand = thr | lax.shift_left(jnp.int32(1), 30 - i)
        return jnp.where(count_ge(cand) >= kf, cand, thr)

    return lax.fori_loop(0, 31, bit_body, thr0)


def _strict_upper(n):
    return (lax.broadcasted_iota(I32, (n, n), 0) < lax.broadcasted_iota(I32, (n, n), 1)).astype(BF16)


def _dsa_prompt_kernel(q_ref, iq_ref, iw_ref, ikw_ref, k_ref, v1_ref, o_ref,
                       keys_ref, gmax_ref, iqs_ref, qg_ref, d_ref, m_ref, acc_ref, *, tq, tk, topk):
    j = pl.program_id(1)
    nkt = ((j + 1) * tq + tk - 1) // tk
    gq = DSA_GROUP * tq
    rp = 16
    for h in range(IDX_HEADS):
        iqs_ref[h * tq:(h + 1) * tq, :] = iq_ref[0, :, h * LANE:(h + 1) * LANE]
    for c in range(DSA_KV_HEADS):
        for g in range(DSA_GROUP):
            hh = c * DSA_GROUP + g
            qg_ref[c, g * tq:(g + 1) * tq, :] = q_ref[0, :, hh * DSA_HEAD_DIM:(hh + 1) * DSA_HEAD_DIM]
    ws = iw_ref[0][:, IDX_DIM:IDX_DIM + IDX_HEADS] * (IDX_HEADS ** -0.5) * (IDX_DIM ** -0.5)

    def score_tile(kt, diagonal):
        ikt = ikw_ref[0, pl.ds(pl.multiple_of(kt * tk, tk), tk), :]
        d_ref[...] = _dot_nt(iqs_ref[...], ikt)
        for r in range(tq // rp):
            s = jnp.zeros((rp, tk), F32)
            for h in range(IDX_HEADS):
                dh = d_ref[h * tq + r * rp:h * tq + (r + 1) * rp, :]
                s = s + ws[r * rp:(r + 1) * rp, h:h + 1] * jnp.maximum(dh, 0.0)
            key = _sortable(s + 0.0)
            if diagonal:
                qpos = j * tq + r * rp + lax.broadcasted_iota(I32, (rp, tk), 0)
                kpos = kt * tk + lax.broadcasted_iota(I32, (rp, tk), 1)
                key = jnp.where(kpos <= qpos, key, INT_MIN)
            keys_ref[kt, r * rp:(r + 1) * rp, :] = key
            even, odd = key[:, 0:LANE], key[:, LANE:2 * LANE]
            for lt in range(2, tk // LANE, 2):
                even = jnp.maximum(even, key[:, lt * LANE:(lt + 1) * LANE])
                odd = jnp.maximum(odd, key[:, (lt + 1) * LANE:(lt + 2) * LANE])
            rows = slice(r * rp, (r + 1) * rp)
            gmax_ref[rows, :] = jnp.maximum(gmax_ref[rows, :], jnp.concatenate([even, odd], axis=1))

    def score_body(kt, carry):
        score_tile(kt, False)
        return carry

    gmax_ref[...] = jnp.full(gmax_ref.shape, INT_MIN, I32)
    lax.fori_loop(0, nkt - 1, score_body, 0)
    score_tile(nkt - 1, True)

    def counts(pred):
        def body(kt, acc):
            return acc + _lane_fold(jnp.where(pred(keys_ref[kt]), 1.0, 0.0))
        return jnp.sum(lax.fori_loop(0, nkt, body, jnp.zeros((tq, LANE), F32)), axis=1, keepdims=True)

    kf = float(topk)
    gmax = gmax_ref[...]
    lo_key = jnp.min(gmax, axis=1, keepdims=True)
    hi_key = jnp.max(gmax, axis=1, keepdims=True)
    shared = jnp.min(lax.clz(lo_key ^ hi_key))
    nbits = 32 - (shared // 4) * 4
    low_mask = jnp.where(nbits >= 32, -1, lax.shift_left(jnp.int32(1), jnp.minimum(nbits, 31)) - 1)
    thr0 = (lo_key ^ INT_MIN) & ~low_mask

    def search_cond(state):
        g, _, _, settled = state
        return jnp.logical_and(g * 4 < nbits, jnp.logical_not(settled))

    def search_body(state):
        g, thr_u, c_thr, _ = state
        for i in range(4):
            cand_u = thr_u | lax.shift_left(jnp.int32(1), nbits - 1 - (g * 4 + i))
            cand = cand_u ^ INT_MIN
            c = counts(lambda kk: kk >= cand)
            ok = c >= kf
            thr_u = jnp.where(ok, cand_u, thr_u)
            c_thr = jnp.where(ok, c, c_thr)
        settled = jnp.min(jnp.where(c_thr == kf, 1.0, 0.0)) > 0.0
        return g + 1, thr_u, c_thr, settled

    _, thr_u, c_thr, settled = lax.while_loop(
        search_cond, search_body, (jnp.int32(0), thr0, jnp.full((tq, 1), 3e38, F32), jnp.bool_(False)))
    thr = thr_u ^ INT_MIN

    @pl.when(settled)
    def _():
        def body(kt, carry):
            keys_ref[kt] = _bias_bits(keys_ref[kt] >= thr)
            return carry
        lax.fori_loop(0, nkt, body, 0)

    @pl.when(jnp.logical_not(settled))
    def _():
        n_gt = counts(lambda kk: kk > thr)
        n_eq = counts(lambda kk: kk == thr)
        need = kf - n_gt
        tie_row = jnp.logical_and(n_eq > need, thr > INT_MIN)
        any_tie = jnp.max(jnp.where(tie_row, 1.0, 0.0)) > 0.0

        @pl.when(jnp.logical_not(any_tie))
        def _():
            thr_eff = jnp.maximum(thr, INT_MIN + 1)

            def body(kt, carry):
                keys_ref[kt] = _bias_bits(keys_ref[kt] >= thr_eff)
                return carry
            lax.fori_loop(0, nkt, body, 0)

        @pl.when(any_tie)
        def _():
            upper = _strict_upper(tk)

            def body(kt, seen):
                kk = keys_ref[kt]
                eq = kk == thr
                before = seen + _dot(jnp.where(eq, 1.0, 0.0).astype(BF16), upper)
                sel = jnp.logical_and(kk > INT_MIN, jnp.logical_or(kk > thr, jnp.logical_and(eq, before < need)))
                keys_ref[kt] = _bias_bits(sel)
                return seen + jnp.sum(jnp.where(eq, 1.0, 0.0), axis=1, keepdims=True)
            lax.fori_loop(0, nkt, body, jnp.zeros((tq, 1), F32))

    m_ref[...] = jnp.full(m_ref.shape, MASKED, F32)
    acc_ref[...] = jnp.zeros_like(acc_ref)
    c2 = (DSA_HEAD_DIM ** -0.5) * 1.4426950408889634
    dh = DSA_HEAD_DIM

    def attend_body(kt, carry):
        rows = pl.ds(pl.multiple_of(kt * tk, tk), tk)
        bias = lax.bitcast_convert_type(keys_ref[kt], F32)[None]
        for c in range(DSA_KV_HEADS):
            kc = k_ref[0, rows, c * dh:(c + 1) * dh]
            v1 = v1_ref[0, rows, c * 2 * dh:(c + 1) * 2 * dh]
            x = (_dot_nt(qg_ref[c], kc).reshape(DSA_GROUP, tq, tk) + bias).reshape(gq, tk)
            m_old = m_ref[c]
            m_new = jnp.maximum(m_old, jnp.max(x, axis=1, keepdims=True))
            p = jnp.exp2((x - jnp.concatenate([m_new] * (tk // LANE), axis=1)) * c2).astype(BF16)
            alpha = jnp.exp2((m_old - m_new) * c2)
            acc_ref[c] = jnp.concatenate([alpha, alpha], axis=1) * acc_ref[c] + _dot(p, v1)
            m_ref[c] = m_new
        return carry

    lax.fori_loop(0, nkt, attend_body, 0)
    for c in range(DSA_KV_HEADS):
        oc = acc_ref[c, :, 0:dh] / acc_ref[c, :, dh:2 * dh]
        for g in range(DSA_GROUP):
            hh = c * DSA_GROUP + g
            o_ref[0, :, hh * dh:(hh + 1) * dh] = oc[g * tq:(g + 1) * tq].astype(BF16)


def _dsa_prompt(hf, hb, *, tq=128, tk=1024):
    bsz, s, _ = hf.shape
    tk = min(tk, s)
    assert s % tq == 0 and s % tk == 0 and tk % tq == 0
    topk = min(TOPK_MAX, s // 4)
    hq = DSA_HEADS * DSA_HEAD_DIM
    gq = DSA_GROUP * tq
    return pl.pallas_call(
        functools.partial(_dsa_prompt_kernel, tq=tq, tk=tk, topk=topk),
        grid=(bsz, s // tq),
        in_specs=[pl.BlockSpec((1, tq, hq), lambda b, j: (b, j, DSA_Q0 // hq)),
                  pl.BlockSpec((1, tq, IDX_HEADS * LANE), lambda b, j: (b, j, DSA_IQ0 // (IDX_HEADS * LANE))),
                  pl.BlockSpec((1, tq, LANE), lambda b, j: (b, j, DSA_IKW0 // LANE)),
                  pl.BlockSpec((1, s, LANE), lambda b, j: (b, 0, DSA_IKW0 // LANE)),
                  pl.BlockSpec((1, s, DSA_HKV), lambda b, j: (b, 0, DSA_K0 // DSA_HKV)),
                  pl.BlockSpec((1, s, 2 * DSA_HKV), lambda b, j: (b, 0, DSA_V1_0 // (2 * DSA_HKV)))],
        out_specs=pl.BlockSpec((1, tq, hq), lambda b, j: (b, j, 0)),
        out_shape=jax.ShapeDtypeStruct((bsz, s, hq), BF16),
        scratch_shapes=[pltpu.VMEM((s // tk, tq, tk), I32),
                        pltpu.VMEM((tq, 2 * LANE), I32),
                        pltpu.VMEM((IDX_HEADS * tq, LANE), BF16),
                        pltpu.VMEM((DSA_KV_HEADS, gq, DSA_HEAD_DIM), BF16),
                        pltpu.VMEM((IDX_HEADS * tq, tk), F32),
                        pltpu.VMEM((DSA_KV_HEADS, gq, LANE), F32),
                        pltpu.VMEM((DSA_KV_HEADS, gq, 2 * DSA_HEAD_DIM), F32)],
        compiler_params=_cparams(("arbitrary", "arbitrary"), 56),
        name="dsa_prompt",
    )(hb, hb, hf, hb, hb, hb)


def _sample_ws(iw_ref):
    return iw_ref[0][:, IDX_DIM:IDX_DIM + IDX_HEADS] * (IDX_HEADS ** -0.5)


def _sample_scores(d, ws, t):
    s = jnp.zeros((t, d.shape[1]), F32)
    for h in range(IDX_HEADS):
        s = s + ws[:, h:h + 1] * jnp.maximum(d[h * t:(h + 1) * t] * (IDX_DIM ** -0.5), 0.0)
    return s + 0.0


def _dsa_sample_scores_kernel(pt_ref, iq_ref, iw_ref, *refs, pp, t):
    page_refs, o_ref = refs[:pp], refs[pp]
    iq_stack = jnp.concatenate([iq_ref[0, :, h * LANE:h * LANE + IDX_DIM] for h in range(IDX_HEADS)], axis=0)
    ik_t = jnp.concatenate([r[0].astype(BF16) for r in page_refs], axis=1)
    o_ref[0] = _sample_scores(_dot(iq_stack.astype(BF16), ik_t), _sample_ws(iw_ref), t)


def _dsa_sample_scores(hf, cache_ik_t, page_table, *, pp):
    bsz, t, _ = hf.shape
    n_pages = page_table.shape[1]
    assert n_pages % pp == 0
    iq_w = IDX_HEADS * LANE

    def page_spec(i):
        return pl.BlockSpec((1, IDX_DIM, PAGE_SIZE), lambda b, j, pt: (pt[b, j * pp + i], 0, 0))

    grid_spec = pltpu.PrefetchScalarGridSpec(
        num_scalar_prefetch=1, grid=(bsz, n_pages // pp),
        in_specs=[pl.BlockSpec((1, t, iq_w), lambda b, j, pt: (b, 0, DSA_IQ0 // iq_w)),
                  pl.BlockSpec((1, t, LANE), lambda b, j, pt: (b, 0, DSA_IKW0 // LANE))]
                 + [page_spec(i) for i in range(pp)],
        out_specs=pl.BlockSpec((1, t, pp * PAGE_SIZE), lambda b, j, pt: (b, 0, j)))
    return pl.pallas_call(
        functools.partial(_dsa_sample_scores_kernel, pp=pp, t=t),
        grid_spec=grid_spec,
        out_shape=jax.ShapeDtypeStruct((bsz, t, n_pages * PAGE_SIZE), F32),
        compiler_params=_cparams(("arbitrary", "arbitrary"), 32),
        name="dsa_sample_scores",
    )(page_table, hf, hf, *([cache_ik_t] * pp))


def _dsa_sample_select_kernel(sp_ref, iq_ref, iw_ref, ikw_ref, o_ref, keys_ref, *, t, past, new_w, topk, ck):
    iq_stack = jnp.concatenate([iq_ref[0, :, h * LANE:(h + 1) * LANE] for h in range(IDX_HEADS)], axis=0)
    ik_new = jnp.concatenate([ikw_ref[0], jnp.zeros((new_w - t, LANE), F32)], axis=0)
    d_new = _dot_nt(iq_stack.astype(BF16), ik_new.astype(BF16))
    s_new = _sample_scores(d_new, _sample_ws(iw_ref), t)
    row = lax.broadcasted_iota(I32, (t, new_w), 0)
    lane = lax.broadcasted_iota(I32, (t, new_w), 1)
    s_new = jnp.where(lane <= row, s_new, -jnp.inf)
    keys_ref[:, 0:past] = _sortable(sp_ref[0])
    keys_ref[:, past:past + new_w] = _sortable(s_new)
    n_ck = (past + new_w) // ck

    def counts(pred):
        acc = jnp.zeros((t, LANE), F32)
        for i in range(n_ck):
            acc = acc + _lane_fold(jnp.where(pred(keys_ref[:, i * ck:(i + 1) * ck]), 1.0, 0.0))
        return jnp.sum(acc, axis=1, keepdims=True)

    thr = _kth_largest_key(lambda cand: counts(lambda kk: kk >= cand), t, topk)
    n_gt = counts(lambda kk: kk > thr)
    n_eq = counts(lambda kk: kk == thr)
    need = float(topk) - n_gt
    tie_row = jnp.logical_and(n_eq > need, thr > NEG_INF_KEY)
    any_tie = jnp.max(jnp.where(tie_row, 1.0, 0.0)) > 0.0

    @pl.when(jnp.logical_not(any_tie))
    def _():
        thr_eff = jnp.maximum(thr, NEG_INF_KEY + 1)
        o_ref[0] = jnp.where(keys_ref[...] >= thr_eff, 1.0, 0.0)

    @pl.when(any_tie)
    def _():
        upper = _strict_upper(ck)
        seen = jnp.zeros((t, 1), F32)
        for i in range(n_ck):
            kk = keys_ref[:, i * ck:(i + 1) * ck]
            eq = kk == thr
            before = seen + _dot(jnp.where(eq, 1.0, 0.0).astype(BF16), upper)
            sel = jnp.logical_and(kk > NEG_INF_KEY, jnp.logical_or(kk > thr, jnp.logical_and(eq, before < need)))
            o_ref[0, :, i * ck:(i + 1) * ck] = jnp.where(sel, 1.0, 0.0)
            seen = seen + jnp.sum(jnp.where(eq, 1.0, 0.0), axis=1, keepdims=True)


def _dsa_sample_select(scores_past, hf, *, new_w=512, ck=512):
    bsz, t, past = scores_past.shape
    assert past % ck == 0 and new_w % ck == 0
    topk = min(TOPK_MAX, (past + t) // 4)
    iq_w = IDX_HEADS * LANE
    total = past + new_w
    return pl.pallas_call(
        functools.partial(_dsa_sample_select_kernel, t=t, past=past, new_w=new_w, topk=topk, ck=ck),
        grid=(bsz,),
        in_specs=[pl.BlockSpec((1, t, past), lambda b: (b, 0, 0)),
                  pl.BlockSpec((1, t, iq_w), lambda b: (b, 0, DSA_IQ0 // iq_w)),
                  pl.BlockSpec((1, t, LANE), lambda b: (b, 0, DSA_IKW0 // LANE)),
                  pl.BlockSpec((1, t, LANE), lambda b: (b, 0, DSA_IKW0 // LANE))],
        out_specs=pl.BlockSpec((1, t, total), lambda b: (b, 0, 0)),
        out_shape=jax.ShapeDtypeStruct((bsz, t, total), F32),
        scratch_shapes=[pltpu.VMEM((t, total), I32)],
        compiler_params=_cparams(("arbitrary",), 32),
        name="dsa_sample_select",
    )(scores_past, hf, hf, hf)


def _dsa_sample_attend_kernel(pt_ref, q_ref, kn_ref, vn_ref, selp_ref, seln_ref, *refs, pp, t, new_w):
    k_pages, v_pages = refs[:pp], refs[pp:2 * pp]
    o_ref, m_ref, l_ref, acc_ref = refs[2 * pp:]
    j = pl.program_id(1)
    gt = DSA_GROUP * t

    @pl.when(j == 0)
    def _():
        m_ref[...] = jnp.full(m_ref.shape, MASKED, F32)
        l_ref[...] = jnp.zeros_like(l_ref)
        acc_ref[...] = jnp.zeros_like(acc_ref)

    q_groups = [jnp.concatenate([q_ref[0, :, (c * DSA_GROUP + g) * DSA_HEAD_DIM:(c * DSA_GROUP + g + 1) * DSA_HEAD_DIM]
                                 for g in range(DSA_GROUP)], axis=0).astype(BF16) for c in range(DSA_KV_HEADS)]

    def attend(kcs, vcs, sel):
        n = kcs[0].shape[0]
        for c in range(DSA_KV_HEADS):
            kc, vc = kcs[c], vcs[c]
            logits = (_dot_nt(q_groups[c], kc) * (DSA_HEAD_DIM ** -0.5)).reshape(DSA_GROUP, t, n)
            masked = jnp.where(sel[None], logits, MASKED).reshape(gt, n)
            m_old = m_ref[c]
            m_new = jnp.maximum(m_old, jnp.max(masked, axis=1, keepdims=True))
            p = jnp.where(sel[None], jnp.exp(masked - m_new).reshape(DSA_GROUP, t, n), 0.0).reshape(gt, n)
            alpha = jnp.exp(m_old - m_new)
            l_ref[c] = alpha * l_ref[c] + jnp.sum(p, axis=1, keepdims=True)
            acc_ref[c] = alpha * acc_ref[c] + _dot(p.astype(BF16), vc)
            m_ref[c] = m_new

    def head_rows(pages, c):
        return jnp.concatenate([r[0, pl.ds(c, PAGE_SIZE, stride=DSA_KV_HEADS), :].astype(BF16) for r in pages], axis=0)

    attend([head_rows(k_pages, c) for c in range(DSA_KV_HEADS)],
           [head_rows(v_pages, c) for c in range(DSA_KV_HEADS)], selp_ref[0] != 0.0)

    @pl.when(j == pl.num_programs(1) - 1)
    def _():
        pad = jnp.zeros((new_w - t, DSA_HEAD_DIM), F32)

        def new_rows(ref, c):
            return jnp.concatenate([ref[0, :, c * DSA_HEAD_DIM:(c + 1) * DSA_HEAD_DIM], pad], axis=0).astype(BF16)

        attend([new_rows(kn_ref, c) for c in range(DSA_KV_HEADS)],
               [new_rows(vn_ref, c) for c in range(DSA_KV_HEADS)], seln_ref[0] != 0.0)
        for c in range(DSA_KV_HEADS):
            oc = acc_ref[c] / l_ref[c]
            for g in range(DSA_GROUP):
                hh = c * DSA_GROUP + g
                o_ref[0, :, hh * DSA_HEAD_DIM:(hh + 1) * DSA_HEAD_DIM] = oc[g * t:(g + 1) * t].astype(BF16)


def _dsa_sample_attend(hf, sel, cache_k, cache_v, page_table, *, pp, new_w=512):
    bsz, t, _ = hf.shape
    n_pages = page_table.shape[1]
    past = n_pages * PAGE_SIZE
    assert n_pages % pp == 0 and past % new_w == 0
    hq = DSA_HEADS * DSA_HEAD_DIM
    gt = DSA_GROUP * t

    def page_spec(i):
        return pl.BlockSpec((1, PAGE_SIZE * DSA_KV_HEADS, DSA_HEAD_DIM), lambda b, j, pt: (pt[b, j * pp + i], 0, 0))

    grid_spec = pltpu.PrefetchScalarGridSpec(
        num_scalar_prefetch=1, grid=(bsz, n_pages // pp),
        in_specs=[pl.BlockSpec((1, t, hq), lambda b, j, pt: (b, 0, DSA_Q0 // hq)),
                  pl.BlockSpec((1, t, DSA_HKV), lambda b, j, pt: (b, 0, DSA_K0 // DSA_HKV)),
                  pl.BlockSpec((1, t, DSA_HKV), lambda b, j, pt: (b, 0, DSA_V0 // DSA_HKV)),
                  pl.BlockSpec((1, t, pp * PAGE_SIZE), lambda b, j, pt: (b, 0, j)),
                  pl.BlockSpec((1, t, new_w), lambda b, j, pt: (b, 0, past // new_w))]
                 + [page_spec(i) for i in range(pp)] * 2,
        out_specs=pl.BlockSpec((1, t, hq), lambda b, j, pt: (b, 0, 0)),
        scratch_shapes=[pltpu.VMEM((DSA_KV_HEADS, gt, 1), F32),
                        pltpu.VMEM((DSA_KV_HEADS, gt, 1), F32),
                        pltpu.VMEM((DSA_KV_HEADS, gt, DSA_HEAD_DIM), F32)])
    return pl.pallas_call(
        functools.partial(_dsa_sample_attend_kernel, pp=pp, t=t, new_w=new_w),
        grid_spec=grid_spec,
        out_shape=jax.ShapeDtypeStruct((bsz, t, hq), BF16),
        compiler_params=_cparams(("arbitrary", "arbitrary"), 48),
        name="dsa_sample_attend",
    )(page_table, hf, hf, hf, sel, sel, *([cache_k] * pp), *([cache_v] * pp))


def _router_kernel(x_ref, wt_ref, o_ref):
    x = x_ref[...]
    wt = wt_ref[...]
    xh = x.astype(BF16)
    xl = (x - xh.astype(F32)).astype(BF16)
    wh = wt.astype(BF16)
    wl = (wt - wh.astype(F32)).astype(BF16)
    logits = _dot_nt(wh, xh) + (_dot_nt(wl, xh) + _dot_nt(wh, xl))
    sub = lax.broadcasted_iota(I32, logits.shape, 0)
    l1 = jnp.max(logits, axis=0, keepdims=True)
    i1 = jnp.min(jnp.where(logits == l1, sub, N_EXPERTS), axis=0, keepdims=True)
    rest = jnp.where(sub == i1, -jnp.inf, logits)
    l2 = jnp.max(rest, axis=0, keepdims=True)
    i2 = jnp.min(jnp.where(rest == l2, sub, N_EXPERTS), axis=0, keepdims=True)
    e2 = jnp.exp(l2 - l1)
    denom = 1.0 + e2
    o_ref[...] = jnp.where(sub == i1, 1.0 / denom, jnp.where(sub == i2, e2 / denom, 0.0))


def _router(x, w_t):
    n, d = x.shape
    tm = _row_tile(n, 512)
    return pl.pallas_call(
        _router_kernel,
        grid=(n // tm,),
        in_specs=[pl.BlockSpec((tm, d), lambda i: (i, 0)), pl.BlockSpec((N_EXPERTS, d), lambda i: (0, 0))],
        out_specs=pl.BlockSpec((N_EXPERTS, tm), lambda i: (0, i)),
        out_shape=jax.ShapeDtypeStruct((N_EXPERTS, n), F32),
        compiler_params=_cparams(("arbitrary",), 32),
        name="router",
    )(x, w_t)


MOE_ROWS = 64


def _moe_ln_kernel(cnt_ref, x_ref, gt_ref, wg_ref, wu_ref, wd_ref, g_ref, b_ref, o_ref,
                   xb_ref, rank_ref, xg_ref, acc_ref, y_ref, upper_ref, *, alpha, arms):
    i, e, f = pl.program_id(0), pl.program_id(1), pl.program_id(2)
    tm = x_ref.shape[0]
    first_f = f == 0
    last_f = f == pl.num_programs(2) - 1
    nb = (cnt_ref[i * N_EXPERTS + e] + MOE_ROWS - 1) // MOE_ROWS

    @pl.when(jnp.logical_and(i == 0, jnp.logical_and(e == 0, first_f)))
    def _():
        upper_ref[...] = _strict_upper(tm)

    @pl.when(jnp.logical_and(e == 0, first_f))
    def _():
        xb_ref[...] = x_ref[...].astype(BF16)
        y_ref[...] = jnp.zeros_like(y_ref)
        chosen = jnp.where(gt_ref[...] > 0.0, 1.0, 0.0).astype(BF16)
        rank_ref[...] = _dot(chosen, upper_ref[...])

    ge = gt_ref[pl.ds(e, 1), :]
    slot = jnp.where(ge > 0.0, rank_ref[pl.ds(e, 1), :], -1.0)

    def one_hot(start, rows):
        r = (start + lax.broadcasted_iota(I32, (rows, 1), 0)).astype(F32)
        return slot == r

    def for_arms(fn):
        for size in arms:
            start = pl.multiple_of((nb & ~(2 * size - 1)) * MOE_ROWS, MOE_ROWS)

            @pl.when((nb & size) != 0)
            def _():
                fn(start, size * MOE_ROWS)

    @pl.when(first_f)
    def _():
        def gather(start, rows):
            hot = jnp.where(one_hot(start, rows), 1.0, 0.0).astype(BF16)
            xg_ref[pl.ds(start, rows), :] = _dot(hot, xb_ref[...]).astype(BF16)
            acc_ref[pl.ds(start, rows), :] = jnp.zeros((rows, acc_ref.shape[1]), F32)
        for_arms(gather)

    def expert(start, rows):
        xs = xg_ref[pl.ds(start, rows), :]
        hg = _dot(xs, wg_ref[0])
        hu = _dot(xs, wu_ref[0])
        h = (hg * _sigmoid(hg) * hu).astype(BF16)
        acc_ref[pl.ds(start, rows), :] += _dot(h, wd_ref[0])
    for_arms(expert)

    @pl.when(last_f)
    def _():
        def scatter(start, rows):
            hot = one_hot(start, rows)
            gate = jnp.sum(jnp.where(hot, ge, 0.0), axis=1, keepdims=True)
            v = (gate * acc_ref[pl.ds(start, rows), :]).astype(BF16)
            y_ref[...] += _dot_tn(jnp.where(hot, 1.0, 0.0).astype(BF16), v)
        for_arms(scatter)

    @pl.when(jnp.logical_and(e == N_EXPERTS - 1, last_f))
    def _():
        o_ref[...] = _layer_norm(alpha * x_ref[...] + y_ref[...], g_ref[...], b_ref[...])


def _moe_ln(x, gates_t, wg, wu, wd, g, b, *, alpha, tf=896):
    n, d = x.shape
    ff = wg.shape[2]
    assert ff % tf == 0
    tm = _row_tile(n, 1024)
    assert tm % MOE_ROWS == 0
    max_blocks = tm // MOE_ROWS
    arms = tuple(s for s in (64, 32, 16, 8, 4, 2, 1) if s <= max_blocks)
    assert arms[0] == max_blocks
    counts = jnp.sum((gates_t > 0.0).reshape(N_EXPERTS, n // tm, tm), axis=2).astype(I32).T.reshape(-1)
    grid_spec = pltpu.PrefetchScalarGridSpec(
        num_scalar_prefetch=1, grid=(n // tm, N_EXPERTS, ff // tf),
        in_specs=[pl.BlockSpec((tm, d), lambda i, e, f, c: (i, 0)),
                  pl.BlockSpec((N_EXPERTS, tm), lambda i, e, f, c: (0, i)),
                  pl.BlockSpec((1, d, tf), lambda i, e, f, c: (e, 0, f)),
                  pl.BlockSpec((1, d, tf), lambda i, e, f, c: (e, 0, f)),
                  pl.BlockSpec((1, tf, d), lambda i, e, f, c: (e, f, 0)),
                  pl.BlockSpec((1, d), lambda i, e, f, c: (0, 0)),
                  pl.BlockSpec((1, d), lambda i, e, f, c: (0, 0))],
        out_specs=pl.BlockSpec((tm, d), lambda i, e, f, c: (i, 0)),
        scratch_shapes=[pltpu.VMEM((tm, d), BF16), pltpu.VMEM((N_EXPERTS, tm), F32), pltpu.VMEM((tm, d), BF16),
                        pltpu.VMEM((tm, d), F32), pltpu.VMEM((tm, d), F32), pltpu.VMEM((tm, tm), BF16)])
    return pl.pallas_call(
        functools.partial(_moe_ln_kernel, alpha=alpha, arms=arms),
        grid_spec=grid_spec,
        out_shape=jax.ShapeDtypeStruct((n, d), F32),
        compiler_params=_cparams(("arbitrary", "arbitrary", "arbitrary"), 56),
        name="moe_ln",
    )(counts, x, gates_t, wg, wu, wd, g, b)


def _pad_cols(w, cols):
    return jnp.pad(w, ((0, 0), (0, cols - w.shape[1])))


def _gla_layer(x, s0, w_in, w_up, b_a, norm_g, w_out, ln_g, ln_b, *, alpha):
    bsz, t, d = x.shape
    x2 = x.reshape(bsz * t, d)
    hf = _proj(x2, _pad_cols(w_in, GLA_COLS).astype(BF16), tn=640, bf16_copy=False)
    hf = hf.reshape(bsz, t, GLA_COLS)
    c = GLA_CHUNK if t % GLA_CHUNK == 0 else t
    valid = c
    if c < GLA_CHUNK:
        hf = jnp.pad(hf, ((0, 0), (0, GLA_CHUNK - c), (0, 0)))
        c = GLA_CHUNK
    w_up_pad = jnp.pad(w_up, ((0, LANE - GLA_GATE_RANK), (0, 0))).astype(BF16)
    o, st = _gla_core(hf, jnp.swapaxes(s0, 2, 3), w_up_pad, b_a.reshape(1, GLA_HK), norm_g.reshape(1, GLA_HV),
                      c=c, valid=valid)
    o = o[:, :t].reshape(bsz * t, GLA_HV)
    y = _mm_res_ln(x2, o, w_out.astype(BF16), ln_g.reshape(1, d), ln_b.reshape(1, d), alpha=alpha)
    return y.reshape(bsz, t, d), jnp.swapaxes(st, 2, 3)


def _dsa_weight(w_in):
    hq = DSA_HEADS * DSA_HEAD_DIM
    iq0 = hq + 2 * DSA_HKV
    ik0 = iq0 + IDX_HEADS * IDX_DIM
    w_iq = w_in[:, iq0:ik0].reshape(-1, IDX_HEADS, IDX_DIM)
    w_iq = jnp.pad(w_iq, ((0, 0), (0, 0), (0, LANE - IDX_DIM))).reshape(-1, IDX_HEADS * LANE)
    w_v = w_in[:, hq + DSA_HKV:iq0].reshape(-1, DSA_KV_HEADS, DSA_HEAD_DIM)
    w_v1 = jnp.pad(w_v, ((0, 0), (0, 0), (0, DSA_HEAD_DIM))).reshape(-1, 2 * DSA_HKV)
    w = jnp.concatenate([w_in[:, :hq], w_iq, w_in[:, hq:iq0], w_v1, _pad_cols(w_in[:, ik0:], LANE)], axis=1)
    return w.astype(BF16)


def _dsa_project(x, w_in):
    bsz, t, d = x.shape
    hf, hb = _proj(x.reshape(bsz * t, d), _dsa_weight(w_in), tn=DSA_COLS // 5, ones_at=DSA_ONES)
    hf = hf.reshape(bsz, t, DSA_COLS)
    k = hf[:, :, DSA_K0:DSA_V0].reshape(bsz, t, DSA_KV_HEADS, DSA_HEAD_DIM)
    v = hf[:, :, DSA_V0:DSA_V1_0].reshape(bsz, t, DSA_KV_HEADS, DSA_HEAD_DIM)
    ik = hf[:, :, DSA_IKW0:DSA_IKW0 + IDX_DIM]
    return hf, hb.reshape(bsz, t, DSA_COLS), k, v, ik


def kernel(x_prompt, x_sample, state_gla, cache_k, cache_v, cache_idx_k, page_table, p_prompt, p_sample,
           ln_mix_g, ln_mix_b, ln_ffn_g, ln_ffn_b, w_in_gla, w_alpha_up, b_alpha, gla_norm_g, w_out_gla,
           w_in_dsa, w_out_dsa, w_ffn_gate, w_ffn_up, w_ffn_down, w_router, w_exp_gate, w_exp_up,
           w_exp_down, w_ple_proj, w_ple_gate):
    depth = ln_mix_g.shape[0]
    alpha = (2.0 * depth) ** 0.25
    d = x_prompt.shape[-1]
    xp, xs = x_prompt, x_sample
    gla_p, gla_s, kp, vp, ikp, ksm, vsm, iksm = [], [], [], [], [], [], [], []
    n_pages = page_table.shape[1]
    pp = 16 if n_pages % 16 == 0 else n_pages
    for i in range(depth):
        j = i // 2
        lg, lb = ln_mix_g[i], ln_mix_b[i]
        if i % 2 == 0:
            gw = (w_in_gla[j], w_alpha_up[j], b_alpha[j], gla_norm_g[j], w_out_gla[j], lg, lb)
            s0 = jnp.zeros((xp.shape[0], GLA_HEADS, GLA_DK, GLA_DV), F32)
            xp, sp = _gla_layer(xp, s0, *gw, alpha=alpha)
            xs, ss = _gla_layer(xs, state_gla[j], *gw, alpha=alpha)
            gla_p.append(sp)
            gla_s.append(ss)
        else:
            w_out = w_out_dsa[j].astype(BF16)
            hf, hb, k_p, v_p, ik_p = _dsa_project(xp, w_in_dsa[j])
            op = _dsa_prompt(hf, hb)
            bp, tp, _ = xp.shape
            xp = _mm_res_ln(xp.reshape(bp * tp, d), op.reshape(bp * tp, -1), w_out, lg.reshape(1, d),
                            lb.reshape(1, d), alpha=alpha).reshape(bp, tp, d)
            hf, hb, k_s, v_s, ik_s = _dsa_project(xs, w_in_dsa[j])
            sc = _dsa_sample_scores(hf, jnp.swapaxes(cache_idx_k[j], 1, 2), page_table, pp=pp)
            sel = _dsa_sample_select(sc, hf)
            n_pool = cache_k.shape[1]
            rows = PAGE_SIZE * DSA_KV_HEADS
            os_ = _dsa_sample_attend(hf, sel, cache_k[j].reshape(n_pool, rows, DSA_HEAD_DIM),
                                     cache_v[j].reshape(n_pool, rows, DSA_HEAD_DIM), page_table, pp=pp)
            bs, ts, _ = xs.shape
            xs = _mm_res_ln(xs.reshape(bs * ts, d), os_.reshape(bs * ts, -1), w_out, lg.reshape(1, d),
                            lb.reshape(1, d), alpha=alpha).reshape(bs, ts, d)
            kp.append(k_p)
            vp.append(v_p)
            ikp.append(ik_p)
            ksm.append(k_s)
            vsm.append(v_s)
            iksm.append(ik_s)
        fg, fb = ln_ffn_g[i].reshape(1, d), ln_ffn_b[i].reshape(1, d)
        outs = []
        for x, p in ((xp, p_prompt[i]), (xs, p_sample[i])):
            bsz, t, _ = x.shape
            x2 = x.reshape(bsz * t, d)
            if i % 2 == 0:
                y2 = _ffn_ln(x2, w_ffn_gate[j].astype(BF16), w_ffn_up[j].astype(BF16), w_ffn_down[j].astype(BF16),
                             fg, fb, alpha=alpha)
            else:
                gates = _router(x2, w_router[j].T)
                y2 = _moe_ln(x2, gates, w_exp_gate[j].astype(BF16), w_exp_up[j].astype(BF16),
                             w_exp_down[j].astype(BF16), fg, fb, alpha=alpha)
            y2 = _ple(y2, p.reshape(bsz * t, -1), w_ple_gate[i].astype(BF16), w_ple_proj[i].astype(BF16))
            outs.append(y2.reshape(bsz, t, d))
        xp, xs = outs
    return (xp, xs, jnp.stack(gla_p), jnp.stack(gla_s), jnp.stack(kp), jnp.stack(vp), jnp.stack(ikp),
            jnp.stack(ksm), jnp.stack(vsm), jnp.stack(iksm))
```

```python
import functools

import jax
import jax.numpy as jnp
from jax import lax
from jax.experimental import pallas as pl
from jax.experimental.pallas import tpu as pltpu

F32 = jnp.float32
BF16 = jnp.bfloat16
I32 = jnp.int32

D_MODEL = 1024
LN_EPS = 1e-5
LANE = 128

GLA_HEADS = 4
GLA_DK = 128
GLA_DV = 256
GLA_GATE_RANK = 16
GLA_TAU = 16.0
GLA_CHUNK = 64
GLA_HK = GLA_HEADS * GLA_DK
GLA_HV = GLA_HEADS * GLA_DV
GLA_COLS = 2 * GLA_HK + 2 * GLA_HV + LANE

DSA_HEADS = 8
DSA_HEAD_DIM = 128
DSA_KV_HEADS = 2
DSA_GROUP = DSA_HEADS // DSA_KV_HEADS
DSA_HKV = DSA_KV_HEADS * DSA_HEAD_DIM
IDX_HEADS = 8
IDX_DIM = 64
TOPK_MAX = 256
PAGE_SIZE = 128
DSA_Q0 = 0
DSA_IQ0 = DSA_HEADS * DSA_HEAD_DIM
DSA_K0 = DSA_IQ0 + IDX_HEADS * LANE
DSA_V0 = DSA_K0 + DSA_HKV
DSA_V1_0 = DSA_V0 + DSA_HKV
DSA_IKW0 = DSA_V1_0 + 2 * DSA_HKV
DSA_COLS = DSA_IKW0 + LANE
DSA_ONES = tuple(DSA_V1_0 + (2 * c + 1) * DSA_HEAD_DIM for c in range(DSA_KV_HEADS))

N_EXPERTS = 8

NEG_INF_KEY = -2139095041
INT_MIN = -2147483648
MASKED = -1e30


def _cparams(sem, vmem_mb):
    return pltpu.CompilerParams(dimension_semantics=sem, vmem_limit_bytes=vmem_mb * 2 ** 20)


def _dot(a, b):
    return jnp.dot(a, b, preferred_element_type=F32)


def _dot_nt(a, b):
    return lax.dot_general(a, b, (((1,), (1,)), ((), ())), preferred_element_type=F32)


def _dot_tn(a, b):
    return lax.dot_general(a, b, (((0,), (0,)), ((), ())), preferred_element_type=F32)


def _sigmoid(x):
    return 1.0 / (1.0 + jnp.exp(-x))


def _layer_norm(y, g, b):
    mu = jnp.mean(y, axis=-1, keepdims=True)
    yc = y - mu
    var = jnp.mean(yc * yc, axis=-1, keepdims=True)
    return yc * lax.rsqrt(var + LN_EPS) * g + b


def _row_tile(n, pref):
    t = min(n, pref)
    assert n % t == 0
    return t


def _proj_kernel(x_ref, w_ref, o_ref, ob_ref, *, tn, ones_at):
    xb = x_ref[...].astype(BF16)
    for c in range(w_ref.shape[1] // tn):
        r = _dot(xb, w_ref[:, c * tn:(c + 1) * tn])
        o_ref[:, c * tn:(c + 1) * tn] = r
        if ob_ref is not None:
            ob_ref[:, c * tn:(c + 1) * tn] = r.astype(BF16)
    for c0 in ones_at:
        ob_ref[:, c0:c0 + LANE] = jnp.ones((x_ref.shape[0], LANE), BF16)


def _proj_f32_kernel(x_ref, w_ref, o_ref, *, tn):
    _proj_kernel(x_ref, w_ref, o_ref, None, tn=tn, ones_at=())


def _proj(x, w, *, tn, ones_at=(), bf16_copy=True):
    n, d = x.shape
    cols = w.shape[1]
    assert cols % tn == 0 and (bf16_copy or not ones_at)
    tm = _row_tile(n, 512)
    out_spec = pl.BlockSpec((tm, cols), lambda i: (i, 0))
    if bf16_copy:
        body = functools.partial(_proj_kernel, tn=tn, ones_at=ones_at)
        out_specs = [out_spec, out_spec]
        out_shape = [jax.ShapeDtypeStruct((n, cols), F32), jax.ShapeDtypeStruct((n, cols), BF16)]
    else:
        body = functools.partial(_proj_f32_kernel, tn=tn)
        out_specs, out_shape = out_spec, jax.ShapeDtypeStruct((n, cols), F32)
    return pl.pallas_call(
        body,
        grid=(n // tm,),
        in_specs=[pl.BlockSpec((tm, d), lambda i: (i, 0)), pl.BlockSpec((d, cols), lambda i: (0, 0))],
        out_specs=out_specs,
        out_shape=out_shape,
        compiler_params=_cparams(("arbitrary",), 56),
        name="proj",
    )(x, w)


def _mm_res_ln_kernel(x_ref, a_ref, w_ref, g_ref, b_ref, o_ref, *, alpha):
    m = _dot(a_ref[...].astype(BF16), w_ref[...])
    o_ref[...] = _layer_norm(alpha * x_ref[...] + m, g_ref[...], b_ref[...])


def _mm_res_ln(x, a, w, g, b, *, alpha):
    n, d = x.shape
    ka = a.shape[1]
    tm = _row_tile(n, 512)
    row = lambda i: (i, 0)
    fixed = lambda i: (0, 0)
    return pl.pallas_call(
        functools.partial(_mm_res_ln_kernel, alpha=alpha),
        grid=(n // tm,),
        in_specs=[pl.BlockSpec((tm, d), row), pl.BlockSpec((tm, ka), row), pl.BlockSpec((ka, d), fixed),
                  pl.BlockSpec((1, d), fixed), pl.BlockSpec((1, d), fixed)],
        out_specs=pl.BlockSpec((tm, d), row),
        out_shape=jax.ShapeDtypeStruct((n, d), F32),
        compiler_params=_cparams(("arbitrary",), 40),
        name="mm_res_ln",
    )(x, a, w, g, b)


def _ffn_ln_kernel(x_ref, wg_ref, wu_ref, wd_ref, g_ref, b_ref, o_ref, xb_ref, acc_ref, *, alpha):
    f = pl.program_id(1)

    @pl.when(f == 0)
    def _():
        xb_ref[...] = x_ref[...].astype(BF16)
        acc_ref[...] = jnp.zeros_like(acc_ref)

    xb = xb_ref[...]
    hg = _dot(xb, wg_ref[...])
    hu = _dot(xb, wu_ref[...])
    h = (hg * _sigmoid(hg) * hu).astype(BF16)
    acc_ref[...] += _dot(h, wd_ref[...])

    @pl.when(f == pl.num_programs(1) - 1)
    def _():
        o_ref[...] = _layer_norm(alpha * x_ref[...] + acc_ref[...], g_ref[...], b_ref[...])


def _ffn_ln(x, wg, wu, wd, g, b, *, alpha, tf=512):
    n, d = x.shape
    ff = wg.shape[1]
    assert ff % tf == 0
    tm = _row_tile(n, 1024)
    return pl.pallas_call(
        functools.partial(_ffn_ln_kernel, alpha=alpha),
        grid=(n // tm, ff // tf),
        in_specs=[pl.BlockSpec((tm, d), lambda i, f: (i, 0)),
                  pl.BlockSpec((d, tf), lambda i, f: (0, f)),
                  pl.BlockSpec((d, tf), lambda i, f: (0, f)),
                  pl.BlockSpec((tf, d), lambda i, f: (f, 0)),
                  pl.BlockSpec((1, d), lambda i, f: (0, 0)),
                  pl.BlockSpec((1, d), lambda i, f: (0, 0))],
        out_specs=pl.BlockSpec((tm, d), lambda i, f: (i, 0)),
        out_shape=jax.ShapeDtypeStruct((n, d), F32),
        scratch_shapes=[pltpu.VMEM((tm, d), BF16), pltpu.VMEM((tm, d), F32)],
        compiler_params=_cparams(("arbitrary", "arbitrary"), 48),
        name="ffn_ln",
    )(x, wg, wu, wd, g, b)


def _ple_kernel(x_ref, p_ref, wg_ref, wp_ref, o_ref):
    x = x_ref[...]
    gate = _sigmoid(_dot(x.astype(BF16), wg_ref[...]))
    o_ref[...] = x + gate * _dot(p_ref[...].astype(BF16), wp_ref[...])


def _ple(x, p, wg, wp):
    n, d = x.shape
    pd = p.shape[1]
    tm = _row_tile(n, 512)
    row = lambda i: (i, 0)
    fixed = lambda i: (0, 0)
    return pl.pallas_call(
        _ple_kernel,
        grid=(n // tm,),
        in_specs=[pl.BlockSpec((tm, d), row), pl.BlockSpec((tm, pd), row),
                  pl.BlockSpec((d, d), fixed), pl.BlockSpec((pd, d), fixed)],
        out_specs=pl.BlockSpec((tm, d), row),
        out_shape=jax.ShapeDtypeStruct((n, d), F32),
        compiler_params=_cparams(("arbitrary",), 40),
        name="ple",
    )(x, p, wg, wp)


def _gla_kernel(q_ref, k_ref, v_ref, g_ref, a_ref, wup_ref, ba_ref, ng_ref, s0_ref, o_ref, st_out_ref, st_ref,
                *, c, valid, nbs):
    t = pl.program_id(1)

    @pl.when(t == 0)
    def _():
        st_ref[...] = s0_ref[...]

    row = lax.broadcasted_iota(I32, (c, c), 0)
    col = lax.broadcasted_iota(I32, (c, c), 1)
    causal = row >= col
    for bi in range(nbs):
        logit = _dot(a_ref[bi].astype(BF16), wup_ref[...]) + ba_ref[...]
        log_a = (jnp.minimum(logit, 0.0) - jnp.log(1.0 + jnp.exp(-jnp.abs(logit)))) * (1.0 / GLA_TAU)
        if valid < c:
            log_a = jnp.where(lax.broadcasted_iota(I32, log_a.shape, 0) < valid, log_a, 0.0)
        bcum = jnp.dot(causal.astype(F32), log_a, preferred_element_type=F32, precision=lax.Precision.HIGHEST)
        blast = bcum[c - 1:c, :]
        q = q_ref[bi] * (GLA_DK ** -0.5)
        k = k_ref[bi]
        q_dec = (q * jnp.exp(bcum)).astype(BF16)
        k_inv = (k * jnp.exp(-bcum)).astype(BF16)
        k_dec = (k * jnp.exp(blast - bcum)).astype(BF16)
        decay = jnp.exp(blast)
        v = v_ref[bi].astype(BF16)
        for h in range(GLA_HEADS):
            sk = slice(h * GLA_DK, (h + 1) * GLA_DK)
            sv = slice(h * GLA_DV, (h + 1) * GLA_DV)
            att = jnp.where(causal, _dot_nt(q_dec[:, sk], k_inv[:, sk]), 0.0).astype(BF16)
            st = st_ref[bi, h]
            o = _dot(att, v[:, sv]) + _dot_nt(q_dec[:, sk], st.astype(BF16))
            st_ref[bi, h] = st * decay[:, sk] + _dot_tn(v[:, sv], k_dec[:, sk])
            mu = jnp.mean(o, axis=-1, keepdims=True)
            oc = o - mu
            var = jnp.mean(oc * oc, axis=-1, keepdims=True)
            on = oc * lax.rsqrt(var + LN_EPS) * ng_ref[:, sv]
            gg = g_ref[bi, :, sv]
            o_ref[bi, :, sv] = (on * (gg * _sigmoid(gg))).astype(BF16)

    @pl.when(t == pl.num_programs(1) - 1)
    def _():
        st_out_ref[...] = st_ref[...]


def _gla_core(h, s0_t, wup, ba, ng, *, c, valid):
    bsz, tlen, _ = h.shape
    assert tlen % c == 0
    nbs = 4 if bsz % 4 == 0 else 1
    v_blk = (2 * GLA_HK) // GLA_HV
    state_spec = pl.BlockSpec((nbs, GLA_HEADS, GLA_DV, GLA_DK), lambda b, t: (b, 0, 0, 0))
    return pl.pallas_call(
        functools.partial(_gla_kernel, c=c, valid=valid, nbs=nbs),
        grid=(bsz // nbs, tlen // c),
        in_specs=[pl.BlockSpec((nbs, c, GLA_HK), lambda b, t: (b, t, 0)),
                  pl.BlockSpec((nbs, c, GLA_HK), lambda b, t: (b, t, 1)),
                  pl.BlockSpec((nbs, c, GLA_HV), lambda b, t: (b, t, v_blk)),
                  pl.BlockSpec((nbs, c, GLA_HV), lambda b, t: (b, t, v_blk + 1)),
                  pl.BlockSpec((nbs, c, LANE), lambda b, t: (b, t, (2 * GLA_HK + 2 * GLA_HV) // LANE)),
                  pl.BlockSpec((LANE, GLA_HK), lambda b, t: (0, 0)),
                  pl.BlockSpec((1, GLA_HK), lambda b, t: (0, 0)),
                  pl.BlockSpec((1, GLA_HV), lambda b, t: (0, 0)),
                  state_spec],
        out_specs=[pl.BlockSpec((nbs, c, GLA_HV), lambda b, t: (b, t, 0)), state_spec],
        out_shape=[jax.ShapeDtypeStruct((bsz, tlen, GLA_HV), BF16),
                   jax.ShapeDtypeStruct((bsz, GLA_HEADS, GLA_DV, GLA_DK), F32)],
        scratch_shapes=[pltpu.VMEM((nbs, GLA_HEADS, GLA_DV, GLA_DK), F32)],
        compiler_params=_cparams(("arbitrary", "arbitrary"), 32),
        name="gla_core",
    )(h, h, h, h, h, wup, ba, ng, s0_t)


def _sortable(s):
    bits = lax.bitcast_convert_type(s, I32)
    return bits ^ (lax.shift_right_arithmetic(bits, 31) & 0x7FFFFFFF)


def _lane_fold(m):
    acc = m[:, 0:LANE]
    for i in range(1, m.shape[1] // LANE):
        acc = acc + m[:, i * LANE:(i + 1) * LANE]
    return acc


def _lane_fold_max(m):
    acc = m[:, 0:LANE]
    for i in range(1, m.shape[1] // LANE):
        acc = jnp.maximum(acc, m[:, i * LANE:(i + 1) * LANE])
    return acc


def _bias_bits(sel):
    return lax.bitcast_convert_type(jnp.where(sel, 0.0, MASKED).astype(F32), I32)


def _kth_largest_key(count_ge, rows, k):
    kf = float(k)
    thr0 = jnp.where(count_ge(jnp.zeros((rows, 1), I32)) >= kf, 0, INT_MIN).astype(I32)

    def bit_body(i, thr):
        cand = thr | lax.shift_left(jnp.int32(1), 30 - i)
        return jnp.where(count_ge(cand) >= kf, cand, thr)

    return lax.fori_loop(0, 31, bit_body, thr0)


def _strict_upper(n):
    return (lax.broadcasted_iota(I32, (n, n), 0) < lax.broadcasted_iota(I32, (n, n), 1)).astype(BF16)


def _dsa_prompt_kernel(q_ref, iq_ref, iw_ref, ikw_ref, k_ref, v1_ref, o_ref,
                       keys_ref, top2_ref, iqs_ref, qg_ref, d_ref, m_ref, acc_ref, *, tq, tk, topk):
    j = pl.program_id(1)
    nkt = ((j + 1) * tq + tk - 1) // tk
    gq = DSA_GROUP * tq
    rp = 16
    for h in range(IDX_HEADS):
        iqs_ref[h * tq:(h + 1) * tq, :] = iq_ref[0, :, h * LANE:(h + 1) * LANE]
    for c in range(DSA_KV_HEADS):
        for g in range(DSA_GROUP):
            hh = c * DSA_GROUP + g
            qg_ref[c, g * tq:(g + 1) * tq, :] = q_ref[0, :, hh * DSA_HEAD_DIM:(hh + 1) * DSA_HEAD_DIM]
    ws = iw_ref[0][:, IDX_DIM:IDX_DIM + IDX_HEADS] * (IDX_HEADS ** -0.5) * (IDX_DIM ** -0.5)

    def score_tile(kt, diagonal):
        ikt = ikw_ref[0, pl.ds(pl.multiple_of(kt * tk, tk), tk), :]
        d_ref[...] = _dot_nt(iqs_ref[...], ikt)
        for r in range(tq // rp):
            s = jnp.zeros((rp, tk), F32)
            for h in range(IDX_HEADS):
                dh = d_ref[h * tq + r * rp:h * tq + (r + 1) * rp, :]
                s = s + ws[r * rp:(r + 1) * rp, h:h + 1] * jnp.maximum(dh, 0.0)
            key = _sortable(s + 0.0)
            if diagonal:
                qpos = j * tq + r * rp + lax.broadcasted_iota(I32, (rp, tk), 0)
                kpos = kt * tk + lax.broadcasted_iota(I32, (rp, tk), 1)
                key = jnp.where(kpos <= qpos, key, INT_MIN)
            keys_ref[kt, r * rp:(r + 1) * rp, :] = key
            rows = slice(r * rp, (r + 1) * rp)
            m1, m2 = top2_ref[rows, 0:LANE], top2_ref[rows, LANE:2 * LANE]
            for lt in range(tk // LANE):
                x = key[:, lt * LANE:(lt + 1) * LANE]
                m2 = jnp.maximum(m2, jnp.minimum(m1, x))
                m1 = jnp.maximum(m1, x)
            top2_ref[rows, :] = jnp.concatenate([m1, m2], axis=1)

    def score_body(kt, carry):
        score_tile(kt, False)
        return carry

    top2_ref[...] = jnp.full(top2_ref.shape, INT_MIN, I32)
    lax.fori_loop(0, nkt - 1, score_body, 0)
    score_tile(nkt - 1, True)

    def count_ge(t):
        def body(kt, acc):
            return acc + _lane_fold(jnp.where(keys_ref[kt] >= t, 1.0, 0.0))
        return jnp.sum(lax.fori_loop(0, nkt, body, jnp.zeros((tq, LANE), F32)), axis=1, keepdims=True)

    kf = float(topk)
    lo0 = jnp.min(top2_ref[:, LANE:2 * LANE], axis=1, keepdims=True)
    hi0 = jnp.max(top2_ref[:, 0:LANE], axis=1, keepdims=True) + 1

    def search_done(lo, hi, c_lo):
        row_done = jnp.logical_or(c_lo == kf, hi == lo + 1)
        return jnp.min(jnp.where(row_done, 1.0, 0.0)) > 0.0

    def search_cond(state):
        return jnp.logical_not(state[4])

    def search_body(state):
        lo, hi, c_lo, c_hi, _ = state
        for _ in range(4):
            mid = lax.shift_right_arithmetic(lo, 1) + lax.shift_right_arithmetic(hi, 1) + (lo & hi & 1)
            c = count_ge(mid)
            ok = c >= kf
            lo, c_lo = jnp.where(ok, mid, lo), jnp.where(ok, c, c_lo)
            hi, c_hi = jnp.where(ok, hi, mid), jnp.where(ok, c_hi, c)
        return lo, hi, c_lo, c_hi, search_done(lo, hi, c_lo)

    c_lo0 = count_ge(lo0)
    thr, _, c_thr, c_above, _ = lax.while_loop(
        search_cond, search_body, (lo0, hi0, c_lo0, jnp.zeros((tq, 1), F32), search_done(lo0, hi0, c_lo0)))
    need = kf - c_above
    tie_row = jnp.logical_and(c_thr > kf, thr > INT_MIN)
    any_tie = jnp.max(jnp.where(tie_row, 1.0, 0.0)) > 0.0

    @pl.when(jnp.logical_not(any_tie))
    def _():
        thr_eff = jnp.maximum(thr, INT_MIN + 1)

        def body(kt, carry):
            keys_ref[kt] = _bias_bits(keys_ref[kt] >= thr_eff)
            return carry
        lax.fori_loop(0, nkt, body, 0)

    @pl.when(any_tie)
    def _():
        upper = _strict_upper(tk)

        def body(kt, seen):
            kk = keys_ref[kt]
            eq = kk == thr
            before = seen + _dot(jnp.where(eq, 1.0, 0.0).astype(BF16), upper)
            sel = jnp.logical_and(kk > INT_MIN, jnp.logical_or(kk > thr, jnp.logical_and(eq, before < need)))
            keys_ref[kt] = _bias_bits(sel)
            return seen + jnp.sum(jnp.where(eq, 1.0, 0.0), axis=1, keepdims=True)
        lax.fori_loop(0, nkt, body, jnp.zeros((tq, 1), F32))

    m_ref[...] = jnp.full(m_ref.shape, MASKED, F32)
    acc_ref[...] = jnp.zeros_like(acc_ref)
    c2 = (DSA_HEAD_DIM ** -0.5) * 1.4426950408889634
    dh = DSA_HEAD_DIM

    def attend_body(kt, carry):
        rows = pl.ds(pl.multiple_of(kt * tk, tk), tk)
        bias = lax.bitcast_convert_type(keys_ref[kt], F32)[None]
        for c in range(DSA_KV_HEADS):
            kc = k_ref[0, rows, c * dh:(c + 1) * dh]
            v1 = v1_ref[0, rows, c * 2 * dh:(c + 1) * 2 * dh]
            x = (_dot_nt(qg_ref[c], kc).reshape(DSA_GROUP, tq, tk) + bias).reshape(gq, tk)
            m_old = m_ref[c]
            m_new = jnp.maximum(m_old, jnp.max(x, axis=1, keepdims=True))
            p = jnp.exp2((x - jnp.concatenate([m_new] * (tk // LANE), axis=1)) * c2).astype(BF16)
            alpha = jnp.exp2((m_old - m_new) * c2)
            acc_ref[c] = jnp.concatenate([alpha, alpha], axis=1) * acc_ref[c] + _dot(p, v1)
            m_ref[c] = m_new
        return carry

    lax.fori_loop(0, nkt, attend_body, 0)
    for c in range(DSA_KV_HEADS):
        oc = acc_ref[c, :, 0:dh] / acc_ref[c, :, dh:2 * dh]
        for g in range(DSA_GROUP):
            hh = c * DSA_GROUP + g
            o_ref[0, :, hh * dh:(hh + 1) * dh] = oc[g * tq:(g + 1) * tq].astype(BF16)


def _dsa_prompt(hf, hb, *, tq=128, tk=1024):
    bsz, s, _ = hf.shape
    tk = min(tk, s)
    assert s % tq == 0 and s % tk == 0 and tk % tq == 0
    topk = min(TOPK_MAX, s // 4)
    assert topk <= 2 * LANE
    hq = DSA_HEADS * DSA_HEAD_DIM
    gq = DSA_GROUP * tq
    return pl.pallas_call(
        functools.partial(_dsa_prompt_kernel, tq=tq, tk=tk, topk=topk),
        grid=(bsz, s // tq),
        in_specs=[pl.BlockSpec((1, tq, hq), lambda b, j: (b, j, DSA_Q0 // hq)),
                  pl.BlockSpec((1, tq, IDX_HEADS * LANE), lambda b, j: (b, j, DSA_IQ0 // (IDX_HEADS * LANE))),
                  pl.BlockSpec((1, tq, LANE), lambda b, j: (b, j, DSA_IKW0 // LANE)),
                  pl.BlockSpec((1, s, LANE), lambda b, j: (b, 0, DSA_IKW0 // LANE)),
                  pl.BlockSpec((1, s, DSA_HKV), lambda b, j: (b, 0, DSA_K0 // DSA_HKV)),
                  pl.BlockSpec((1, s, 2 * DSA_HKV), lambda b, j: (b, 0, DSA_V1_0 // (2 * DSA_HKV)))],
        out_specs=pl.BlockSpec((1, tq, hq), lambda b, j: (b, j, 0)),
        out_shape=jax.ShapeDtypeStruct((bsz, s, hq), BF16),
        scratch_shapes=[pltpu.VMEM((s // tk, tq, tk), I32),
                        pltpu.VMEM((tq, 2 * LANE), I32),
                        pltpu.VMEM((IDX_HEADS * tq, LANE), BF16),
                        pltpu.VMEM((DSA_KV_HEADS, gq, DSA_HEAD_DIM), BF16),
                        pltpu.VMEM((IDX_HEADS * tq, tk), F32),
                        pltpu.VMEM((DSA_KV_HEADS, gq, LANE), F32),
                        pltpu.VMEM((DSA_KV_HEADS, gq, 2 * DSA_HEAD_DIM), F32)],
        compiler_params=_cparams(("arbitrary", "arbitrary"), 56),
        name="dsa_prompt",
    )(hb, hb, hf, hb, hb, hb)


def _sample_ws(iw_ref):
    return iw_ref[0][:, IDX_DIM:IDX_DIM + IDX_HEADS] * (IDX_HEADS ** -0.5)


def _sample_scores(d, ws, t):
    s = jnp.zeros((t, d.shape[1]), F32)
    for h in range(IDX_HEADS):
        s = s + ws[:, h:h + 1] * jnp.maximum(d[h * t:(h + 1) * t] * (IDX_DIM ** -0.5), 0.0)
    return s + 0.0


def _dsa_sample_scores_kernel(pt_ref, iq_ref, iw_ref, *refs, pp, t):
    page_refs, o_ref = refs[:pp], refs[pp]
    iq_stack = jnp.concatenate([iq_ref[0, :, h * LANE:h * LANE + IDX_DIM] for h in range(IDX_HEADS)], axis=0)
    ik_t = jnp.concatenate([r[0].astype(BF16) for r in page_refs], axis=1)
    o_ref[0] = _sample_scores(_dot(iq_stack.astype(BF16), ik_t), _sample_ws(iw_ref), t)


def _dsa_sample_scores(hf, cache_ik_t, page_table, *, pp):
    bsz, t, _ = hf.shape
    n_pages = page_table.shape[1]
    assert n_pages % pp == 0
    iq_w = IDX_HEADS * LANE

    def page_spec(i):
        return pl.BlockSpec((1, IDX_DIM, PAGE_SIZE), lambda b, j, pt: (pt[b, j * pp + i], 0, 0))

    grid_spec = pltpu.PrefetchScalarGridSpec(
        num_scalar_prefetch=1, grid=(bsz, n_pages // pp),
        in_specs=[pl.BlockSpec((1, t, iq_w), lambda b, j, pt: (b, 0, DSA_IQ0 // iq_w)),
                  pl.BlockSpec((1, t, LANE), lambda b, j, pt: (b, 0, DSA_IKW0 // LANE))]
                 + [page_spec(i) for i in range(pp)],
        out_specs=pl.BlockSpec((1, t, pp * PAGE_SIZE), lambda b, j, pt: (b, 0, j)))
    return pl.pallas_call(
        functools.partial(_dsa_sample_scores_kernel, pp=pp, t=t),
        grid_spec=grid_spec,
        out_shape=jax.ShapeDtypeStruct((bsz, t, n_pages * PAGE_SIZE), F32),
        compiler_params=_cparams(("arbitrary", "arbitrary"), 32),
        name="dsa_sample_scores",
    )(page_table, hf, hf, *([cache_ik_t] * pp))


def _dsa_sample_select_kernel(sp_ref, iq_ref, iw_ref, ikw_ref, o_ref, keys_ref, *, t, past, new_w, topk, ck):
    iq_stack = jnp.concatenate([iq_ref[0, :, h * LANE:(h + 1) * LANE] for h in range(IDX_HEADS)], axis=0)
    ik_new = jnp.concatenate([ikw_ref[0], jnp.zeros((new_w - t, LANE), F32)], axis=0)
    d_new = _dot_nt(iq_stack.astype(BF16), ik_new.astype(BF16))
    s_new = _sample_scores(d_new, _sample_ws(iw_ref), t)
    row = lax.broadcasted_iota(I32, (t, new_w), 0)
    lane = lax.broadcasted_iota(I32, (t, new_w), 1)
    s_new = jnp.where(lane <= row, s_new, -jnp.inf)
    keys_ref[:, 0:past] = _sortable(sp_ref[0])
    keys_ref[:, past:past + new_w] = _sortable(s_new)
    n_ck = (past + new_w) // ck

    def counts(pred):
        acc = jnp.zeros((t, LANE), F32)
        for i in range(n_ck):
            acc = acc + _lane_fold(jnp.where(pred(keys_ref[:, i * ck:(i + 1) * ck]), 1.0, 0.0))
        return jnp.sum(acc, axis=1, keepdims=True)

    thr = _kth_largest_key(lambda cand: counts(lambda kk: kk >= cand), t, topk)
    n_gt = counts(lambda kk: kk > thr)
    n_eq = counts(lambda kk: kk == thr)
    need = float(topk) - n_gt
    tie_row = jnp.logical_and(n_eq > need, thr > NEG_INF_KEY)
    any_tie = jnp.max(jnp.where(tie_row, 1.0, 0.0)) > 0.0

    @pl.when(jnp.logical_not(any_tie))
    def _():
        thr_eff = jnp.maximum(thr, NEG_INF_KEY + 1)
        o_ref[0] = jnp.where(keys_ref[...] >= thr_eff, 1.0, 0.0)

    @pl.when(any_tie)
    def _():
        upper = _strict_upper(ck)
        seen = jnp.zeros((t, 1), F32)
        for i in range(n_ck):
            kk = keys_ref[:, i * ck:(i + 1) * ck]
            eq = kk == thr
            before = seen + _dot(jnp.where(eq, 1.0, 0.0).astype(BF16), upper)
            sel = jnp.logical_and(kk > NEG_INF_KEY, jnp.logical_or(kk > thr, jnp.logical_and(eq, before < need)))
            o_ref[0, :, i * ck:(i + 1) * ck] = jnp.where(sel, 1.0, 0.0)
            seen = seen + jnp.sum(jnp.where(eq, 1.0, 0.0), axis=1, keepdims=True)


def _dsa_sample_select(scores_past, hf, *, new_w=512, ck=512):
    bsz, t, past = scores_past.shape
    assert past % ck == 0 and new_w % ck == 0
    topk = min(TOPK_MAX, (past + t) // 4)
    iq_w = IDX_HEADS * LANE
    total = past + new_w
    return pl.pallas_call(
        functools.partial(_dsa_sample_select_kernel, t=t, past=past, new_w=new_w, topk=topk, ck=ck),
        grid=(bsz,),
        in_specs=[pl.BlockSpec((1, t, past), lambda b: (b, 0, 0)),
                  pl.BlockSpec((1, t, iq_w), lambda b: (b, 0, DSA_IQ0 // iq_w)),
                  pl.BlockSpec((1, t, LANE), lambda b: (b, 0, DSA_IKW0 // LANE)),
                  pl.BlockSpec((1, t, LANE), lambda b: (b, 0, DSA_IKW0 // LANE))],
        out_specs=pl.BlockSpec((1, t, total), lambda b: (b, 0, 0)),
        out_shape=jax.ShapeDtypeStruct((bsz, t, total), F32),
        scratch_shapes=[pltpu.VMEM((t, total), I32)],
        compiler_params=_cparams(("arbitrary",), 32),
        name="dsa_sample_select",
    )(scores_past, hf, hf, hf)


def _dsa_sample_attend_kernel(pt_ref, q_ref, kn_ref, vn_ref, selp_ref, seln_ref, *refs, pp, t, new_w):
    k_pages, v_pages = refs[:pp], refs[pp:2 * pp]
    o_ref, m_ref, l_ref, acc_ref = refs[2 * pp:]
    j = pl.program_id(1)
    gt = DSA_GROUP * t

    @pl.when(j == 0)
    def _():
        m_ref[...] = jnp.full(m_ref.shape, MASKED, F32)
        l_ref[...] = jnp.zeros_like(l_ref)
        acc_ref[...] = jnp.zeros_like(acc_ref)

    q_groups = [jnp.concatenate([q_ref[0, :, (c * DSA_GROUP + g) * DSA_HEAD_DIM:(c * DSA_GROUP + g + 1) * DSA_HEAD_DIM]
                                 for g in range(DSA_GROUP)], axis=0).astype(BF16) for c in range(DSA_KV_HEADS)]

    def attend(kcs, vcs, sel):
        n = kcs[0].shape[0]
        for c in range(DSA_KV_HEADS):
            kc, vc = kcs[c], vcs[c]
            logits = (_dot_nt(q_groups[c], kc) * (DSA_HEAD_DIM ** -0.5)).reshape(DSA_GROUP, t, n)
            masked = jnp.where(sel[None], logits, MASKED).reshape(gt, n)
            m_old = m_ref[c]
            m_new = jnp.maximum(m_old, jnp.max(masked, axis=1, keepdims=True))
            p = jnp.where(sel[None], jnp.exp(masked - m_new).reshape(DSA_GROUP, t, n), 0.0).reshape(gt, n)
            alpha = jnp.exp(m_old - m_new)
            l_ref[c] = alpha * l_ref[c] + jnp.sum(p, axis=1, keepdims=True)
            acc_ref[c] = alpha * acc_ref[c] + _dot(p.astype(BF16), vc)
            m_ref[c] = m_new

    def head_rows(pages, c):
        return jnp.concatenate([r[0, pl.ds(c, PAGE_SIZE, stride=DSA_KV_HEADS), :].astype(BF16) for r in pages], axis=0)

    attend([head_rows(k_pages, c) for c in range(DSA_KV_HEADS)],
           [head_rows(v_pages, c) for c in range(DSA_KV_HEADS)], selp_ref[0] != 0.0)

    @pl.when(j == pl.num_programs(1) - 1)
    def _():
        pad = jnp.zeros((new_w - t, DSA_HEAD_DIM), F32)

        def new_rows(ref, c):
            return jnp.concatenate([ref[0, :, c * DSA_HEAD_DIM:(c + 1) * DSA_HEAD_DIM], pad], axis=0).astype(BF16)

        attend([new_rows(kn_ref, c) for c in range(DSA_KV_HEADS)],
               [new_rows(vn_ref, c) for c in range(DSA_KV_HEADS)], seln_ref[0] != 0.0)
        for c in range(DSA_KV_HEADS):
            oc = acc_ref[c] / l_ref[c]
            for g in range(DSA_GROUP):
                hh = c * DSA_GROUP + g
                o_ref[0, :, hh * DSA_HEAD_DIM:(hh + 1) * DSA_HEAD_DIM] = oc[g * t:(g + 1) * t].astype(BF16)


def _dsa_sample_attend(hf, sel, cache_k, cache_v, page_table, *, pp, new_w=512):
    bsz, t, _ = hf.shape
    n_pages = page_table.shape[1]
    past = n_pages * PAGE_SIZE
    assert n_pages % pp == 0 and past % new_w == 0
    hq = DSA_HEADS * DSA_HEAD_DIM
    gt = DSA_GROUP * t

    def page_spec(i):
        return pl.BlockSpec((1, PAGE_SIZE * DSA_KV_HEADS, DSA_HEAD_DIM), lambda b, j, pt: (pt[b, j * pp + i], 0, 0))

    grid_spec = pltpu.PrefetchScalarGridSpec(
        num_scalar_prefetch=1, grid=(bsz, n_pages // pp),
        in_specs=[pl.BlockSpec((1, t, hq), lambda b, j, pt: (b, 0, DSA_Q0 // hq)),
                  pl.BlockSpec((1, t, DSA_HKV), lambda b, j, pt: (b, 0, DSA_K0 // DSA_HKV)),
                  pl.BlockSpec((1, t, DSA_HKV), lambda b, j, pt: (b, 0, DSA_V0 // DSA_HKV)),
                  pl.BlockSpec((1, t, pp * PAGE_SIZE), lambda b, j, pt: (b, 0, j)),
                  pl.BlockSpec((1, t, new_w), lambda b, j, pt: (b, 0, past // new_w))]
                 + [page_spec(i) for i in range(pp)] * 2,
        out_specs=pl.BlockSpec((1, t, hq), lambda b, j, pt: (b, 0, 0)),
        scratch_shapes=[pltpu.VMEM((DSA_KV_HEADS, gt, 1), F32),
                        pltpu.VMEM((DSA_KV_HEADS, gt, 1), F32),
                        pltpu.VMEM((DSA_KV_HEADS, gt, DSA_HEAD_DIM), F32)])
    return pl.pallas_call(
        functools.partial(_dsa_sample_attend_kernel, pp=pp, t=t, new_w=new_w),
        grid_spec=grid_spec,
        out_shape=jax.ShapeDtypeStruct((bsz, t, hq), BF16),
        compiler_params=_cparams(("arbitrary", "arbitrary"), 48),
        name="dsa_sample_attend",
    )(page_table, hf, hf, hf, sel, sel, *([cache_k] * pp), *([cache_v] * pp))


def _router_kernel(x_ref, wt_ref, o_ref):
    x = x_ref[...]
    wt = wt_ref[...]
    xh = x.astype(BF16)
    xl = (x - xh.astype(F32)).astype(BF16)
    wh = wt.astype(BF16)
    wl = (wt - wh.astype(F32)).astype(BF16)
    logits = _dot_nt(wh, xh) + (_dot_nt(wl, xh) + _dot_nt(wh, xl))
    sub = lax.broadcasted_iota(I32, logits.shape, 0)
    l1 = jnp.max(logits, axis=0, keepdims=True)
    i1 = jnp.min(jnp.where(logits == l1, sub, N_EXPERTS), axis=0, keepdims=True)
    rest = jnp.where(sub == i1, -jnp.inf, logits)
    l2 = jnp.max(rest, axis=0, keepdims=True)
    i2 = jnp.min(jnp.where(rest == l2, sub, N_EXPERTS), axis=0, keepdims=True)
    e2 = jnp.exp(l2 - l1)
    denom = 1.0 + e2
    o_ref[...] = jnp.where(sub == i1, 1.0 / denom, jnp.where(sub == i2, e2 / denom, 0.0))


def _router(x, w_t):
    n, d = x.shape
    tm = _row_tile(n, 512)
    return pl.pallas_call(
        _router_kernel,
        grid=(n // tm,),
        in_specs=[pl.BlockSpec((tm, d), lambda i: (i, 0)), pl.BlockSpec((N_EXPERTS, d), lambda i: (0, 0))],
        out_specs=pl.BlockSpec((N_EXPERTS, tm), lambda i: (0, i)),
        out_shape=jax.ShapeDtypeStruct((N_EXPERTS, n), F32),
        compiler_params=_cparams(("arbitrary",), 32),
        name="router",
    )(x, w_t)


MOE_ROWS = 64


def _moe_ln_kernel(cnt_ref, x_ref, gt_ref, wg_ref, wu_ref, wd_ref, g_ref, b_ref, o_ref,
                   xb_ref, rank_ref, xg_ref, acc_ref, y_ref, upper_ref, *, alpha, arms):
    i, e, f = pl.program_id(0), pl.program_id(1), pl.program_id(2)
    tm = x_ref.shape[0]
    first_f = f == 0
    last_f = f == pl.num_programs(2) - 1
    nb = (cnt_ref[i * N_EXPERTS + e] + MOE_ROWS - 1) // MOE_ROWS

    @pl.when(jnp.logical_and(i == 0, jnp.logical_and(e == 0, first_f)))
    def _():
        upper_ref[...] = _strict_upper(tm)

    @pl.when(jnp.logical_and(e == 0, first_f))
    def _():
        xb_ref[...] = x_ref[...].astype(BF16)
        y_ref[...] = jnp.zeros_like(y_ref)
        chosen = jnp.where(gt_ref[...] > 0.0, 1.0, 0.0).astype(BF16)
        rank_ref[...] = _dot(chosen, upper_ref[...])

    ge = gt_ref[pl.ds(e, 1), :]
    slot = jnp.where(ge > 0.0, rank_ref[pl.ds(e, 1), :], -1.0)

    def one_hot(start, rows):
        r = (start + lax.broadcasted_iota(I32, (rows, 1), 0)).astype(F32)
        return slot == r

    def for_arms(fn):
        for size in arms:
            start = pl.multiple_of((nb & ~(2 * size - 1)) * MOE_ROWS, MOE_ROWS)

            @pl.when((nb & size) != 0)
            def _():
                fn(start, size * MOE_ROWS)

    @pl.when(first_f)
    def _():
        def gather(start, rows):
            hot = jnp.where(one_hot(start, rows), 1.0, 0.0).astype(BF16)
            xg_ref[pl.ds(start, rows), :] = _dot(hot, xb_ref[...]).astype(BF16)
            acc_ref[pl.ds(start, rows), :] = jnp.zeros((rows, acc_ref.shape[1]), F32)
        for_arms(gather)

    def expert(start, rows):
        xs = xg_ref[pl.ds(start, rows), :]
        hg = _dot(xs, wg_ref[0])
        hu = _dot(xs, wu_ref[0])
        h = (hg * _sigmoid(hg) * hu).astype(BF16)
        acc_ref[pl.ds(start, rows), :] += _dot(h, wd_ref[0])
    for_arms(expert)

    @pl.when(last_f)
    def _():
        def scatter(start, rows):
            hot = one_hot(start, rows)
            gate = jnp.sum(jnp.where(hot, ge, 0.0), axis=1, keepdims=True)
            v = (gate * acc_ref[pl.ds(start, rows), :]).astype(BF16)
            y_ref[...] += _dot_tn(jnp.where(hot, 1.0, 0.0).astype(BF16), v)
        for_arms(scatter)

    @pl.when(jnp.logical_and(e == N_EXPERTS - 1, last_f))
    def _():
        o_ref[...] = _layer_norm(alpha * x_ref[...] + y_ref[...], g_ref[...], b_ref[...])


def _moe_ln(x, gates_t, wg, wu, wd, g, b, *, alpha, tf=896):
    n, d = x.shape
    ff = wg.shape[2]
    assert ff % tf == 0
    tm = _row_tile(n, 1024)
    assert tm % MOE_ROWS == 0
    max_blocks = tm // MOE_ROWS
    arms = tuple(s for s in (64, 32, 16, 8, 4, 2, 1) if s <= max_blocks)
    assert arms[0] == max_blocks
    counts = jnp.sum((gates_t > 0.0).reshape(N_EXPERTS, n // tm, tm), axis=2).astype(I32).T.reshape(-1)
    grid_spec = pltpu.PrefetchScalarGridSpec(
        num_scalar_prefetch=1, grid=(n // tm, N_EXPERTS, ff // tf),
        in_specs=[pl.BlockSpec((tm, d), lambda i, e, f, c: (i, 0)),
                  pl.BlockSpec((N_EXPERTS, tm), lambda i, e, f, c: (0, i)),
                  pl.BlockSpec((1, d, tf), lambda i, e, f, c: (e, 0, f)),
                  pl.BlockSpec((1, d, tf), lambda i, e, f, c: (e, 0, f)),
                  pl.BlockSpec((1, tf, d), lambda i, e, f, c: (e, f, 0)),
                  pl.BlockSpec((1, d), lambda i, e, f, c: (0, 0)),
                  pl.BlockSpec((1, d), lambda i, e, f, c: (0, 0))],
        out_specs=pl.BlockSpec((tm, d), lambda i, e, f, c: (i, 0)),
        scratch_shapes=[pltpu.VMEM((tm, d), BF16), pltpu.VMEM((N_EXPERTS, tm), F32), pltpu.VMEM((tm, d), BF16),
                        pltpu.VMEM((tm, d), F32), pltpu.VMEM((tm, d), F32), pltpu.VMEM((tm, tm), BF16)])
    return pl.pallas_call(
        functools.partial(_moe_ln_kernel, alpha=alpha, arms=arms),
        grid_spec=grid_spec,
        out_shape=jax.ShapeDtypeStruct((n, d), F32),
        compiler_params=_cparams(("arbitrary", "arbitrary", "arbitrary"), 56),
        name="moe_ln",
    )(counts, x, gates_t, wg, wu, wd, g, b)


def _pad_cols(w, cols):
    return jnp.pad(w, ((0, 0), (0, cols - w.shape[1])))


def _gla_layer(x, s0, w_in, w_up, b_a, norm_g, w_out, ln_g, ln_b, *, alpha):
    bsz, t, d = x.shape
    x2 = x.reshape(bsz * t, d)
    hf = _proj(x2, _pad_cols(w_in, GLA_COLS).astype(BF16), tn=640, bf16_copy=False)
    hf = hf.reshape(bsz, t, GLA_COLS)
    c = GLA_CHUNK if t % GLA_CHUNK == 0 else t
    valid = c
    if c < GLA_CHUNK:
        hf = jnp.pad(hf, ((0, 0), (0, GLA_CHUNK - c), (0, 0)))
        c = GLA_CHUNK
    w_up_pad = jnp.pad(w_up, ((0, LANE - GLA_GATE_RANK), (0, 0))).astype(BF16)
    o, st = _gla_core(hf, jnp.swapaxes(s0, 2, 3), w_up_pad, b_a.reshape(1, GLA_HK), norm_g.reshape(1, GLA_HV),
                      c=c, valid=valid)
    o = o[:, :t].reshape(bsz * t, GLA_HV)
    y = _mm_res_ln(x2, o, w_out.astype(BF16), ln_g.reshape(1, d), ln_b.reshape(1, d), alpha=alpha)
    return y.reshape(bsz, t, d), jnp.swapaxes(st, 2, 3)


def _dsa_weight(w_in):
    hq = DSA_HEADS * DSA_HEAD_DIM
    iq0 = hq + 2 * DSA_HKV
    ik0 = iq0 + IDX_HEADS * IDX_DIM
    w_iq = w_in[:, iq0:ik0].reshape(-1, IDX_HEADS, IDX_DIM)
    w_iq = jnp.pad(w_iq, ((0, 0), (0, 0), (0, LANE - IDX_DIM))).reshape(-1, IDX_HEADS * LANE)
    w_v = w_in[:, hq + DSA_HKV:iq0].reshape(-1, DSA_KV_HEADS, DSA_HEAD_DIM)
    w_v1 = jnp.pad(w_v, ((0, 0), (0, 0), (0, DSA_HEAD_DIM))).reshape(-1, 2 * DSA_HKV)
    w = jnp.concatenate([w_in[:, :hq], w_iq, w_in[:, hq:iq0], w_v1, _pad_cols(w_in[:, ik0:], LANE)], axis=1)
    return w.astype(BF16)


def _dsa_project(x, w_in):
    bsz, t, d = x.shape
    hf, hb = _proj(x.reshape(bsz * t, d), _dsa_weight(w_in), tn=DSA_COLS // 5, ones_at=DSA_ONES)
    hf = hf.reshape(bsz, t, DSA_COLS)
    k = hf[:, :, DSA_K0:DSA_V0].reshape(bsz, t, DSA_KV_HEADS, DSA_HEAD_DIM)
    v = hf[:, :, DSA_V0:DSA_V1_0].reshape(bsz, t, DSA_KV_HEADS, DSA_HEAD_DIM)
    ik = hf[:, :, DSA_IKW0:DSA_IKW0 + IDX_DIM]
    return hf, hb.reshape(bsz, t, DSA_COLS), k, v, ik


def kernel(x_prompt, x_sample, state_gla, cache_k, cache_v, cache_idx_k, page_table, p_prompt, p_sample,
           ln_mix_g, ln_mix_b, ln_ffn_g, ln_ffn_b, w_in_gla, w_alpha_up, b_alpha, gla_norm_g, w_out_gla,
           w_in_dsa, w_out_dsa, w_ffn_gate, w_ffn_up, w_ffn_down, w_router, w_exp_gate, w_exp_up,
           w_exp_down, w_ple_proj, w_ple_gate):
    depth = ln_mix_g.shape[0]
    alpha = (2.0 * depth) ** 0.25
    d = x_prompt.shape[-1]
    xp, xs = x_prompt, x_sample
    gla_p, gla_s, kp, vp, ikp, ksm, vsm, iksm = [], [], [], [], [], [], [], []
    n_pages = page_table.shape[1]
    pp = 16 if n_pages % 16 == 0 else n_pages
    for i in range(depth):
        j = i // 2
        lg, lb = ln_mix_g[i], ln_mix_b[i]
        if i % 2 == 0:
            gw = (w_in_gla[j], w_alpha_up[j], b_alpha[j], gla_norm_g[j], w_out_gla[j], lg, lb)
            s0 = jnp.zeros((xp.shape[0], GLA_HEADS, GLA_DK, GLA_DV), F32)
            xp, sp = _gla_layer(xp, s0, *gw, alpha=alpha)
            xs, ss = _gla_layer(xs, state_gla[j], *gw, alpha=alpha)
            gla_p.append(sp)
            gla_s.append(ss)
        else:
            w_out = w_out_dsa[j].astype(BF16)
            hf, hb, k_p, v_p, ik_p = _dsa_project(xp, w_in_dsa[j])
            op = _dsa_prompt(hf, hb)
            bp, tp, _ = xp.shape
            xp = _mm_res_ln(xp.reshape(bp * tp, d), op.reshape(bp * tp, -1), w_out, lg.reshape(1, d),
                            lb.reshape(1, d), alpha=alpha).reshape(bp, tp, d)
            hf, hb, k_s, v_s, ik_s = _dsa_project(xs, w_in_dsa[j])
            sc = _dsa_sample_scores(hf, jnp.swapaxes(cache_idx_k[j], 1, 2), page_table, pp=pp)
            sel = _dsa_sample_select(sc, hf)
            n_pool = cache_k.shape[1]
            rows = PAGE_SIZE * DSA_KV_HEADS
            os_ = _dsa_sample_attend(hf, sel, cache_k[j].reshape(n_pool, rows, DSA_HEAD_DIM),
                                     cache_v[j].reshape(n_pool, rows, DSA_HEAD_DIM), page_table, pp=pp)
            bs, ts, _ = xs.shape
            xs = _mm_res_ln(xs.reshape(bs * ts, d), os_.reshape(bs * ts, -1), w_out, lg.reshape(1, d),
                            lb.reshape(1, d), alpha=alpha).reshape(bs, ts, d)
            kp.append(k_p)
            vp.append(v_p)
            ikp.append(ik_p)
            ksm.append(k_s)
            vsm.append(v_s)
            iksm.append(ik_s)
        fg, fb = ln_ffn_g[i].reshape(1, d), ln_ffn_b[i].reshape(1, d)
        outs = []
        for x, p in ((xp, p_prompt[i]), (xs, p_sample[i])):
            bsz, t, _ = x.shape
            x2 = x.reshape(bsz * t, d)
            if i % 2 == 0:
                y2 = _ffn_ln(x2, w_ffn_gate[j].astype(BF16), w_ffn_up[j].astype(BF16), w_ffn_down[j].astype(BF16),
                             fg, fb, alpha=alpha)
            else:
                gates = _router(x2, w_router[j].T)
                y2 = _moe_ln(x2, gates, w_exp_gate[j].astype(BF16), w_exp_up[j].astype(BF16),
                             w_exp_down[j].astype(BF16), fg, fb, alpha=alpha)
            y2 = _ple(y2, p.reshape(bsz * t, -1), w_ple_gate[i].astype(BF16), w_ple_proj[i].astype(BF16))
            outs.append(y2.reshape(bsz, t, d))
        xp, xs = outs
    return (xp, xs, jnp.stack(gla_p), jnp.stack(gla_s), jnp.stack(kp), jnp.stack(vp), jnp.stack(ikp),
            jnp.stack(ksm), jnp.stack(vsm), jnp.stack(iksm))
```

```python
import functools

import jax
import jax.numpy as jnp
from jax import lax
from jax.experimental import pallas as pl
from jax.experimental.pallas import tpu as pltpu

F32 = jnp.float32
BF16 = jnp.bfloat16
I32 = jnp.int32

D_MODEL = 1024
LN_EPS = 1e-5
LANE = 128

GLA_HEADS = 4
GLA_DK = 128
GLA_DV = 256
GLA_GATE_RANK = 16
GLA_TAU = 16.0
GLA_CHUNK = 64
GLA_HK = GLA_HEADS * GLA_DK
GLA_HV = GLA_HEADS * GLA_DV
GLA_COLS = 2 * GLA_HK + 2 * GLA_HV + LANE

DSA_HEADS = 8
DSA_HEAD_DIM = 128
DSA_KV_HEADS = 2
DSA_GROUP = DSA_HEADS // DSA_KV_HEADS
DSA_HKV = DSA_KV_HEADS * DSA_HEAD_DIM
IDX_HEADS = 8
IDX_DIM = 64
TOPK_MAX = 256
PAGE_SIZE = 128
DSA_Q0 = 0
DSA_IQ0 = DSA_HEADS * DSA_HEAD_DIM
DSA_K0 = DSA_IQ0 + IDX_HEADS * LANE
DSA_V0 = DSA_K0 + DSA_HKV
DSA_V1_0 = DSA_V0 + DSA_HKV
DSA_IKW0 = DSA_V1_0 + 2 * DSA_HKV
DSA_COLS = DSA_IKW0 + LANE
DSA_ONES = tuple(DSA_V1_0 + (2 * c + 1) * DSA_HEAD_DIM for c in range(DSA_KV_HEADS))

N_EXPERTS = 8

NEG_INF_KEY = -2139095041
INT_MIN = -2147483648
MASKED = -1e30


def _cparams(sem, vmem_mb):
    return pltpu.CompilerParams(dimension_semantics=sem, vmem_limit_bytes=vmem_mb * 2 ** 20)


def _dot(a, b):
    return jnp.dot(a, b, preferred_element_type=F32)


def _dot_nt(a, b):
    return lax.dot_general(a, b, (((1,), (1,)), ((), ())), preferred_element_type=F32)


def _dot_tn(a, b):
    return lax.dot_general(a, b, (((0,), (0,)), ((), ())), preferred_element_type=F32)


def _sigmoid(x):
    return 1.0 / (1.0 + jnp.exp(-x))


def _layer_norm(y, g, b):
    mu = jnp.mean(y, axis=-1, keepdims=True)
    yc = y - mu
    var = jnp.mean(yc * yc, axis=-1, keepdims=True)
    return yc * lax.rsqrt(var + LN_EPS) * g + b


def _row_tile(n, pref):
    t = min(n, pref)
    assert n % t == 0
    return t


def _proj_kernel(x_ref, w_ref, o_ref, ob_ref, *, tn, ones_at):
    xb = x_ref[...].astype(BF16)
    for c in range(w_ref.shape[1] // tn):
        r = _dot(xb, w_ref[:, c * tn:(c + 1) * tn])
        o_ref[:, c * tn:(c + 1) * tn] = r
        if ob_ref is not None:
            ob_ref[:, c * tn:(c + 1) * tn] = r.astype(BF16)
    for c0 in ones_at:
        ob_ref[:, c0:c0 + LANE] = jnp.ones((x_ref.shape[0], LANE), BF16)


def _proj_f32_kernel(x_ref, w_ref, o_ref, *, tn):
    _proj_kernel(x_ref, w_ref, o_ref, None, tn=tn, ones_at=())


def _proj(x, w, *, tn, ones_at=(), bf16_copy=True):
    n, d = x.shape
    cols = w.shape[1]
    assert cols % tn == 0 and (bf16_copy or not ones_at)
    tm = _row_tile(n, 512)
    out_spec = pl.BlockSpec((tm, cols), lambda i: (i, 0))
    if bf16_copy:
        body = functools.partial(_proj_kernel, tn=tn, ones_at=ones_at)
        out_specs = [out_spec, out_spec]
        out_shape = [jax.ShapeDtypeStruct((n, cols), F32), jax.ShapeDtypeStruct((n, cols), BF16)]
    else:
        body = functools.partial(_proj_f32_kernel, tn=tn)
        out_specs, out_shape = out_spec, jax.ShapeDtypeStruct((n, cols), F32)
    return pl.pallas_call(
        body,
        grid=(n // tm,),
        in_specs=[pl.BlockSpec((tm, d), lambda i: (i, 0)), pl.BlockSpec((d, cols), lambda i: (0, 0))],
        out_specs=out_specs,
        out_shape=out_shape,
        compiler_params=_cparams(("arbitrary",), 56),
        name="proj",
    )(x, w)


def _mm_res_ln_kernel(x_ref, a_ref, w_ref, g_ref, b_ref, o_ref, ob_ref, *, alpha):
    m = _dot(a_ref[...].astype(BF16), w_ref[...])
    y = _layer_norm(alpha * x_ref[...] + m, g_ref[...], b_ref[...])
    o_ref[...] = y
    ob_ref[...] = y.astype(BF16)


def _mm_res_ln(x, a, w, g, b, *, alpha):
    n, d = x.shape
    ka = a.shape[1]
    tm = _row_tile(n, 512)
    row = lambda i: (i, 0)
    fixed = lambda i: (0, 0)
    return pl.pallas_call(
        functools.partial(_mm_res_ln_kernel, alpha=alpha),
        grid=(n // tm,),
        in_specs=[pl.BlockSpec((tm, d), row), pl.BlockSpec((tm, ka), row), pl.BlockSpec((ka, d), fixed),
                  pl.BlockSpec((1, d), fixed), pl.BlockSpec((1, d), fixed)],
        out_specs=[pl.BlockSpec((tm, d), row), pl.BlockSpec((tm, d), row)],
        out_shape=[jax.ShapeDtypeStruct((n, d), F32), jax.ShapeDtypeStruct((n, d), BF16)],
        compiler_params=_cparams(("arbitrary",), 40),
        name="mm_res_ln",
    )(x, a, w, g, b)


def _ffn_ln_kernel(x_ref, wg_ref, wu_ref, wd_ref, g_ref, b_ref, o_ref, xb_ref, acc_ref, *, alpha):
    f = pl.program_id(1)

    @pl.when(f == 0)
    def _():
        xb_ref[...] = x_ref[...].astype(BF16)
        acc_ref[...] = jnp.zeros_like(acc_ref)

    xb = xb_ref[...]
    hg = _dot(xb, wg_ref[...])
    hu = _dot(xb, wu_ref[...])
    h = (hg * _sigmoid(hg) * hu).astype(BF16)
    acc_ref[...] += _dot(h, wd_ref[...])

    @pl.when(f == pl.num_programs(1) - 1)
    def _():
        o_ref[...] = _layer_norm(alpha * x_ref[...] + acc_ref[...], g_ref[...], b_ref[...])


def _ffn_ln(x, wg, wu, wd, g, b, *, alpha, tf=512):
    n, d = x.shape
    ff = wg.shape[1]
    assert ff % tf == 0
    tm = _row_tile(n, 1024)
    return pl.pallas_call(
        functools.partial(_ffn_ln_kernel, alpha=alpha),
        grid=(n // tm, ff // tf),
        in_specs=[pl.BlockSpec((tm, d), lambda i, f: (i, 0)),
                  pl.BlockSpec((d, tf), lambda i, f: (0, f)),
                  pl.BlockSpec((d, tf), lambda i, f: (0, f)),
                  pl.BlockSpec((tf, d), lambda i, f: (f, 0)),
                  pl.BlockSpec((1, d), lambda i, f: (0, 0)),
                  pl.BlockSpec((1, d), lambda i, f: (0, 0))],
        out_specs=pl.BlockSpec((tm, d), lambda i, f: (i, 0)),
        out_shape=jax.ShapeDtypeStruct((n, d), F32),
        scratch_shapes=[pltpu.VMEM((tm, d), BF16), pltpu.VMEM((tm, d), F32)],
        compiler_params=_cparams(("arbitrary", "arbitrary"), 48),
        name="ffn_ln",
    )(x, wg, wu, wd, g, b)


def _ple_kernel(x_ref, p_ref, wg_ref, wp_ref, o_ref):
    x = x_ref[...]
    gate = _sigmoid(_dot(x.astype(BF16), wg_ref[...]))
    o_ref[...] = x + gate * _dot(p_ref[...].astype(BF16), wp_ref[...])


def _ple(x, p, wg, wp):
    n, d = x.shape
    pd = p.shape[1]
    tm = _row_tile(n, 512)
    row = lambda i: (i, 0)
    fixed = lambda i: (0, 0)
    return pl.pallas_call(
        _ple_kernel,
        grid=(n // tm,),
        in_specs=[pl.BlockSpec((tm, d), row), pl.BlockSpec((tm, pd), row),
                  pl.BlockSpec((d, d), fixed), pl.BlockSpec((pd, d), fixed)],
        out_specs=pl.BlockSpec((tm, d), row),
        out_shape=jax.ShapeDtypeStruct((n, d), F32),
        compiler_params=_cparams(("arbitrary",), 40),
        name="ple",
    )(x, p, wg, wp)


def _res_ln_ple_kernel(x_ref, f_ref, p_ref, g_ref, b_ref, wg_ref, wp_ref, o_ref, *, alpha):
    z = _layer_norm(alpha * x_ref[...] + f_ref[...], g_ref[...], b_ref[...])
    gate = _sigmoid(_dot(z.astype(BF16), wg_ref[...]))
    o_ref[...] = z + gate * _dot(p_ref[...].astype(BF16), wp_ref[...])


def _res_ln_ple(x, f, p, g, b, wg, wp, *, alpha):
    n, d = x.shape
    pd = p.shape[1]
    tm = _row_tile(n, 512)
    row = lambda i: (i, 0)
    fixed = lambda i: (0, 0)
    return pl.pallas_call(
        functools.partial(_res_ln_ple_kernel, alpha=alpha),
        grid=(n // tm,),
        in_specs=[pl.BlockSpec((tm, d), row), pl.BlockSpec((tm, d), row), pl.BlockSpec((tm, pd), row),
                  pl.BlockSpec((1, d), fixed), pl.BlockSpec((1, d), fixed),
                  pl.BlockSpec((d, d), fixed), pl.BlockSpec((pd, d), fixed)],
        out_specs=pl.BlockSpec((tm, d), row),
        out_shape=jax.ShapeDtypeStruct((n, d), F32),
        compiler_params=_cparams(("arbitrary",), 40),
        name="res_ln_ple",
    )(x, f, p, g, b, wg, wp)


def _gla_kernel(q_ref, k_ref, v_ref, g_ref, a_ref, wup_ref, ba_ref, ng_ref, s0_ref, o_ref, st_out_ref, st_ref,
                *, c, valid, nbs):
    t = pl.program_id(1)

    @pl.when(t == 0)
    def _():
        st_ref[...] = s0_ref[...]

    row = lax.broadcasted_iota(I32, (c, c), 0)
    col = lax.broadcasted_iota(I32, (c, c), 1)
    causal = row >= col
    for bi in range(nbs):
        logit = _dot(a_ref[bi].astype(BF16), wup_ref[...]) + ba_ref[...]
        log_a = (jnp.minimum(logit, 0.0) - jnp.log(1.0 + jnp.exp(-jnp.abs(logit)))) * (1.0 / GLA_TAU)
        if valid < c:
            log_a = jnp.where(lax.broadcasted_iota(I32, log_a.shape, 0) < valid, log_a, 0.0)
        bcum = jnp.dot(causal.astype(F32), log_a, preferred_element_type=F32, precision=lax.Precision.HIGHEST)
        blast = bcum[c - 1:c, :]
        q = q_ref[bi] * (GLA_DK ** -0.5)
        k = k_ref[bi]
        q_dec = (q * jnp.exp(bcum)).astype(BF16)
        k_inv = (k * jnp.exp(-bcum)).astype(BF16)
        k_dec = (k * jnp.exp(blast - bcum)).astype(BF16)
        decay = jnp.exp(blast)
        v = v_ref[bi].astype(BF16)
        for h in range(GLA_HEADS):
            sk = slice(h * GLA_DK, (h + 1) * GLA_DK)
            sv = slice(h * GLA_DV, (h + 1) * GLA_DV)
            att = jnp.where(causal, _dot_nt(q_dec[:, sk], k_inv[:, sk]), 0.0).astype(BF16)
            st = st_ref[bi, h]
            o = _dot(att, v[:, sv]) + _dot_nt(q_dec[:, sk], st.astype(BF16))
            st_ref[bi, h] = st * decay[:, sk] + _dot_tn(v[:, sv], k_dec[:, sk])
            mu = jnp.mean(o, axis=-1, keepdims=True)
            oc = o - mu
            var = jnp.mean(oc * oc, axis=-1, keepdims=True)
            on = oc * lax.rsqrt(var + LN_EPS) * ng_ref[:, sv]
            gg = g_ref[bi, :, sv]
            o_ref[bi, :, sv] = (on * (gg * _sigmoid(gg))).astype(BF16)

    @pl.when(t == pl.num_programs(1) - 1)
    def _():
        st_out_ref[...] = st_ref[...]


def _gla_core(h, s0_t, wup, ba, ng, *, c, valid):
    bsz, tlen, _ = h.shape
    assert tlen % c == 0
    nbs = 4 if bsz % 4 == 0 else 1
    v_blk = (2 * GLA_HK) // GLA_HV
    state_spec = pl.BlockSpec((nbs, GLA_HEADS, GLA_DV, GLA_DK), lambda b, t: (b, 0, 0, 0))
    return pl.pallas_call(
        functools.partial(_gla_kernel, c=c, valid=valid, nbs=nbs),
        grid=(bsz // nbs, tlen // c),
        in_specs=[pl.BlockSpec((nbs, c, GLA_HK), lambda b, t: (b, t, 0)),
                  pl.BlockSpec((nbs, c, GLA_HK), lambda b, t: (b, t, 1)),
                  pl.BlockSpec((nbs, c, GLA_HV), lambda b, t: (b, t, v_blk)),
                  pl.BlockSpec((nbs, c, GLA_HV), lambda b, t: (b, t, v_blk + 1)),
                  pl.BlockSpec((nbs, c, LANE), lambda b, t: (b, t, (2 * GLA_HK + 2 * GLA_HV) // LANE)),
                  pl.BlockSpec((LANE, GLA_HK), lambda b, t: (0, 0)),
                  pl.BlockSpec((1, GLA_HK), lambda b, t: (0, 0)),
                  pl.BlockSpec((1, GLA_HV), lambda b, t: (0, 0)),
                  state_spec],
        out_specs=[pl.BlockSpec((nbs, c, GLA_HV), lambda b, t: (b, t, 0)), state_spec],
        out_shape=[jax.ShapeDtypeStruct((bsz, tlen, GLA_HV), BF16),
                   jax.ShapeDtypeStruct((bsz, GLA_HEADS, GLA_DV, GLA_DK), F32)],
        scratch_shapes=[pltpu.VMEM((nbs, GLA_HEADS, GLA_DV, GLA_DK), F32)],
        compiler_params=_cparams(("arbitrary", "arbitrary"), 32),
        name="gla_core",
    )(h, h, h, h, h, wup, ba, ng, s0_t)


def _sortable(s):
    bits = lax.bitcast_convert_type(s, I32)
    return bits ^ (lax.shift_right_arithmetic(bits, 31) & 0x7FFFFFFF)


def _lane_fold(m):
    acc = m[:, 0:LANE]
    for i in range(1, m.shape[1] // LANE):
        acc = acc + m[:, i * LANE:(i + 1) * LANE]
    return acc


def _lane_fold_max(m):
    acc = m[:, 0:LANE]
    for i in range(1, m.shape[1] // LANE):
        acc = jnp.maximum(acc, m[:, i * LANE:(i + 1) * LANE])
    return acc


def _bias_bits(sel):
    return lax.bitcast_convert_type(jnp.where(sel, 0.0, MASKED).astype(F32), I32)


def _kth_largest_key(count_ge, rows, k):
    kf = float(k)
    thr0 = jnp.where(count_ge(jnp.zeros((rows, 1), I32)) >= kf, 0, INT_MIN).astype(I32)

    def bit_body(i, thr):
        cand = thr | lax.shift_left(jnp.int32(1), 30 - i)
        return jnp.where(count_ge(cand) >= kf, cand, thr)

    return lax.fori_loop(0, 31, bit_body, thr0)


def _strict_upper(n):
    return (lax.broadcasted_iota(I32, (n, n), 0) < lax.broadcasted_iota(I32, (n, n), 1)).astype(BF16)


def _dsa_prompt_kernel(q_ref, iq_ref, iw_ref, ikw_ref, k_ref, v1_ref, o_ref,
                       keys_ref, top2_ref, iqs_ref, qg_ref, d_ref, m_ref, acc_ref, *, tq, tk, topk):
    j = pl.program_id(1)
    nkt = ((j + 1) * tq + tk - 1) // tk
    gq = DSA_GROUP * tq
    rp = 16
    for h in range(IDX_HEADS):
        iqs_ref[h * tq:(h + 1) * tq, :] = iq_ref[0, :, h * LANE:(h + 1) * LANE]
    for c in range(DSA_KV_HEADS):
        for g in range(DSA_GROUP):
            hh = c * DSA_GROUP + g
            qg_ref[c, g * tq:(g + 1) * tq, :] = q_ref[0, :, hh * DSA_HEAD_DIM:(hh + 1) * DSA_HEAD_DIM]
    ws = iw_ref[0][:, IDX_DIM:IDX_DIM + IDX_HEADS] * (IDX_HEADS ** -0.5) * (IDX_DIM ** -0.5)

    def score_tile(kt, diagonal):
        ikt = ikw_ref[0, pl.ds(pl.multiple_of(kt * tk, tk), tk), :]
        d_ref[...] = _dot_nt(iqs_ref[...], ikt)
        for r in range(tq // rp):
            s = jnp.zeros((rp, tk), F32)
            for h in range(IDX_HEADS):
                dh = d_ref[h * tq + r * rp:h * tq + (r + 1) * rp, :]
                s = s + ws[r * rp:(r + 1) * rp, h:h + 1] * jnp.maximum(dh, 0.0)
            key = _sortable(s + 0.0)
            if diagonal:
                qpos = j * tq + r * rp + lax.broadcasted_iota(I32, (rp, tk), 0)
                kpos = kt * tk + lax.broadcasted_iota(I32, (rp, tk), 1)
                key = jnp.where(kpos <= qpos, key, INT_MIN)
            keys_ref[kt, r * rp:(r + 1) * rp, :] = key
            rows = slice(r * rp, (r + 1) * rp)
            m1, m2 = top2_ref[rows, 0:LANE], top2_ref[rows, LANE:2 * LANE]
            for lt in range(tk // LANE):
                x = key[:, lt * LANE:(lt + 1) * LANE]
                m2 = jnp.maximum(m2, jnp.minimum(m1, x))
                m1 = jnp.maximum(m1, x)
            top2_ref[rows, :] = jnp.concatenate([m1, m2], axis=1)

    def score_body(kt, carry):
        score_tile(kt, False)
        return carry

    top2_ref[...] = jnp.full(top2_ref.shape, INT_MIN, I32)
    lax.fori_loop(0, nkt - 1, score_body, 0)
    score_tile(nkt - 1, True)

    def count_ge(t):
        def body(kt, acc):
            return acc + _lane_fold(jnp.where(keys_ref[kt] >= t, 1.0, 0.0))
        return jnp.sum(lax.fori_loop(0, nkt, body, jnp.zeros((tq, LANE), F32)), axis=1, keepdims=True)

    kf = float(topk)
    lo0 = jnp.min(top2_ref[:, LANE:2 * LANE], axis=1, keepdims=True)
    hi0 = jnp.max(top2_ref[:, 0:LANE], axis=1, keepdims=True) + 1

    def search_done(lo, hi, c_lo):
        row_done = jnp.logical_or(c_lo == kf, hi == lo + 1)
        return jnp.min(jnp.where(row_done, 1.0, 0.0)) > 0.0

    def search_cond(state):
        return jnp.logical_not(state[4])

    def search_body(state):
        lo, hi, c_lo, c_hi, _ = state
        for _ in range(4):
            mid = lax.shift_right_arithmetic(lo, 1) + lax.shift_right_arithmetic(hi, 1) + (lo & hi & 1)
            c = count_ge(mid)
            ok = c >= kf
            lo, c_lo = jnp.where(ok, mid, lo), jnp.where(ok, c, c_lo)
            hi, c_hi = jnp.where(ok, hi, mid), jnp.where(ok, c_hi, c)
        return lo, hi, c_lo, c_hi, search_done(lo, hi, c_lo)

    c_lo0 = count_ge(lo0)
    thr, _, c_thr, c_above, _ = lax.while_loop(
        search_cond, search_body, (lo0, hi0, c_lo0, jnp.zeros((tq, 1), F32), search_done(lo0, hi0, c_lo0)))
    need = kf - c_above
    tie_row = jnp.logical_and(c_thr > kf, thr > INT_MIN)
    any_tie = jnp.max(jnp.where(tie_row, 1.0, 0.0)) > 0.0

    @pl.when(jnp.logical_not(any_tie))
    def _():
        thr_eff = jnp.maximum(thr, INT_MIN + 1)

        def body(kt, carry):
            keys_ref[kt] = _bias_bits(keys_ref[kt] >= thr_eff)
            return carry
        lax.fori_loop(0, nkt, body, 0)

    @pl.when(any_tie)
    def _():
        upper = _strict_upper(tk)

        def body(kt, seen):
            kk = keys_ref[kt]
            eq = kk == thr
            before = seen + _dot(jnp.where(eq, 1.0, 0.0).astype(BF16), upper)
            sel = jnp.logical_and(kk > INT_MIN, jnp.logical_or(kk > thr, jnp.logical_and(eq, before < need)))
            keys_ref[kt] = _bias_bits(sel)
            return seen + jnp.sum(jnp.where(eq, 1.0, 0.0), axis=1, keepdims=True)
        lax.fori_loop(0, nkt, body, jnp.zeros((tq, 1), F32))

    m_ref[...] = jnp.full(m_ref.shape, MASKED, F32)
    acc_ref[...] = jnp.zeros_like(acc_ref)
    c2 = (DSA_HEAD_DIM ** -0.5) * 1.4426950408889634
    dh = DSA_HEAD_DIM

    def attend_body(kt, carry):
        rows = pl.ds(pl.multiple_of(kt * tk, tk), tk)
        bias = lax.bitcast_convert_type(keys_ref[kt], F32)[None]
        for c in range(DSA_KV_HEADS):
            kc = k_ref[0, rows, c * dh:(c + 1) * dh]
            v1 = v1_ref[0, rows, c * 2 * dh:(c + 1) * 2 * dh]
            x = (_dot_nt(qg_ref[c], kc).reshape(DSA_GROUP, tq, tk) + bias).reshape(gq, tk)
            m_old = m_ref[c]
            m_new = jnp.maximum(m_old, jnp.max(x, axis=1, keepdims=True))
            p = jnp.exp2((x - jnp.concatenate([m_new] * (tk // LANE), axis=1)) * c2).astype(BF16)
            alpha = jnp.exp2((m_old - m_new) * c2)
            acc_ref[c] = jnp.concatenate([alpha, alpha], axis=1) * acc_ref[c] + _dot(p, v1)
            m_ref[c] = m_new
        return carry

    lax.fori_loop(0, nkt, attend_body, 0)
    for c in range(DSA_KV_HEADS):
        oc = acc_ref[c, :, 0:dh] / acc_ref[c, :, dh:2 * dh]
        for g in range(DSA_GROUP):
            hh = c * DSA_GROUP + g
            o_ref[0, :, hh * dh:(hh + 1) * dh] = oc[g * tq:(g + 1) * tq].astype(BF16)


def _dsa_prompt(hf, hb, *, tq=128, tk=1024):
    bsz, s, _ = hf.shape
    tk = min(tk, s)
    assert s % tq == 0 and s % tk == 0 and tk % tq == 0
    topk = min(TOPK_MAX, s // 4)
    assert topk <= 2 * LANE
    hq = DSA_HEADS * DSA_HEAD_DIM
    gq = DSA_GROUP * tq
    return pl.pallas_call(
        functools.partial(_dsa_prompt_kernel, tq=tq, tk=tk, topk=topk),
        grid=(bsz, s // tq),
        in_specs=[pl.BlockSpec((1, tq, hq), lambda b, j: (b, j, DSA_Q0 // hq)),
                  pl.BlockSpec((1, tq, IDX_HEADS * LANE), lambda b, j: (b, j, DSA_IQ0 // (IDX_HEADS * LANE))),
                  pl.BlockSpec((1, tq, LANE), lambda b, j: (b, j, DSA_IKW0 // LANE)),
                  pl.BlockSpec((1, s, LANE), lambda b, j: (b, 0, DSA_IKW0 // LANE)),
                  pl.BlockSpec((1, s, DSA_HKV), lambda b, j: (b, 0, DSA_K0 // DSA_HKV)),
                  pl.BlockSpec((1, s, 2 * DSA_HKV), lambda b, j: (b, 0, DSA_V1_0 // (2 * DSA_HKV)))],
        out_specs=pl.BlockSpec((1, tq, hq), lambda b, j: (b, j, 0)),
        out_shape=jax.ShapeDtypeStruct((bsz, s, hq), BF16),
        scratch_shapes=[pltpu.VMEM((s // tk, tq, tk), I32),
                        pltpu.VMEM((tq, 2 * LANE), I32),
                        pltpu.VMEM((IDX_HEADS * tq, LANE), BF16),
                        pltpu.VMEM((DSA_KV_HEADS, gq, DSA_HEAD_DIM), BF16),
                        pltpu.VMEM((IDX_HEADS * tq, tk), F32),
                        pltpu.VMEM((DSA_KV_HEADS, gq, LANE), F32),
                        pltpu.VMEM((DSA_KV_HEADS, gq, 2 * DSA_HEAD_DIM), F32)],
        compiler_params=_cparams(("arbitrary", "arbitrary"), 56),
        name="dsa_prompt",
    )(hb, hb, hf, hb, hb, hb)


def _sample_ws(iw_ref):
    return iw_ref[0][:, IDX_DIM:IDX_DIM + IDX_HEADS] * (IDX_HEADS ** -0.5)


def _sample_scores(d, ws, t):
    s = jnp.zeros((t, d.shape[1]), F32)
    for h in range(IDX_HEADS):
        s = s + ws[:, h:h + 1] * jnp.maximum(d[h * t:(h + 1) * t] * (IDX_DIM ** -0.5), 0.0)
    return s + 0.0


def _dsa_sample_scores_kernel(pt_ref, iq_ref, iw_ref, *refs, pp, t):
    page_refs, o_ref = refs[:pp], refs[pp]
    iq_stack = jnp.concatenate([iq_ref[0, :, h * LANE:h * LANE + IDX_DIM] for h in range(IDX_HEADS)], axis=0)
    ik_t = jnp.concatenate([r[0].astype(BF16) for r in page_refs], axis=1)
    o_ref[0] = _sample_scores(_dot(iq_stack.astype(BF16), ik_t), _sample_ws(iw_ref), t)


def _dsa_sample_scores(hf, cache_ik_t, page_table, *, pp):
    bsz, t, _ = hf.shape
    n_pages = page_table.shape[1]
    assert n_pages % pp == 0
    iq_w = IDX_HEADS * LANE

    def page_spec(i):
        return pl.BlockSpec((1, IDX_DIM, PAGE_SIZE), lambda b, j, pt: (pt[b, j * pp + i], 0, 0))

    grid_spec = pltpu.PrefetchScalarGridSpec(
        num_scalar_prefetch=1, grid=(bsz, n_pages // pp),
        in_specs=[pl.BlockSpec((1, t, iq_w), lambda b, j, pt: (b, 0, DSA_IQ0 // iq_w)),
                  pl.BlockSpec((1, t, LANE), lambda b, j, pt: (b, 0, DSA_IKW0 // LANE))]
                 + [page_spec(i) for i in range(pp)],
        out_specs=pl.BlockSpec((1, t, pp * PAGE_SIZE), lambda b, j, pt: (b, 0, j)))
    return pl.pallas_call(
        functools.partial(_dsa_sample_scores_kernel, pp=pp, t=t),
        grid_spec=grid_spec,
        out_shape=jax.ShapeDtypeStruct((bsz, t, n_pages * PAGE_SIZE), F32),
        compiler_params=_cparams(("arbitrary", "arbitrary"), 32),
        name="dsa_sample_scores",
    )(page_table, hf, hf, *([cache_ik_t] * pp))


def _dsa_sample_select_kernel(sp_ref, iq_ref, iw_ref, ikw_ref, o_ref, keys_ref, *, t, past, new_w, topk, ck):
    iq_stack = jnp.concatenate([iq_ref[0, :, h * LANE:(h + 1) * LANE] for h in range(IDX_HEADS)], axis=0)
    ik_new = jnp.concatenate([ikw_ref[0], jnp.zeros((new_w - t, LANE), F32)], axis=0)
    d_new = _dot_nt(iq_stack.astype(BF16), ik_new.astype(BF16))
    s_new = _sample_scores(d_new, _sample_ws(iw_ref), t)
    row = lax.broadcasted_iota(I32, (t, new_w), 0)
    lane = lax.broadcasted_iota(I32, (t, new_w), 1)
    s_new = jnp.where(lane <= row, s_new, -jnp.inf)
    keys_ref[:, 0:past] = _sortable(sp_ref[0])
    keys_ref[:, past:past + new_w] = _sortable(s_new)
    n_ck = (past + new_w) // ck

    def counts(pred):
        acc = jnp.zeros((t, LANE), F32)
        for i in range(n_ck):
            acc = acc + _lane_fold(jnp.where(pred(keys_ref[:, i * ck:(i + 1) * ck]), 1.0, 0.0))
        return jnp.sum(acc, axis=1, keepdims=True)

    thr = _kth_largest_key(lambda cand: counts(lambda kk: kk >= cand), t, topk)
    n_gt = counts(lambda kk: kk > thr)
    n_eq = counts(lambda kk: kk == thr)
    need = float(topk) - n_gt
    tie_row = jnp.logical_and(n_eq > need, thr > NEG_INF_KEY)
    any_tie = jnp.max(jnp.where(tie_row, 1.0, 0.0)) > 0.0

    @pl.when(jnp.logical_not(any_tie))
    def _():
        thr_eff = jnp.maximum(thr, NEG_INF_KEY + 1)
        o_ref[0] = jnp.where(keys_ref[...] >= thr_eff, 1.0, 0.0)

    @pl.when(any_tie)
    def _():
        upper = _strict_upper(ck)
        seen = jnp.zeros((t, 1), F32)
        for i in range(n_ck):
            kk = keys_ref[:, i * ck:(i + 1) * ck]
            eq = kk == thr
            before = seen + _dot(jnp.where(eq, 1.0, 0.0).astype(BF16), upper)
            sel = jnp.logical_and(kk > NEG_INF_KEY, jnp.logical_or(kk > thr, jnp.logical_and(eq, before < need)))
            o_ref[0, :, i * ck:(i + 1) * ck] = jnp.where(sel, 1.0, 0.0)
            seen = seen + jnp.sum(jnp.where(eq, 1.0, 0.0), axis=1, keepdims=True)


def _dsa_sample_select(scores_past, hf, *, new_w=512, ck=512):
    bsz, t, past = scores_past.shape
    assert past % ck == 0 and new_w % ck == 0
    topk = min(TOPK_MAX, (past + t) // 4)
    iq_w = IDX_HEADS * LANE
    total = past + new_w
    return pl.pallas_call(
        functools.partial(_dsa_sample_select_kernel, t=t, past=past, new_w=new_w, topk=topk, ck=ck),
        grid=(bsz,),
        in_specs=[pl.BlockSpec((1, t, past), lambda b: (b, 0, 0)),
                  pl.BlockSpec((1, t, iq_w), lambda b: (b, 0, DSA_IQ0 // iq_w)),
                  pl.BlockSpec((1, t, LANE), lambda b: (b, 0, DSA_IKW0 // LANE)),
                  pl.BlockSpec((1, t, LANE), lambda b: (b, 0, DSA_IKW0 // LANE))],
        out_specs=pl.BlockSpec((1, t, total), lambda b: (b, 0, 0)),
        out_shape=jax.ShapeDtypeStruct((bsz, t, total), F32),
        scratch_shapes=[pltpu.VMEM((t, total), I32)],
        compiler_params=_cparams(("arbitrary",), 32),
        name="dsa_sample_select",
    )(scores_past, hf, hf, hf)


def _dsa_sample_attend_kernel(pt_ref, q_ref, kn_ref, vn_ref, selp_ref, seln_ref, *refs, pp, t, new_w):
    k_pages, v_pages = refs[:pp], refs[pp:2 * pp]
    o_ref, m_ref, l_ref, acc_ref = refs[2 * pp:]
    j = pl.program_id(1)
    gt = DSA_GROUP * t

    @pl.when(j == 0)
    def _():
        m_ref[...] = jnp.full(m_ref.shape, MASKED, F32)
        l_ref[...] = jnp.zeros_like(l_ref)
        acc_ref[...] = jnp.zeros_like(acc_ref)

    q_groups = [jnp.concatenate([q_ref[0, :, (c * DSA_GROUP + g) * DSA_HEAD_DIM:(c * DSA_GROUP + g + 1) * DSA_HEAD_DIM]
                                 for g in range(DSA_GROUP)], axis=0).astype(BF16) for c in range(DSA_KV_HEADS)]

    def attend(kcs, vcs, sel):
        n = kcs[0].shape[0]
        for c in range(DSA_KV_HEADS):
            kc, vc = kcs[c], vcs[c]
            logits = (_dot_nt(q_groups[c], kc) * (DSA_HEAD_DIM ** -0.5)).reshape(DSA_GROUP, t, n)
            masked = jnp.where(sel[None], logits, MASKED).reshape(gt, n)
            m_old = m_ref[c]
            m_new = jnp.maximum(m_old, jnp.max(masked, axis=1, keepdims=True))
            p = jnp.where(sel[None], jnp.exp(masked - m_new).reshape(DSA_GROUP, t, n), 0.0).reshape(gt, n)
            alpha = jnp.exp(m_old - m_new)
            l_ref[c] = alpha * l_ref[c] + jnp.sum(p, axis=1, keepdims=True)
            acc_ref[c] = alpha * acc_ref[c] + _dot(p.astype(BF16), vc)
            m_ref[c] = m_new

    def head_rows(pages, c):
        return jnp.concatenate([r[0, pl.ds(c, PAGE_SIZE, stride=DSA_KV_HEADS), :].astype(BF16) for r in pages], axis=0)

    attend([head_rows(k_pages, c) for c in range(DSA_KV_HEADS)],
           [head_rows(v_pages, c) for c in range(DSA_KV_HEADS)], selp_ref[0] != 0.0)

    @pl.when(j == pl.num_programs(1) - 1)
    def _():
        pad = jnp.zeros((new_w - t, DSA_HEAD_DIM), F32)

        def new_rows(ref, c):
            return jnp.concatenate([ref[0, :, c * DSA_HEAD_DIM:(c + 1) * DSA_HEAD_DIM], pad], axis=0).astype(BF16)

        attend([new_rows(kn_ref, c) for c in range(DSA_KV_HEADS)],
               [new_rows(vn_ref, c) for c in range(DSA_KV_HEADS)], seln_ref[0] != 0.0)
        for c in range(DSA_KV_HEADS):
            oc = acc_ref[c] / l_ref[c]
            for g in range(DSA_GROUP):
                hh = c * DSA_GROUP + g
                o_ref[0, :, hh * DSA_HEAD_DIM:(hh + 1) * DSA_HEAD_DIM] = oc[g * t:(g + 1) * t].astype(BF16)


def _dsa_sample_attend(hf, sel, cache_k, cache_v, page_table, *, pp, new_w=512):
    bsz, t, _ = hf.shape
    n_pages = page_table.shape[1]
    past = n_pages * PAGE_SIZE
    assert n_pages % pp == 0 and past % new_w == 0
    hq = DSA_HEADS * DSA_HEAD_DIM
    gt = DSA_GROUP * t

    def page_spec(i):
        return pl.BlockSpec((1, PAGE_SIZE * DSA_KV_HEADS, DSA_HEAD_DIM), lambda b, j, pt: (pt[b, j * pp + i], 0, 0))

    grid_spec = pltpu.PrefetchScalarGridSpec(
        num_scalar_prefetch=1, grid=(bsz, n_pages // pp),
        in_specs=[pl.BlockSpec((1, t, hq), lambda b, j, pt: (b, 0, DSA_Q0 // hq)),
                  pl.BlockSpec((1, t, DSA_HKV), lambda b, j, pt: (b, 0, DSA_K0 // DSA_HKV)),
                  pl.BlockSpec((1, t, DSA_HKV), lambda b, j, pt: (b, 0, DSA_V0 // DSA_HKV)),
                  pl.BlockSpec((1, t, pp * PAGE_SIZE), lambda b, j, pt: (b, 0, j)),
                  pl.BlockSpec((1, t, new_w), lambda b, j, pt: (b, 0, past // new_w))]
                 + [page_spec(i) for i in range(pp)] * 2,
        out_specs=pl.BlockSpec((1, t, hq), lambda b, j, pt: (b, 0, 0)),
        scratch_shapes=[pltpu.VMEM((DSA_KV_HEADS, gt, 1), F32),
                        pltpu.VMEM((DSA_KV_HEADS, gt, 1), F32),
                        pltpu.VMEM((DSA_KV_HEADS, gt, DSA_HEAD_DIM), F32)])
    return pl.pallas_call(
        functools.partial(_dsa_sample_attend_kernel, pp=pp, t=t, new_w=new_w),
        grid_spec=grid_spec,
        out_shape=jax.ShapeDtypeStruct((bsz, t, hq), BF16),
        compiler_params=_cparams(("arbitrary", "arbitrary"), 48),
        name="dsa_sample_attend",
    )(page_table, hf, hf, hf, sel, sel, *([cache_k] * pp), *([cache_v] * pp))


def _router_kernel(x_ref, wt_ref, o_ref):
    x = x_ref[...]
    wt = wt_ref[...]
    xh = x.astype(BF16)
    xl = (x - xh.astype(F32)).astype(BF16)
    wh = wt.astype(BF16)
    wl = (wt - wh.astype(F32)).astype(BF16)
    logits = _dot_nt(wh, xh) + (_dot_nt(wl, xh) + _dot_nt(wh, xl))
    sub = lax.broadcasted_iota(I32, logits.shape, 0)
    l1 = jnp.max(logits, axis=0, keepdims=True)
    i1 = jnp.min(jnp.where(logits == l1, sub, N_EXPERTS), axis=0, keepdims=True)
    rest = jnp.where(sub == i1, -jnp.inf, logits)
    l2 = jnp.max(rest, axis=0, keepdims=True)
    i2 = jnp.min(jnp.where(rest == l2, sub, N_EXPERTS), axis=0, keepdims=True)
    e2 = jnp.exp(l2 - l1)
    denom = 1.0 + e2
    o_ref[...] = jnp.where(sub == i1, 1.0 / denom, jnp.where(sub == i2, e2 / denom, 0.0))


def _router(x, w_t):
    n, d = x.shape
    tm = _row_tile(n, 512)
    return pl.pallas_call(
        _router_kernel,
        grid=(n // tm,),
        in_specs=[pl.BlockSpec((tm, d), lambda i: (i, 0)), pl.BlockSpec((N_EXPERTS, d), lambda i: (0, 0))],
        out_specs=pl.BlockSpec((N_EXPERTS, tm), lambda i: (0, i)),
        out_shape=jax.ShapeDtypeStruct((N_EXPERTS, n), F32),
        compiler_params=_cparams(("arbitrary",), 32),
        name="router",
    )(x, w_t)


MOE_ROWS = 64


def _moe_kernel(cnt_ref, xb_ref, gt_ref, wg_ref, wu_ref, wd_ref, o_ref,
                rank_ref, xg_ref, acc_ref, upper_ref, *, ts, nsub, arms):
    i, e, f = pl.program_id(0), pl.program_id(1), pl.program_id(2)
    first_f = f == 0
    last_f = f == pl.num_programs(2) - 1

    @pl.when(jnp.logical_and(i == 0, jnp.logical_and(e == 0, first_f)))
    def _():
        upper_ref[...] = _strict_upper(ts)

    @pl.when(jnp.logical_and(e == 0, first_f))
    def _():
        o_ref[...] = jnp.zeros_like(o_ref)
        for sub in range(nsub):
            chosen = jnp.where(gt_ref[:, sub * ts:(sub + 1) * ts] > 0.0, 1.0, 0.0).astype(BF16)
            rank_ref[sub] = _dot(chosen, upper_ref[...])

    for sub in range(nsub):
        tok = slice(sub * ts, (sub + 1) * ts)
        nb = (cnt_ref[(i * nsub + sub) * N_EXPERTS + e] + MOE_ROWS - 1) // MOE_ROWS
        ge = gt_ref[pl.ds(e, 1), tok]
        slot = jnp.where(ge > 0.0, rank_ref[sub, pl.ds(e, 1), :], -1.0)

        def one_hot(start, rows, slot=slot):
            r = (start + lax.broadcasted_iota(I32, (rows, 1), 0)).astype(F32)
            return slot == r

        def for_arms(fn, nb=nb):
            for size in arms:
                start = pl.multiple_of((nb & ~(2 * size - 1)) * MOE_ROWS, MOE_ROWS)

                @pl.when((nb & size) != 0)
                def _():
                    fn(start, size * MOE_ROWS)

        @pl.when(first_f)
        def _():
            def gather(start, rows):
                hot = jnp.where(one_hot(start, rows), 1.0, 0.0).astype(BF16)
                xg_ref[sub, pl.ds(start, rows), :] = _dot(hot, xb_ref[tok, :]).astype(BF16)
                acc_ref[sub, pl.ds(start, rows), :] = jnp.zeros((rows, acc_ref.shape[2]), F32)
            for_arms(gather)

        def expert(start, rows):
            xs = xg_ref[sub, pl.ds(start, rows), :]
            hg = _dot(xs, wg_ref[0])
            hu = _dot(xs, wu_ref[0])
            h = (hg * _sigmoid(hg) * hu).astype(BF16)
            acc_ref[sub, pl.ds(start, rows), :] += _dot(h, wd_ref[0])
        for_arms(expert)

        @pl.when(last_f)
        def _():
            def scatter(start, rows):
                hot = one_hot(start, rows)
                gate = jnp.sum(jnp.where(hot, ge, 0.0), axis=1, keepdims=True)
                v = (gate * acc_ref[sub, pl.ds(start, rows), :]).astype(BF16)
                o_ref[tok, :] += _dot_tn(jnp.where(hot, 1.0, 0.0).astype(BF16), v)
            for_arms(scatter)


def _moe(xb, gates_t, wg, wu, wd, *, tf=896):
    n, d = xb.shape
    ff = wg.shape[2]
    assert ff % tf == 0
    ts = _row_tile(n, 1024)
    nsub = 2 if n % (2 * ts) == 0 else 1
    tm = ts * nsub
    assert ts % MOE_ROWS == 0
    max_blocks = ts // MOE_ROWS
    arms = tuple(s for s in (64, 32, 16, 8, 4, 2, 1) if s <= max_blocks)
    assert arms[0] == max_blocks
    counts = jnp.sum((gates_t > 0.0).reshape(N_EXPERTS, n // ts, ts), axis=2).astype(I32).T.reshape(-1)
    grid_spec = pltpu.PrefetchScalarGridSpec(
        num_scalar_prefetch=1, grid=(n // tm, N_EXPERTS, ff // tf),
        in_specs=[pl.BlockSpec((tm, d), lambda i, e, f, c: (i, 0)),
                  pl.BlockSpec((N_EXPERTS, tm), lambda i, e, f, c: (0, i)),
                  pl.BlockSpec((1, d, tf), lambda i, e, f, c: (e, 0, f)),
                  pl.BlockSpec((1, d, tf), lambda i, e, f, c: (e, 0, f)),
                  pl.BlockSpec((1, tf, d), lambda i, e, f, c: (e, f, 0))],
        out_specs=pl.BlockSpec((tm, d), lambda i, e, f, c: (i, 0)),
        scratch_shapes=[pltpu.VMEM((nsub, N_EXPERTS, ts), F32), pltpu.VMEM((nsub, ts, d), BF16),
                        pltpu.VMEM((nsub, ts, d), F32), pltpu.VMEM((ts, ts), BF16)])
    return pl.pallas_call(
        functools.partial(_moe_kernel, ts=ts, nsub=nsub, arms=arms),
        grid_spec=grid_spec,
        out_shape=jax.ShapeDtypeStruct((n, d), F32),
        compiler_params=_cparams(("arbitrary", "arbitrary", "arbitrary"), 60),
        name="moe",
    )(counts, xb, gates_t, wg, wu, wd)


def _pad_cols(w, cols):
    return jnp.pad(w, ((0, 0), (0, cols - w.shape[1])))


def _gla_layer(x, s0, w_in, w_up, b_a, norm_g, w_out, ln_g, ln_b, *, alpha):
    bsz, t, d = x.shape
    x2 = x.reshape(bsz * t, d)
    hf = _proj(x2, _pad_cols(w_in, GLA_COLS).astype(BF16), tn=640, bf16_copy=False)
    hf = hf.reshape(bsz, t, GLA_COLS)
    c = GLA_CHUNK if t % GLA_CHUNK == 0 else t
    valid = c
    if c < GLA_CHUNK:
        hf = jnp.pad(hf, ((0, 0), (0, GLA_CHUNK - c), (0, 0)))
        c = GLA_CHUNK
    w_up_pad = jnp.pad(w_up, ((0, LANE - GLA_GATE_RANK), (0, 0))).astype(BF16)
    o, st = _gla_core(hf, jnp.swapaxes(s0, 2, 3), w_up_pad, b_a.reshape(1, GLA_HK), norm_g.reshape(1, GLA_HV),
                      c=c, valid=valid)
    o = o[:, :t].reshape(bsz * t, GLA_HV)
    y, yb = _mm_res_ln(x2, o, w_out.astype(BF16), ln_g.reshape(1, d), ln_b.reshape(1, d), alpha=alpha)
    return y.reshape(bsz, t, d), yb, jnp.swapaxes(st, 2, 3)


def _dsa_weight(w_in):
    hq = DSA_HEADS * DSA_HEAD_DIM
    iq0 = hq + 2 * DSA_HKV
    ik0 = iq0 + IDX_HEADS * IDX_DIM
    w_iq = w_in[:, iq0:ik0].reshape(-1, IDX_HEADS, IDX_DIM)
    w_iq = jnp.pad(w_iq, ((0, 0), (0, 0), (0, LANE - IDX_DIM))).reshape(-1, IDX_HEADS * LANE)
    w_v = w_in[:, hq + DSA_HKV:iq0].reshape(-1, DSA_KV_HEADS, DSA_HEAD_DIM)
    w_v1 = jnp.pad(w_v, ((0, 0), (0, 0), (0, DSA_HEAD_DIM))).reshape(-1, 2 * DSA_HKV)
    w = jnp.concatenate([w_in[:, :hq], w_iq, w_in[:, hq:iq0], w_v1, _pad_cols(w_in[:, ik0:], LANE)], axis=1)
    return w.astype(BF16)


def _dsa_project(x, w_in):
    bsz, t, d = x.shape
    hf, hb = _proj(x.reshape(bsz * t, d), _dsa_weight(w_in), tn=DSA_COLS // 5, ones_at=DSA_ONES)
    hf = hf.reshape(bsz, t, DSA_COLS)
    k = hf[:, :, DSA_K0:DSA_V0].reshape(bsz, t, DSA_KV_HEADS, DSA_HEAD_DIM)
    v = hf[:, :, DSA_V0:DSA_V1_0].reshape(bsz, t, DSA_KV_HEADS, DSA_HEAD_DIM)
    ik = hf[:, :, DSA_IKW0:DSA_IKW0 + IDX_DIM]
    return hf, hb.reshape(bsz, t, DSA_COLS), k, v, ik


def kernel(x_prompt, x_sample, state_gla, cache_k, cache_v, cache_idx_k, page_table, p_prompt, p_sample,
           ln_mix_g, ln_mix_b, ln_ffn_g, ln_ffn_b, w_in_gla, w_alpha_up, b_alpha, gla_norm_g, w_out_gla,
           w_in_dsa, w_out_dsa, w_ffn_gate, w_ffn_up, w_ffn_down, w_router, w_exp_gate, w_exp_up,
           w_exp_down, w_ple_proj, w_ple_gate):
    depth = ln_mix_g.shape[0]
    alpha = (2.0 * depth) ** 0.25
    d = x_prompt.shape[-1]
    xp, xs = x_prompt, x_sample
    gla_p, gla_s, kp, vp, ikp, ksm, vsm, iksm = [], [], [], [], [], [], [], []
    n_pages = page_table.shape[1]
    pp = 32 if n_pages % 32 == 0 else n_pages
    for i in range(depth):
        j = i // 2
        lg, lb = ln_mix_g[i], ln_mix_b[i]
        if i % 2 == 0:
            gw = (w_in_gla[j], w_alpha_up[j], b_alpha[j], gla_norm_g[j], w_out_gla[j], lg, lb)
            s0 = jnp.zeros((xp.shape[0], GLA_HEADS, GLA_DK, GLA_DV), F32)
            xp, xpb, sp = _gla_layer(xp, s0, *gw, alpha=alpha)
            xs, xsb, ss = _gla_layer(xs, state_gla[j], *gw, alpha=alpha)
            gla_p.append(sp)
            gla_s.append(ss)
        else:
            w_out = w_out_dsa[j].astype(BF16)
            hf, hb, k_p, v_p, ik_p = _dsa_project(xp, w_in_dsa[j])
            op = _dsa_prompt(hf, hb)
            bp, tp, _ = xp.shape
            xp, xpb = _mm_res_ln(xp.reshape(bp * tp, d), op.reshape(bp * tp, -1), w_out, lg.reshape(1, d),
                                 lb.reshape(1, d), alpha=alpha)
            xp = xp.reshape(bp, tp, d)
            hf, hb, k_s, v_s, ik_s = _dsa_project(xs, w_in_dsa[j])
            sc = _dsa_sample_scores(hf, jnp.swapaxes(cache_idx_k[j], 1, 2), page_table, pp=pp)
            sel = _dsa_sample_select(sc, hf)
            n_pool = cache_k.shape[1]
            rows = PAGE_SIZE * DSA_KV_HEADS
            os_ = _dsa_sample_attend(hf, sel, cache_k[j].reshape(n_pool, rows, DSA_HEAD_DIM),
                                     cache_v[j].reshape(n_pool, rows, DSA_HEAD_DIM), page_table, pp=pp)
            bs, ts, _ = xs.shape
            xs, xsb = _mm_res_ln(xs.reshape(bs * ts, d), os_.reshape(bs * ts, -1), w_out, lg.reshape(1, d),
                                 lb.reshape(1, d), alpha=alpha)
            xs = xs.reshape(bs, ts, d)
            kp.append(k_p)
            vp.append(v_p)
            ikp.append(ik_p)
            ksm.append(k_s)
            vsm.append(v_s)
            iksm.append(ik_s)
        fg, fb = ln_ffn_g[i].reshape(1, d), ln_ffn_b[i].reshape(1, d)
        outs = []
        wpg, wpp = w_ple_gate[i].astype(BF16), w_ple_proj[i].astype(BF16)
        for x, xb, p in ((xp, xpb, p_prompt[i]), (xs, xsb, p_sample[i])):
            bsz, t, _ = x.shape
            x2 = x.reshape(bsz * t, d)
            p2 = p.reshape(bsz * t, -1)
            if i % 2 == 0:
                y2 = _ffn_ln(x2, w_ffn_gate[j].astype(BF16), w_ffn_up[j].astype(BF16), w_ffn_down[j].astype(BF16),
                             fg, fb, alpha=alpha)
                y2 = _ple(y2, p2, wpg, wpp)
            else:
                gates = _router(x2, w_router[j].T)
                f2 = _moe(xb, gates, w_exp_gate[j].astype(BF16), w_exp_up[j].astype(BF16),
                          w_exp_down[j].astype(BF16))
                y2 = _res_ln_ple(x2, f2, p2, fg, fb, wpg, wpp, alpha=alpha)
            outs.append(y2.reshape(bsz, t, d))
        xp, xs = outs
    return (xp, xs, jnp.stack(gla_p), jnp.stack(gla_s), jnp.stack(kp), jnp.stack(vp), jnp.stack(ikp),
            jnp.stack(ksm), jnp.stack(vsm), jnp.stack(iksm))
```

```python
import functools

import jax
import jax.numpy as jnp
from jax import lax
from jax.experimental import pallas as pl
from jax.experimental.pallas import tpu as pltpu

F32 = jnp.float32
BF16 = jnp.bfloat16
I32 = jnp.int32

D_MODEL = 1024
LN_EPS = 1e-5
LANE = 128

GLA_HEADS = 4
GLA_DK = 128
GLA_DV = 256
GLA_GATE_RANK = 16
GLA_TAU = 16.0
GLA_CHUNK = 64
GLA_HK = GLA_HEADS * GLA_DK
GLA_HV = GLA_HEADS * GLA_DV
GLA_COLS = 2 * GLA_HK + 2 * GLA_HV + LANE

DSA_HEADS = 8
DSA_HEAD_DIM = 128
DSA_KV_HEADS = 2
DSA_GROUP = DSA_HEADS // DSA_KV_HEADS
DSA_HKV = DSA_KV_HEADS * DSA_HEAD_DIM
IDX_HEADS = 8
IDX_DIM = 64
TOPK_MAX = 256
PAGE_SIZE = 128
DSA_Q0 = 0
DSA_IQ0 = DSA_HEADS * DSA_HEAD_DIM
DSA_K0 = DSA_IQ0 + IDX_HEADS * LANE
DSA_V0 = DSA_K0 + DSA_HKV
DSA_V1_0 = DSA_V0 + DSA_HKV
DSA_IKW0 = DSA_V1_0 + 2 * DSA_HKV
DSA_COLS = DSA_IKW0 + LANE
DSA_ONES = tuple(DSA_V1_0 + (2 * c + 1) * DSA_HEAD_DIM for c in range(DSA_KV_HEADS))

N_EXPERTS = 8

NEG_INF_KEY = -2139095041
INT_MIN = -2147483648
MASKED = -1e30


def _cparams(sem, vmem_mb):
    return pltpu.CompilerParams(dimension_semantics=sem, vmem_limit_bytes=vmem_mb * 2 ** 20)


def _dot(a, b):
    return jnp.dot(a, b, preferred_element_type=F32)


def _dot_nt(a, b):
    return lax.dot_general(a, b, (((1,), (1,)), ((), ())), preferred_element_type=F32)


def _dot_tn(a, b):
    return lax.dot_general(a, b, (((0,), (0,)), ((), ())), preferred_element_type=F32)


def _sigmoid(x):
    return 1.0 / (1.0 + jnp.exp(-x))


def _layer_norm(y, g, b):
    mu = jnp.mean(y, axis=-1, keepdims=True)
    yc = y - mu
    var = jnp.mean(yc * yc, axis=-1, keepdims=True)
    return yc * lax.rsqrt(var + LN_EPS) * g + b


def _row_tile(n, pref):
    t = min(n, pref)
    assert n % t == 0
    return t


def _proj_kernel(x_ref, w_ref, o_ref, ob_ref, *, tn, ones_at):
    xb = x_ref[...].astype(BF16)
    for c in range(w_ref.shape[1] // tn):
        r = _dot(xb, w_ref[:, c * tn:(c + 1) * tn])
        o_ref[:, c * tn:(c + 1) * tn] = r
        if ob_ref is not None:
            ob_ref[:, c * tn:(c + 1) * tn] = r.astype(BF16)
    for c0 in ones_at:
        ob_ref[:, c0:c0 + LANE] = jnp.ones((x_ref.shape[0], LANE), BF16)


def _proj_f32_kernel(x_ref, w_ref, o_ref, *, tn):
    _proj_kernel(x_ref, w_ref, o_ref, None, tn=tn, ones_at=())


def _proj(x, w, *, tn, ones_at=(), bf16_copy=True):
    n, d = x.shape
    cols = w.shape[1]
    assert cols % tn == 0 and (bf16_copy or not ones_at)
    tm = _row_tile(n, 512)
    out_spec = pl.BlockSpec((tm, cols), lambda i: (i, 0))
    if bf16_copy:
        body = functools.partial(_proj_kernel, tn=tn, ones_at=ones_at)
        out_specs = [out_spec, out_spec]
        out_shape = [jax.ShapeDtypeStruct((n, cols), F32), jax.ShapeDtypeStruct((n, cols), BF16)]
    else:
        body = functools.partial(_proj_f32_kernel, tn=tn)
        out_specs, out_shape = out_spec, jax.ShapeDtypeStruct((n, cols), F32)
    return pl.pallas_call(
        body,
        grid=(n // tm,),
        in_specs=[pl.BlockSpec((tm, d), lambda i: (i, 0)), pl.BlockSpec((d, cols), lambda i: (0, 0))],
        out_specs=out_specs,
        out_shape=out_shape,
        compiler_params=_cparams(("arbitrary",), 56),
        name="proj",
    )(x, w)


def _mm_res_ln_kernel(x_ref, a_ref, w_ref, g_ref, b_ref, o_ref, ob_ref, *, alpha):
    m = _dot(a_ref[...].astype(BF16), w_ref[...])
    y = _layer_norm(alpha * x_ref[...] + m, g_ref[...], b_ref[...])
    o_ref[...] = y
    ob_ref[...] = y.astype(BF16)


def _mm_res_ln(x, a, w, g, b, *, alpha):
    n, d = x.shape
    ka = a.shape[1]
    tm = _row_tile(n, 512)
    row = lambda i: (i, 0)
    fixed = lambda i: (0, 0)
    return pl.pallas_call(
        functools.partial(_mm_res_ln_kernel, alpha=alpha),
        grid=(n // tm,),
        in_specs=[pl.BlockSpec((tm, d), row), pl.BlockSpec((tm, ka), row), pl.BlockSpec((ka, d), fixed),
                  pl.BlockSpec((1, d), fixed), pl.BlockSpec((1, d), fixed)],
        out_specs=[pl.BlockSpec((tm, d), row), pl.BlockSpec((tm, d), row)],
        out_shape=[jax.ShapeDtypeStruct((n, d), F32), jax.ShapeDtypeStruct((n, d), BF16)],
        compiler_params=_cparams(("arbitrary",), 40),
        name="mm_res_ln",
    )(x, a, w, g, b)


def _ffn_ln_kernel(x_ref, wg_ref, wu_ref, wd_ref, g_ref, b_ref, o_ref, xb_ref, acc_ref, *, alpha):
    f = pl.program_id(1)

    @pl.when(f == 0)
    def _():
        xb_ref[...] = x_ref[...].astype(BF16)
        acc_ref[...] = jnp.zeros_like(acc_ref)

    xb = xb_ref[...]
    hg = _dot(xb, wg_ref[...])
    hu = _dot(xb, wu_ref[...])
    h = (hg * _sigmoid(hg) * hu).astype(BF16)
    acc_ref[...] += _dot(h, wd_ref[...])

    @pl.when(f == pl.num_programs(1) - 1)
    def _():
        o_ref[...] = _layer_norm(alpha * x_ref[...] + acc_ref[...], g_ref[...], b_ref[...])


def _ffn_ln(x, wg, wu, wd, g, b, *, alpha, tf=512):
    n, d = x.shape
    ff = wg.shape[1]
    assert ff % tf == 0
    tm = _row_tile(n, 1024)
    return pl.pallas_call(
        functools.partial(_ffn_ln_kernel, alpha=alpha),
        grid=(n // tm, ff // tf),
        in_specs=[pl.BlockSpec((tm, d), lambda i, f: (i, 0)),
                  pl.BlockSpec((d, tf), lambda i, f: (0, f)),
                  pl.BlockSpec((d, tf), lambda i, f: (0, f)),
                  pl.BlockSpec((tf, d), lambda i, f: (f, 0)),
                  pl.BlockSpec((1, d), lambda i, f: (0, 0)),
                  pl.BlockSpec((1, d), lambda i, f: (0, 0))],
        out_specs=pl.BlockSpec((tm, d), lambda i, f: (i, 0)),
        out_shape=jax.ShapeDtypeStruct((n, d), F32),
        scratch_shapes=[pltpu.VMEM((tm, d), BF16), pltpu.VMEM((tm, d), F32)],
        compiler_params=_cparams(("arbitrary", "arbitrary"), 48),
        name="ffn_ln",
    )(x, wg, wu, wd, g, b)


def _ple_kernel(x_ref, p_ref, wg_ref, wp_ref, o_ref):
    x = x_ref[...]
    gate = _sigmoid(_dot(x.astype(BF16), wg_ref[...]))
    o_ref[...] = x + gate * _dot(p_ref[...].astype(BF16), wp_ref[...])


def _ple(x, p, wg, wp):
    n, d = x.shape
    pd = p.shape[1]
    tm = _row_tile(n, 512)
    row = lambda i: (i, 0)
    fixed = lambda i: (0, 0)
    return pl.pallas_call(
        _ple_kernel,
        grid=(n // tm,),
        in_specs=[pl.BlockSpec((tm, d), row), pl.BlockSpec((tm, pd), row),
                  pl.BlockSpec((d, d), fixed), pl.BlockSpec((pd, d), fixed)],
        out_specs=pl.BlockSpec((tm, d), row),
        out_shape=jax.ShapeDtypeStruct((n, d), F32),
        compiler_params=_cparams(("arbitrary",), 40),
        name="ple",
    )(x, p, wg, wp)


def _res_ln_ple_kernel(x_ref, f_ref, p_ref, g_ref, b_ref, wg_ref, wp_ref, o_ref, *, alpha):
    z = _layer_norm(alpha * x_ref[...] + f_ref[...], g_ref[...], b_ref[...])
    gate = _sigmoid(_dot(z.astype(BF16), wg_ref[...]))
    o_ref[...] = z + gate * _dot(p_ref[...].astype(BF16), wp_ref[...])


def _res_ln_ple(x, f, p, g, b, wg, wp, *, alpha):
    n, d = x.shape
    pd = p.shape[1]
    tm = _row_tile(n, 512)
    row = lambda i: (i, 0)
    fixed = lambda i: (0, 0)
    return pl.pallas_call(
        functools.partial(_res_ln_ple_kernel, alpha=alpha),
        grid=(n // tm,),
        in_specs=[pl.BlockSpec((tm, d), row), pl.BlockSpec((tm, d), row), pl.BlockSpec((tm, pd), row),
                  pl.BlockSpec((1, d), fixed), pl.BlockSpec((1, d), fixed),
                  pl.BlockSpec((d, d), fixed), pl.BlockSpec((pd, d), fixed)],
        out_specs=pl.BlockSpec((tm, d), row),
        out_shape=jax.ShapeDtypeStruct((n, d), F32),
        compiler_params=_cparams(("arbitrary",), 40),
        name="res_ln_ple",
    )(x, f, p, g, b, wg, wp)


def _gla_kernel(q_ref, k_ref, v_ref, g_ref, a_ref, wup_ref, ba_ref, ng_ref, s0_ref, o_ref, st_out_ref, st_ref,
                *, c, valid, nbs):
    t = pl.program_id(1)

    @pl.when(t == 0)
    def _():
        st_ref[...] = s0_ref[...]

    row = lax.broadcasted_iota(I32, (c, c), 0)
    col = lax.broadcasted_iota(I32, (c, c), 1)
    causal = row >= col
    for bi in range(nbs):
        logit = _dot(a_ref[bi].astype(BF16), wup_ref[...]) + ba_ref[...]
        log_a = (jnp.minimum(logit, 0.0) - jnp.log(1.0 + jnp.exp(-jnp.abs(logit)))) * (1.0 / GLA_TAU)
        if valid < c:
            log_a = jnp.where(lax.broadcasted_iota(I32, log_a.shape, 0) < valid, log_a, 0.0)
        bcum = jnp.dot(causal.astype(F32), log_a, preferred_element_type=F32, precision=lax.Precision.HIGHEST)
        blast = bcum[c - 1:c, :]
        q = q_ref[bi] * (GLA_DK ** -0.5)
        k = k_ref[bi]
        q_dec = (q * jnp.exp(bcum)).astype(BF16)
        k_inv = (k * jnp.exp(-bcum)).astype(BF16)
        k_dec = (k * jnp.exp(blast - bcum)).astype(BF16)
        decay = jnp.exp(blast)
        v = v_ref[bi].astype(BF16)
        for h in range(GLA_HEADS):
            sk = slice(h * GLA_DK, (h + 1) * GLA_DK)
            sv = slice(h * GLA_DV, (h + 1) * GLA_DV)
            att = jnp.where(causal, _dot_nt(q_dec[:, sk], k_inv[:, sk]), 0.0).astype(BF16)
            st = st_ref[bi, h]
            o = _dot(att, v[:, sv]) + _dot_nt(q_dec[:, sk], st.astype(BF16))
            st_ref[bi, h] = st * decay[:, sk] + _dot_tn(v[:, sv], k_dec[:, sk])
            mu = jnp.mean(o, axis=-1, keepdims=True)
            oc = o - mu
            var = jnp.mean(oc * oc, axis=-1, keepdims=True)
            on = oc * lax.rsqrt(var + LN_EPS) * ng_ref[:, sv]
            gg = g_ref[bi, :, sv]
            o_ref[bi, :, sv] = (on * (gg * _sigmoid(gg))).astype(BF16)

    @pl.when(t == pl.num_programs(1) - 1)
    def _():
        st_out_ref[...] = st_ref[...]


def _gla_core(h, s0_t, wup, ba, ng, *, c, valid):
    bsz, tlen, _ = h.shape
    assert tlen % c == 0
    nbs = 4 if bsz % 4 == 0 else 1
    v_blk = (2 * GLA_HK) // GLA_HV
    state_spec = pl.BlockSpec((nbs, GLA_HEADS, GLA_DV, GLA_DK), lambda b, t: (b, 0, 0, 0))
    return pl.pallas_call(
        functools.partial(_gla_kernel, c=c, valid=valid, nbs=nbs),
        grid=(bsz // nbs, tlen // c),
        in_specs=[pl.BlockSpec((nbs, c, GLA_HK), lambda b, t: (b, t, 0)),
                  pl.BlockSpec((nbs, c, GLA_HK), lambda b, t: (b, t, 1)),
                  pl.BlockSpec((nbs, c, GLA_HV), lambda b, t: (b, t, v_blk)),
                  pl.BlockSpec((nbs, c, GLA_HV), lambda b, t: (b, t, v_blk + 1)),
                  pl.BlockSpec((nbs, c, LANE), lambda b, t: (b, t, (2 * GLA_HK + 2 * GLA_HV) // LANE)),
                  pl.BlockSpec((LANE, GLA_HK), lambda b, t: (0, 0)),
                  pl.BlockSpec((1, GLA_HK), lambda b, t: (0, 0)),
                  pl.BlockSpec((1, GLA_HV), lambda b, t: (0, 0)),
                  state_spec],
        out_specs=[pl.BlockSpec((nbs, c, GLA_HV), lambda b, t: (b, t, 0)), state_spec],
        out_shape=[jax.ShapeDtypeStruct((bsz, tlen, GLA_HV), BF16),
                   jax.ShapeDtypeStruct((bsz, GLA_HEADS, GLA_DV, GLA_DK), F32)],
        scratch_shapes=[pltpu.VMEM((nbs, GLA_HEADS, GLA_DV, GLA_DK), F32)],
        compiler_params=_cparams(("arbitrary", "arbitrary"), 32),
        name="gla_core",
    )(h, h, h, h, h, wup, ba, ng, s0_t)


def _sortable(s):
    bits = lax.bitcast_convert_type(s, I32)
    return bits ^ (lax.shift_right_arithmetic(bits, 31) & 0x7FFFFFFF)


def _lane_fold(m):
    acc = m[:, 0:LANE]
    for i in range(1, m.shape[1] // LANE):
        acc = acc + m[:, i * LANE:(i + 1) * LANE]
    return acc


def _lane_fold_max(m):
    acc = m[:, 0:LANE]
    for i in range(1, m.shape[1] // LANE):
        acc = jnp.maximum(acc, m[:, i * LANE:(i + 1) * LANE])
    return acc


def _bias_bits(sel):
    return lax.bitcast_convert_type(jnp.where(sel, 0.0, MASKED).astype(F32), I32)


def _kth_largest_key(count_ge, rows, k):
    kf = float(k)
    thr0 = jnp.where(count_ge(jnp.zeros((rows, 1), I32)) >= kf, 0, INT_MIN).astype(I32)

    def bit_body(i, thr):
        cand = thr | lax.shift_left(jnp.int32(1), 30 - i)
        return jnp.where(count_ge(cand) >= kf, cand, thr)

    return lax.fori_loop(0, 31, bit_body, thr0)


def _strict_upper(n):
    return (lax.broadcasted_iota(I32, (n, n), 0) < lax.broadcasted_iota(I32, (n, n), 1)).astype(BF16)


def _dsa_prompt_kernel(q_ref, iq_ref, iw_ref, ikw_ref, k_ref, v1_ref, o_ref,
                       keys_ref, top2_ref, iqs_ref, qg_ref, d_ref, m_ref, acc_ref, *, tq, tk, topk):
    j = pl.program_id(1)
    nkt = ((j + 1) * tq + tk - 1) // tk
    gq = DSA_GROUP * tq
    rp = 16
    for h in range(IDX_HEADS):
        iqs_ref[h * tq:(h + 1) * tq, :] = iq_ref[0, :, h * LANE:(h + 1) * LANE]
    for c in range(DSA_KV_HEADS):
        for g in range(DSA_GROUP):
            hh = c * DSA_GROUP + g
            qg_ref[c, g * tq:(g + 1) * tq, :] = q_ref[0, :, hh * DSA_HEAD_DIM:(hh + 1) * DSA_HEAD_DIM]
    ws = iw_ref[0][:, IDX_DIM:IDX_DIM + IDX_HEADS] * (IDX_HEADS ** -0.5) * (IDX_DIM ** -0.5)

    def score_tile(kt, diagonal):
        ikt = ikw_ref[0, pl.ds(pl.multiple_of(kt * tk, tk), tk), :]
        d_ref[...] = _dot_nt(iqs_ref[...], ikt)
        for r in range(tq // rp):
            s = jnp.zeros((rp, tk), F32)
            for h in range(IDX_HEADS):
                dh = d_ref[h * tq + r * rp:h * tq + (r + 1) * rp, :]
                s = s + ws[r * rp:(r + 1) * rp, h:h + 1] * jnp.maximum(dh, 0.0)
            key = _sortable(s + 0.0)
            if diagonal:
                qpos = j * tq + r * rp + lax.broadcasted_iota(I32, (rp, tk), 0)
                kpos = kt * tk + lax.broadcasted_iota(I32, (rp, tk), 1)
                key = jnp.where(kpos <= qpos, key, INT_MIN)
            keys_ref[kt, r * rp:(r + 1) * rp, :] = key
            rows = slice(r * rp, (r + 1) * rp)
            m1, m2 = top2_ref[rows, 0:LANE], top2_ref[rows, LANE:2 * LANE]
            for lt in range(tk // LANE):
                x = key[:, lt * LANE:(lt + 1) * LANE]
                m2 = jnp.maximum(m2, jnp.minimum(m1, x))
                m1 = jnp.maximum(m1, x)
            top2_ref[rows, :] = jnp.concatenate([m1, m2], axis=1)

    def score_body(kt, carry):
        score_tile(kt, False)
        return carry

    top2_ref[...] = jnp.full(top2_ref.shape, INT_MIN, I32)
    lax.fori_loop(0, nkt - 1, score_body, 0)
    score_tile(nkt - 1, True)

    def count_ge(t):
        def body(kt, acc):
            return acc + _lane_fold(jnp.where(keys_ref[kt] >= t, 1.0, 0.0))
        return jnp.sum(lax.fori_loop(0, nkt, body, jnp.zeros((tq, LANE), F32)), axis=1, keepdims=True)

    kf = float(topk)
    lo0 = jnp.min(top2_ref[:, LANE:2 * LANE], axis=1, keepdims=True)
    hi0 = jnp.max(top2_ref[:, 0:LANE], axis=1, keepdims=True) + 1

    def search_done(lo, hi, c_lo):
        row_done = jnp.logical_or(c_lo == kf, hi == lo + 1)
        return jnp.min(jnp.where(row_done, 1.0, 0.0)) > 0.0

    def search_cond(state):
        return jnp.logical_not(state[4])

    def search_body(state):
        lo, hi, c_lo, c_hi, _ = state
        for _ in range(4):
            mid = lax.shift_right_arithmetic(lo, 1) + lax.shift_right_arithmetic(hi, 1) + (lo & hi & 1)
            c = count_ge(mid)
            ok = c >= kf
            lo, c_lo = jnp.where(ok, mid, lo), jnp.where(ok, c, c_lo)
            hi, c_hi = jnp.where(ok, hi, mid), jnp.where(ok, c_hi, c)
        return lo, hi, c_lo, c_hi, search_done(lo, hi, c_lo)

    c_lo0 = count_ge(lo0)
    thr, _, c_thr, c_above, _ = lax.while_loop(
        search_cond, search_body, (lo0, hi0, c_lo0, jnp.zeros((tq, 1), F32), search_done(lo0, hi0, c_lo0)))
    need = kf - c_above
    tie_row = jnp.logical_and(c_thr > kf, thr > INT_MIN)
    any_tie = jnp.max(jnp.where(tie_row, 1.0, 0.0)) > 0.0

    @pl.when(jnp.logical_not(any_tie))
    def _():
        thr_eff = jnp.maximum(thr, INT_MIN + 1)

        def body(kt, carry):
            keys_ref[kt] = _bias_bits(keys_ref[kt] >= thr_eff)
            return carry
        lax.fori_loop(0, nkt, body, 0)

    @pl.when(any_tie)
    def _():
        upper = _strict_upper(tk)

        def body(kt, seen):
            kk = keys_ref[kt]
            eq = kk == thr
            before = seen + _dot(jnp.where(eq, 1.0, 0.0).astype(BF16), upper)
            sel = jnp.logical_and(kk > INT_MIN, jnp.logical_or(kk > thr, jnp.logical_and(eq, before < need)))
            keys_ref[kt] = _bias_bits(sel)
            return seen + jnp.sum(jnp.where(eq, 1.0, 0.0), axis=1, keepdims=True)
        lax.fori_loop(0, nkt, body, jnp.zeros((tq, 1), F32))

    m_ref[...] = jnp.full(m_ref.shape, MASKED, F32)
    acc_ref[...] = jnp.zeros_like(acc_ref)
    c2 = (DSA_HEAD_DIM ** -0.5) * 1.4426950408889634
    dh = DSA_HEAD_DIM

    def attend_body(kt, carry):
        rows = pl.ds(pl.multiple_of(kt * tk, tk), tk)
        bias = lax.bitcast_convert_type(keys_ref[kt], F32)[None]
        for c in range(DSA_KV_HEADS):
            kc = k_ref[0, rows, c * dh:(c + 1) * dh]
            v1 = v1_ref[0, rows, c * 2 * dh:(c + 1) * 2 * dh]
            x = (_dot_nt(qg_ref[c], kc).reshape(DSA_GROUP, tq, tk) + bias).reshape(gq, tk)
            m_old = m_ref[c]
            m_new = jnp.maximum(m_old, jnp.max(x, axis=1, keepdims=True))
            p = jnp.exp2((x - jnp.concatenate([m_new] * (tk // LANE), axis=1)) * c2).astype(BF16)
            alpha = jnp.exp2((m_old - m_new) * c2)
            acc_ref[c] = jnp.concatenate([alpha, alpha], axis=1) * acc_ref[c] + _dot(p, v1)
            m_ref[c] = m_new
        return carry

    lax.fori_loop(0, nkt, attend_body, 0)
    for c in range(DSA_KV_HEADS):
        oc = acc_ref[c, :, 0:dh] / acc_ref[c, :, dh:2 * dh]
        for g in range(DSA_GROUP):
            hh = c * DSA_GROUP + g
            o_ref[0, :, hh * dh:(hh + 1) * dh] = oc[g * tq:(g + 1) * tq].astype(BF16)


def _dsa_prompt(hf, hb, *, tq=128, tk=1024):
    bsz, s, _ = hf.shape
    tk = min(tk, s)
    assert s % tq == 0 and s % tk == 0 and tk % tq == 0
    topk = min(TOPK_MAX, s // 4)
    assert topk <= 2 * LANE
    hq = DSA_HEADS * DSA_HEAD_DIM
    gq = DSA_GROUP * tq
    return pl.pallas_call(
        functools.partial(_dsa_prompt_kernel, tq=tq, tk=tk, topk=topk),
        grid=(bsz, s // tq),
        in_specs=[pl.BlockSpec((1, tq, hq), lambda b, j: (b, j, DSA_Q0 // hq)),
                  pl.BlockSpec((1, tq, IDX_HEADS * LANE), lambda b, j: (b, j, DSA_IQ0 // (IDX_HEADS * LANE))),
                  pl.BlockSpec((1, tq, LANE), lambda b, j: (b, j, DSA_IKW0 // LANE)),
                  pl.BlockSpec((1, s, LANE), lambda b, j: (b, 0, DSA_IKW0 // LANE)),
                  pl.BlockSpec((1, s, DSA_HKV), lambda b, j: (b, 0, DSA_K0 // DSA_HKV)),
                  pl.BlockSpec((1, s, 2 * DSA_HKV), lambda b, j: (b, 0, DSA_V1_0 // (2 * DSA_HKV)))],
        out_specs=pl.BlockSpec((1, tq, hq), lambda b, j: (b, j, 0)),
        out_shape=jax.ShapeDtypeStruct((bsz, s, hq), BF16),
        scratch_shapes=[pltpu.VMEM((s // tk, tq, tk), I32),
                        pltpu.VMEM((tq, 2 * LANE), I32),
                        pltpu.VMEM((IDX_HEADS * tq, LANE), BF16),
                        pltpu.VMEM((DSA_KV_HEADS, gq, DSA_HEAD_DIM), BF16),
                        pltpu.VMEM((IDX_HEADS * tq, tk), F32),
                        pltpu.VMEM((DSA_KV_HEADS, gq, LANE), F32),
                        pltpu.VMEM((DSA_KV_HEADS, gq, 2 * DSA_HEAD_DIM), F32)],
        compiler_params=_cparams(("arbitrary", "arbitrary"), 56),
        name="dsa_prompt",
    )(hb, hb, hf, hb, hb, hb)


def _sample_ws(iw_ref):
    return iw_ref[0][:, IDX_DIM:IDX_DIM + IDX_HEADS] * (IDX_HEADS ** -0.5)


def _sample_scores(d, ws, t):
    s = jnp.zeros((t, d.shape[1]), F32)
    for h in range(IDX_HEADS):
        s = s + ws[:, h:h + 1] * jnp.maximum(d[h * t:(h + 1) * t] * (IDX_DIM ** -0.5), 0.0)
    return s + 0.0


def _dsa_sample_scores_kernel(pt_ref, iq_ref, iw_ref, *refs, pp, t):
    page_refs, o_ref = refs[:pp], refs[pp]
    iq_stack = jnp.concatenate([iq_ref[0, :, h * LANE:h * LANE + IDX_DIM] for h in range(IDX_HEADS)], axis=0)
    ik_t = jnp.concatenate([r[0].astype(BF16) for r in page_refs], axis=1)
    o_ref[0] = _sample_scores(_dot(iq_stack.astype(BF16), ik_t), _sample_ws(iw_ref), t)


def _dsa_sample_scores(hf, cache_ik_t, page_table, *, pp):
    bsz, t, _ = hf.shape
    n_pages = page_table.shape[1]
    assert n_pages % pp == 0
    iq_w = IDX_HEADS * LANE

    def page_spec(i):
        return pl.BlockSpec((1, IDX_DIM, PAGE_SIZE), lambda b, j, pt: (pt[b, j * pp + i], 0, 0))

    grid_spec = pltpu.PrefetchScalarGridSpec(
        num_scalar_prefetch=1, grid=(bsz, n_pages // pp),
        in_specs=[pl.BlockSpec((1, t, iq_w), lambda b, j, pt: (b, 0, DSA_IQ0 // iq_w)),
                  pl.BlockSpec((1, t, LANE), lambda b, j, pt: (b, 0, DSA_IKW0 // LANE))]
                 + [page_spec(i) for i in range(pp)],
        out_specs=pl.BlockSpec((1, t, pp * PAGE_SIZE), lambda b, j, pt: (b, 0, j)))
    return pl.pallas_call(
        functools.partial(_dsa_sample_scores_kernel, pp=pp, t=t),
        grid_spec=grid_spec,
        out_shape=jax.ShapeDtypeStruct((bsz, t, n_pages * PAGE_SIZE), F32),
        compiler_params=_cparams(("arbitrary", "arbitrary"), 32),
        name="dsa_sample_scores",
    )(page_table, hf, hf, *([cache_ik_t] * pp))


def _dsa_sample_select_kernel(sp_ref, iq_ref, iw_ref, ikw_ref, o_ref, keys_ref, *, t, past, new_w, topk, ck):
    iq_stack = jnp.concatenate([iq_ref[0, :, h * LANE:(h + 1) * LANE] for h in range(IDX_HEADS)], axis=0)
    ik_new = jnp.concatenate([ikw_ref[0], jnp.zeros((new_w - t, LANE), F32)], axis=0)
    d_new = _dot_nt(iq_stack.astype(BF16), ik_new.astype(BF16))
    s_new = _sample_scores(d_new, _sample_ws(iw_ref), t)
    row = lax.broadcasted_iota(I32, (t, new_w), 0)
    lane = lax.broadcasted_iota(I32, (t, new_w), 1)
    s_new = jnp.where(lane <= row, s_new, -jnp.inf)
    keys_ref[:, 0:past] = _sortable(sp_ref[0])
    keys_ref[:, past:past + new_w] = _sortable(s_new)
    n_ck = (past + new_w) // ck

    def counts(pred):
        acc = jnp.zeros((t, LANE), F32)
        for i in range(n_ck):
            acc = acc + _lane_fold(jnp.where(pred(keys_ref[:, i * ck:(i + 1) * ck]), 1.0, 0.0))
        return jnp.sum(acc, axis=1, keepdims=True)

    thr = _kth_largest_key(lambda cand: counts(lambda kk: kk >= cand), t, topk)
    n_gt = counts(lambda kk: kk > thr)
    n_eq = counts(lambda kk: kk == thr)
    need = float(topk) - n_gt
    tie_row = jnp.logical_and(n_eq > need, thr > NEG_INF_KEY)
    any_tie = jnp.max(jnp.where(tie_row, 1.0, 0.0)) > 0.0

    @pl.when(jnp.logical_not(any_tie))
    def _():
        thr_eff = jnp.maximum(thr, NEG_INF_KEY + 1)
        o_ref[0] = jnp.where(keys_ref[...] >= thr_eff, 1.0, 0.0)

    @pl.when(any_tie)
    def _():
        upper = _strict_upper(ck)
        seen = jnp.zeros((t, 1), F32)
        for i in range(n_ck):
            kk = keys_ref[:, i * ck:(i + 1) * ck]
            eq = kk == thr
            before = seen + _dot(jnp.where(eq, 1.0, 0.0).astype(BF16), upper)
            sel = jnp.logical_and(kk > NEG_INF_KEY, jnp.logical_or(kk > thr, jnp.logical_and(eq, before < need)))
            o_ref[0, :, i * ck:(i + 1) * ck] = jnp.where(sel, 1.0, 0.0)
            seen = seen + jnp.sum(jnp.where(eq, 1.0, 0.0), axis=1, keepdims=True)


def _dsa_sample_select(scores_past, hf, *, new_w=512, ck=512):
    bsz, t, past = scores_past.shape
    assert past % ck == 0 and new_w % ck == 0
    topk = min(TOPK_MAX, (past + t) // 4)
    iq_w = IDX_HEADS * LANE
    total = past + new_w
    return pl.pallas_call(
        functools.partial(_dsa_sample_select_kernel, t=t, past=past, new_w=new_w, topk=topk, ck=ck),
        grid=(bsz,),
        in_specs=[pl.BlockSpec((1, t, past), lambda b: (b, 0, 0)),
                  pl.BlockSpec((1, t, iq_w), lambda b: (b, 0, DSA_IQ0 // iq_w)),
                  pl.BlockSpec((1, t, LANE), lambda b: (b, 0, DSA_IKW0 // LANE)),
                  pl.BlockSpec((1, t, LANE), lambda b: (b, 0, DSA_IKW0 // LANE))],
        out_specs=pl.BlockSpec((1, t, total), lambda b: (b, 0, 0)),
        out_shape=jax.ShapeDtypeStruct((bsz, t, total), F32),
        scratch_shapes=[pltpu.VMEM((t, total), I32)],
        compiler_params=_cparams(("arbitrary",), 32),
        name="dsa_sample_select",
    )(scores_past, hf, hf, hf)


def _dsa_sample_attend_kernel(pt_ref, q_ref, kn_ref, vn_ref, selp_ref, seln_ref, *refs, pp, t, new_w):
    k_pages, v_pages = refs[:pp], refs[pp:2 * pp]
    o_ref, m_ref, l_ref, acc_ref = refs[2 * pp:]
    j = pl.program_id(1)
    gt = DSA_GROUP * t

    @pl.when(j == 0)
    def _():
        m_ref[...] = jnp.full(m_ref.shape, MASKED, F32)
        l_ref[...] = jnp.zeros_like(l_ref)
        acc_ref[...] = jnp.zeros_like(acc_ref)

    q_groups = [jnp.concatenate([q_ref[0, :, (c * DSA_GROUP + g) * DSA_HEAD_DIM:(c * DSA_GROUP + g + 1) * DSA_HEAD_DIM]
                                 for g in range(DSA_GROUP)], axis=0).astype(BF16) for c in range(DSA_KV_HEADS)]

    def attend(kcs, vcs, sel):
        n = kcs[0].shape[0]
        for c in range(DSA_KV_HEADS):
            kc, vc = kcs[c], vcs[c]
            logits = (_dot_nt(q_groups[c], kc) * (DSA_HEAD_DIM ** -0.5)).reshape(DSA_GROUP, t, n)
            masked = jnp.where(sel[None], logits, MASKED).reshape(gt, n)
            m_old = m_ref[c]
            m_new = jnp.maximum(m_old, jnp.max(masked, axis=1, keepdims=True))
            p = jnp.where(sel[None], jnp.exp(masked - m_new).reshape(DSA_GROUP, t, n), 0.0).reshape(gt, n)
            alpha = jnp.exp(m_old - m_new)
            l_ref[c] = alpha * l_ref[c] + jnp.sum(p, axis=1, keepdims=True)
            acc_ref[c] = alpha * acc_ref[c] + _dot(p.astype(BF16), vc)
            m_ref[c] = m_new

    def head_rows(pages, c):
        return jnp.concatenate([r[0, pl.ds(c, PAGE_SIZE, stride=DSA_KV_HEADS), :].astype(BF16) for r in pages], axis=0)

    attend([head_rows(k_pages, c) for c in range(DSA_KV_HEADS)],
           [head_rows(v_pages, c) for c in range(DSA_KV_HEADS)], selp_ref[0] != 0.0)

    @pl.when(j == pl.num_programs(1) - 1)
    def _():
        pad = jnp.zeros((new_w - t, DSA_HEAD_DIM), F32)

        def new_rows(ref, c):
            return jnp.concatenate([ref[0, :, c * DSA_HEAD_DIM:(c + 1) * DSA_HEAD_DIM], pad], axis=0).astype(BF16)

        attend([new_rows(kn_ref, c) for c in range(DSA_KV_HEADS)],
               [new_rows(vn_ref, c) for c in range(DSA_KV_HEADS)], seln_ref[0] != 0.0)
        for c in range(DSA_KV_HEADS):
            oc = acc_ref[c] / l_ref[c]
            for g in range(DSA_GROUP):
                hh = c * DSA_GROUP + g
                o_ref[0, :, hh * DSA_HEAD_DIM:(hh + 1) * DSA_HEAD_DIM] = oc[g * t:(g + 1) * t].astype(BF16)


def _dsa_sample_attend(hf, sel, cache_k, cache_v, page_table, *, pp, new_w=512):
    bsz, t, _ = hf.shape
    n_pages = page_table.shape[1]
    past = n_pages * PAGE_SIZE
    assert n_pages % pp == 0 and past % new_w == 0
    hq = DSA_HEADS * DSA_HEAD_DIM
    gt = DSA_GROUP * t

    def page_spec(i):
        return pl.BlockSpec((1, PAGE_SIZE * DSA_KV_HEADS, DSA_HEAD_DIM), lambda b, j, pt: (pt[b, j * pp + i], 0, 0))

    grid_spec = pltpu.PrefetchScalarGridSpec(
        num_scalar_prefetch=1, grid=(bsz, n_pages // pp),
        in_specs=[pl.BlockSpec((1, t, hq), lambda b, j, pt: (b, 0, DSA_Q0 // hq)),
                  pl.BlockSpec((1, t, DSA_HKV), lambda b, j, pt: (b, 0, DSA_K0 // DSA_HKV)),
                  pl.BlockSpec((1, t, DSA_HKV), lambda b, j, pt: (b, 0, DSA_V0 // DSA_HKV)),
                  pl.BlockSpec((1, t, pp * PAGE_SIZE), lambda b, j, pt: (b, 0, j)),
                  pl.BlockSpec((1, t, new_w), lambda b, j, pt: (b, 0, past // new_w))]
                 + [page_spec(i) for i in range(pp)] * 2,
        out_specs=pl.BlockSpec((1, t, hq), lambda b, j, pt: (b, 0, 0)),
        scratch_shapes=[pltpu.VMEM((DSA_KV_HEADS, gt, 1), F32),
                        pltpu.VMEM((DSA_KV_HEADS, gt, 1), F32),
                        pltpu.VMEM((DSA_KV_HEADS, gt, DSA_HEAD_DIM), F32)])
    return pl.pallas_call(
        functools.partial(_dsa_sample_attend_kernel, pp=pp, t=t, new_w=new_w),
        grid_spec=grid_spec,
        out_shape=jax.ShapeDtypeStruct((bsz, t, hq), BF16),
        compiler_params=_cparams(("arbitrary", "arbitrary"), 48),
        name="dsa_sample_attend",
    )(page_table, hf, hf, hf, sel, sel, *([cache_k] * pp), *([cache_v] * pp))


def _router_kernel(x_ref, wt_ref, o_ref):
    x = x_ref[...]
    wt = wt_ref[...]
    xh = x.astype(BF16)
    xl = (x - xh.astype(F32)).astype(BF16)
    wh = wt.astype(BF16)
    wl = (wt - wh.astype(F32)).astype(BF16)
    logits = _dot_nt(wh, xh) + (_dot_nt(wl, xh) + _dot_nt(wh, xl))
    sub = lax.broadcasted_iota(I32, logits.shape, 0)
    l1 = jnp.max(logits, axis=0, keepdims=True)
    i1 = jnp.min(jnp.where(logits == l1, sub, N_EXPERTS), axis=0, keepdims=True)
    rest = jnp.where(sub == i1, -jnp.inf, logits)
    l2 = jnp.max(rest, axis=0, keepdims=True)
    i2 = jnp.min(jnp.where(rest == l2, sub, N_EXPERTS), axis=0, keepdims=True)
    e2 = jnp.exp(l2 - l1)
    denom = 1.0 + e2
    o_ref[...] = jnp.where(sub == i1, 1.0 / denom, jnp.where(sub == i2, e2 / denom, 0.0))


def _router(x, w_t):
    n, d = x.shape
    tm = _row_tile(n, 512)
    return pl.pallas_call(
        _router_kernel,
        grid=(n // tm,),
        in_specs=[pl.BlockSpec((tm, d), lambda i: (i, 0)), pl.BlockSpec((N_EXPERTS, d), lambda i: (0, 0))],
        out_specs=pl.BlockSpec((N_EXPERTS, tm), lambda i: (0, i)),
        out_shape=jax.ShapeDtypeStruct((N_EXPERTS, n), F32),
        compiler_params=_cparams(("arbitrary",), 32),
        name="router",
    )(x, w_t)


MOE_ROWS = 128
MOE_CHUNK_BLOCKS = 2


def _moe_kernel(cnt_ref, xb_ref, gt_ref, wg_ref, wu_ref, wd_ref, o_ref,
                rank_ref, xg_ref, acc_ref, upper_ref, *, ts, nsub, arms):
    i, e, f = pl.program_id(0), pl.program_id(1), pl.program_id(2)
    first_f = f == 0
    last_f = f == pl.num_programs(2) - 1

    @pl.when(jnp.logical_and(i == 0, jnp.logical_and(e == 0, first_f)))
    def _():
        upper_ref[...] = _strict_upper(ts)

    @pl.when(jnp.logical_and(e == 0, first_f))
    def _():
        o_ref[...] = jnp.zeros_like(o_ref)
        for sub in range(nsub):
            chosen = jnp.where(gt_ref[:, sub * ts:(sub + 1) * ts] > 0.0, 1.0, 0.0).astype(BF16)
            rank_ref[sub] = _dot(chosen, upper_ref[...])

    for sub in range(nsub):
        tok = slice(sub * ts, (sub + 1) * ts)
        nb = (cnt_ref[(i * nsub + sub) * N_EXPERTS + e] + MOE_ROWS - 1) // MOE_ROWS
        ge = gt_ref[pl.ds(e, 1), tok]
        slot = jnp.where(ge > 0.0, rank_ref[sub, pl.ds(e, 1), :], -1.0)

        def one_hot(start, rows, slot=slot):
            r = (start + lax.broadcasted_iota(I32, (rows, 1), 0)).astype(F32)
            return slot == r

        def for_arms(fn, nb=nb):
            big = arms[0]

            def chunk(t, carry):
                fn(pl.multiple_of(t * (big * MOE_ROWS), big * MOE_ROWS), big * MOE_ROWS)
                return carry
            lax.fori_loop(0, nb // big, chunk, 0)
            for size in arms[1:]:
                start = pl.multiple_of((nb & ~(2 * size - 1)) * MOE_ROWS, MOE_ROWS)

                @pl.when((nb & size) != 0)
                def _():
                    fn(start, size * MOE_ROWS)

        @pl.when(first_f)
        def _():
            def gather(start, rows):
                hot = jnp.where(one_hot(start, rows), 1.0, 0.0).astype(BF16)
                xg_ref[sub, pl.ds(start, rows), :] = _dot(hot, xb_ref[tok, :]).astype(BF16)
                acc_ref[sub, pl.ds(start, rows), :] = jnp.zeros((rows, acc_ref.shape[2]), F32)
            for_arms(gather)

        def expert(start, rows):
            xs = xg_ref[sub, pl.ds(start, rows), :]
            hg = _dot(xs, wg_ref[0])
            hu = _dot(xs, wu_ref[0])
            h = (hg * _sigmoid(hg) * hu).astype(BF16)
            acc_ref[sub, pl.ds(start, rows), :] += _dot(h, wd_ref[0])
        for_arms(expert)

        @pl.when(last_f)
        def _():
            def scatter(start, rows):
                hot = one_hot(start, rows)
                gate = jnp.sum(jnp.where(hot, ge, 0.0), axis=1, keepdims=True)
                v = (gate * acc_ref[sub, pl.ds(start, rows), :]).astype(BF16)
                o_ref[tok, :] += _dot_tn(jnp.where(hot, 1.0, 0.0).astype(BF16), v)
            for_arms(scatter)


def _moe(xb, gates_t, wg, wu, wd, *, tf=896):
    n, d = xb.shape
    ff = wg.shape[2]
    assert ff % tf == 0
    ts = _row_tile(n, 1024)
    nsub = 2 if n % (2 * ts) == 0 else 1
    tm = ts * nsub
    assert ts % (MOE_CHUNK_BLOCKS * MOE_ROWS) == 0
    arms = tuple(s for s in (8, 4, 2, 1) if s <= MOE_CHUNK_BLOCKS)
    assert arms[0] == MOE_CHUNK_BLOCKS
    counts = jnp.sum((gates_t > 0.0).reshape(N_EXPERTS, n // ts, ts), axis=2).astype(I32).T.reshape(-1)
    grid_spec = pltpu.PrefetchScalarGridSpec(
        num_scalar_prefetch=1, grid=(n // tm, N_EXPERTS, ff // tf),
        in_specs=[pl.BlockSpec((tm, d), lambda i, e, f, c: (i, 0)),
                  pl.BlockSpec((N_EXPERTS, tm), lambda i, e, f, c: (0, i)),
                  pl.BlockSpec((1, d, tf), lambda i, e, f, c: (e, 0, f)),
                  pl.BlockSpec((1, d, tf), lambda i, e, f, c: (e, 0, f)),
                  pl.BlockSpec((1, tf, d), lambda i, e, f, c: (e, f, 0))],
        out_specs=pl.BlockSpec((tm, d), lambda i, e, f, c: (i, 0)),
        scratch_shapes=[pltpu.VMEM((nsub, N_EXPERTS, ts), F32), pltpu.VMEM((nsub, ts, d), BF16),
                        pltpu.VMEM((nsub, ts, d), F32), pltpu.VMEM((ts, ts), BF16)])
    return pl.pallas_call(
        functools.partial(_moe_kernel, ts=ts, nsub=nsub, arms=arms),
        grid_spec=grid_spec,
        out_shape=jax.ShapeDtypeStruct((n, d), F32),
        compiler_params=_cparams(("arbitrary", "arbitrary", "arbitrary"), 60),
        name="moe",
    )(counts, xb, gates_t, wg, wu, wd)


def _pad_cols(w, cols):
    return jnp.pad(w, ((0, 0), (0, cols - w.shape[1])))


def _gla_layer(x, s0, w_in, w_up, b_a, norm_g, w_out, ln_g, ln_b, *, alpha):
    bsz, t, d = x.shape
    x2 = x.reshape(bsz * t, d)
    hf = _proj(x2, _pad_cols(w_in, GLA_COLS).astype(BF16), tn=640, bf16_copy=False)
    hf = hf.reshape(bsz, t, GLA_COLS)
    c = GLA_CHUNK if t % GLA_CHUNK == 0 else t
    valid = c
    if c < GLA_CHUNK:
        hf = jnp.pad(hf, ((0, 0), (0, GLA_CHUNK - c), (0, 0)))
        c = GLA_CHUNK
    w_up_pad = jnp.pad(w_up, ((0, LANE - GLA_GATE_RANK), (0, 0))).astype(BF16)
    o, st = _gla_core(hf, jnp.swapaxes(s0, 2, 3), w_up_pad, b_a.reshape(1, GLA_HK), norm_g.reshape(1, GLA_HV),
                      c=c, valid=valid)
    o = o[:, :t].reshape(bsz * t, GLA_HV)
    y, yb = _mm_res_ln(x2, o, w_out.astype(BF16), ln_g.reshape(1, d), ln_b.reshape(1, d), alpha=alpha)
    return y.reshape(bsz, t, d), yb, jnp.swapaxes(st, 2, 3)


def _dsa_weight(w_in):
    hq = DSA_HEADS * DSA_HEAD_DIM
    iq0 = hq + 2 * DSA_HKV
    ik0 = iq0 + IDX_HEADS * IDX_DIM
    w_iq = w_in[:, iq0:ik0].reshape(-1, IDX_HEADS, IDX_DIM)
    w_iq = jnp.pad(w_iq, ((0, 0), (0, 0), (0, LANE - IDX_DIM))).reshape(-1, IDX_HEADS * LANE)
    w_v = w_in[:, hq + DSA_HKV:iq0].reshape(-1, DSA_KV_HEADS, DSA_HEAD_DIM)
    w_v1 = jnp.pad(w_v, ((0, 0), (0, 0), (0, DSA_HEAD_DIM))).reshape(-1, 2 * DSA_HKV)
    w = jnp.concatenate([w_in[:, :hq], w_iq, w_in[:, hq:iq0], w_v1, _pad_cols(w_in[:, ik0:], LANE)], axis=1)
    return w.astype(BF16)


def _dsa_project(x, w_in):
    bsz, t, d = x.shape
    hf, hb = _proj(x.reshape(bsz * t, d), _dsa_weight(w_in), tn=DSA_COLS // 5, ones_at=DSA_ONES)
    hf = hf.reshape(bsz, t, DSA_COLS)
    k = hf[:, :, DSA_K0:DSA_V0].reshape(bsz, t, DSA_KV_HEADS, DSA_HEAD_DIM)
    v = hf[:, :, DSA_V0:DSA_V1_0].reshape(bsz, t, DSA_KV_HEADS, DSA_HEAD_DIM)
    ik = hf[:, :, DSA_IKW0:DSA_IKW0 + IDX_DIM]
    return hf, hb.reshape(bsz, t, DSA_COLS), k, v, ik


def kernel(x_prompt, x_sample, state_gla, cache_k, cache_v, cache_idx_k, page_table, p_prompt, p_sample,
           ln_mix_g, ln_mix_b, ln_ffn_g, ln_ffn_b, w_in_gla, w_alpha_up, b_alpha, gla_norm_g, w_out_gla,
           w_in_dsa, w_out_dsa, w_ffn_gate, w_ffn_up, w_ffn_down, w_router, w_exp_gate, w_exp_up,
           w_exp_down, w_ple_proj, w_ple_gate):
    depth = ln_mix_g.shape[0]
    alpha = (2.0 * depth) ** 0.25
    d = x_prompt.shape[-1]
    xp, xs = x_prompt, x_sample
    gla_p, gla_s, kp, vp, ikp, ksm, vsm, iksm = [], [], [], [], [], [], [], []
    n_pages = page_table.shape[1]
    pp = 32 if n_pages % 32 == 0 else n_pages
    for i in range(depth):
        j = i // 2
        lg, lb = ln_mix_g[i], ln_mix_b[i]
        if i % 2 == 0:
            gw = (w_in_gla[j], w_alpha_up[j], b_alpha[j], gla_norm_g[j], w_out_gla[j], lg, lb)
            s0 = jnp.zeros((xp.shape[0], GLA_HEADS, GLA_DK, GLA_DV), F32)
            xp, xpb, sp = _gla_layer(xp, s0, *gw, alpha=alpha)
            xs, xsb, ss = _gla_layer(xs, state_gla[j], *gw, alpha=alpha)
            gla_p.append(sp)
            gla_s.append(ss)
        else:
            w_out = w_out_dsa[j].astype(BF16)
            hf, hb, k_p, v_p, ik_p = _dsa_project(xp, w_in_dsa[j])
            op = _dsa_prompt(hf, hb)
            bp, tp, _ = xp.shape
            xp, xpb = _mm_res_ln(xp.reshape(bp * tp, d), op.reshape(bp * tp, -1), w_out, lg.reshape(1, d),
                                 lb.reshape(1, d), alpha=alpha)
            xp = xp.reshape(bp, tp, d)
            hf, hb, k_s, v_s, ik_s = _dsa_project(xs, w_in_dsa[j])
            sc = _dsa_sample_scores(hf, jnp.swapaxes(cache_idx_k[j], 1, 2), page_table, pp=pp)
            sel = _dsa_sample_select(sc, hf)
            n_pool = cache_k.shape[1]
            rows = PAGE_SIZE * DSA_KV_HEADS
            os_ = _dsa_sample_attend(hf, sel, cache_k[j].reshape(n_pool, rows, DSA_HEAD_DIM),
                                     cache_v[j].reshape(n_pool, rows, DSA_HEAD_DIM), page_table, pp=pp)
            bs, ts, _ = xs.shape
            xs, xsb = _mm_res_ln(xs.reshape(bs * ts, d), os_.reshape(bs * ts, -1), w_out, lg.reshape(1, d),
                                 lb.reshape(1, d), alpha=alpha)
            xs = xs.reshape(bs, ts, d)
            kp.append(k_p)
            vp.append(v_p)
            ikp.append(ik_p)
            ksm.append(k_s)
            vsm.append(v_s)
            iksm.append(ik_s)
        fg, fb = ln_ffn_g[i].reshape(1, d), ln_ffn_b[i].reshape(1, d)
        outs = []
        wpg, wpp = w_ple_gate[i].astype(BF16), w_ple_proj[i].astype(BF16)
        for x, xb, p in ((xp, xpb, p_prompt[i]), (xs, xsb, p_sample[i])):
            bsz, t, _ = x.shape
            x2 = x.reshape(bsz * t, d)
            p2 = p.reshape(bsz * t, -1)
            if i % 2 == 0:
                y2 = _ffn_ln(x2, w_ffn_gate[j].astype(BF16), w_ffn_up[j].astype(BF16), w_ffn_down[j].astype(BF16),
                             fg, fb, alpha=alpha)
                y2 = _ple(y2, p2, wpg, wpp)
            else:
                gates = _router(x2, w_router[j].T)
                f2 = _moe(xb, gates, w_exp_gate[j].astype(BF16), w_exp_up[j].astype(BF16),
                          w_exp_down[j].astype(BF16))
                y2 = _res_ln_ple(x2, f2, p2, fg, fb, wpg, wpp, alpha=alpha)
            outs.append(y2.reshape(bsz, t, d))
        xp, xs = outs
    return (xp, xs, jnp.stack(gla_p), jnp.stack(gla_s), jnp.stack(kp), jnp.stack(vp), jnp.stack(ikp),
            jnp.stack(ksm), jnp.stack(vsm), jnp.stack(iksm))
```

```python
import functools

import jax
import jax.numpy as jnp
from jax import lax
from jax.experimental import pallas as pl
from jax.experimental.pallas import tpu as pltpu

F32 = jnp.float32
BF16 = jnp.bfloat16
I32 = jnp.int32

D_MODEL = 1024
LN_EPS = 1e-5
LANE = 128

GLA_HEADS = 4
GLA_DK = 128
GLA_DV = 256
GLA_GATE_RANK = 16
GLA_TAU = 16.0
GLA_CHUNK = 64
GLA_HK = GLA_HEADS * GLA_DK
GLA_HV = GLA_HEADS * GLA_DV
GLA_COLS = 2 * GLA_HK + 2 * GLA_HV + LANE

DSA_HEADS = 8
DSA_HEAD_DIM = 128
DSA_KV_HEADS = 2
DSA_GROUP = DSA_HEADS // DSA_KV_HEADS
DSA_HKV = DSA_KV_HEADS * DSA_HEAD_DIM
IDX_HEADS = 8
IDX_DIM = 64
TOPK_MAX = 256
PAGE_SIZE = 128
DSA_Q0 = 0
DSA_IQ0 = DSA_HEADS * DSA_HEAD_DIM
DSA_K0 = DSA_IQ0 + IDX_HEADS * LANE
DSA_V0 = DSA_K0 + DSA_HKV
DSA_V1_0 = DSA_V0 + DSA_HKV
DSA_IKW0 = DSA_V1_0 + 2 * DSA_HKV
DSA_COLS = DSA_IKW0 + LANE
DSA_ONES = tuple(DSA_V1_0 + (2 * c + 1) * DSA_HEAD_DIM for c in range(DSA_KV_HEADS))

N_EXPERTS = 8

NEG_INF_KEY = -2139095041
INT_MIN = -2147483648
MASKED = -1e30


def _cparams(sem, vmem_mb):
    return pltpu.CompilerParams(dimension_semantics=sem, vmem_limit_bytes=vmem_mb * 2 ** 20)


def _dot(a, b):
    return jnp.dot(a, b, preferred_element_type=F32)


def _dot_nt(a, b):
    return lax.dot_general(a, b, (((1,), (1,)), ((), ())), preferred_element_type=F32)


def _dot_tn(a, b):
    return lax.dot_general(a, b, (((0,), (0,)), ((), ())), preferred_element_type=F32)


def _sigmoid(x):
    return 1.0 / (1.0 + jnp.exp(-x))


def _layer_norm(y, g, b):
    mu = jnp.mean(y, axis=-1, keepdims=True)
    yc = y - mu
    var = jnp.mean(yc * yc, axis=-1, keepdims=True)
    return yc * lax.rsqrt(var + LN_EPS) * g + b


def _row_tile(n, pref):
    t = min(n, pref)
    assert n % t == 0
    return t


def _proj_kernel(x_ref, w_ref, o_ref, ob_ref, *, tn, ones_at):
    xb = x_ref[...].astype(BF16)
    for c in range(w_ref.shape[1] // tn):
        r = _dot(xb, w_ref[:, c * tn:(c + 1) * tn])
        o_ref[:, c * tn:(c + 1) * tn] = r
        if ob_ref is not None:
            ob_ref[:, c * tn:(c + 1) * tn] = r.astype(BF16)
    for c0 in ones_at:
        ob_ref[:, c0:c0 + LANE] = jnp.ones((x_ref.shape[0], LANE), BF16)


def _proj_f32_kernel(x_ref, w_ref, o_ref, *, tn):
    _proj_kernel(x_ref, w_ref, o_ref, None, tn=tn, ones_at=())


def _proj(x, w, *, tn, ones_at=(), bf16_copy=True):
    n, d = x.shape
    cols = w.shape[1]
    assert cols % tn == 0 and (bf16_copy or not ones_at)
    tm = _row_tile(n, 512)
    out_spec = pl.BlockSpec((tm, cols), lambda i: (i, 0))
    if bf16_copy:
        body = functools.partial(_proj_kernel, tn=tn, ones_at=ones_at)
        out_specs = [out_spec, out_spec]
        out_shape = [jax.ShapeDtypeStruct((n, cols), F32), jax.ShapeDtypeStruct((n, cols), BF16)]
    else:
        body = functools.partial(_proj_f32_kernel, tn=tn)
        out_specs, out_shape = out_spec, jax.ShapeDtypeStruct((n, cols), F32)
    return pl.pallas_call(
        body,
        grid=(n // tm,),
        in_specs=[pl.BlockSpec((tm, d), lambda i: (i, 0)), pl.BlockSpec((d, cols), lambda i: (0, 0))],
        out_specs=out_specs,
        out_shape=out_shape,
        compiler_params=_cparams(("arbitrary",), 56),
        name="proj",
    )(x, w)


def _mm_res_ln_kernel(x_ref, a_ref, w_ref, g_ref, b_ref, o_ref, ob_ref, *, alpha):
    m = _dot(a_ref[...].astype(BF16), w_ref[...])
    y = _layer_norm(alpha * x_ref[...] + m, g_ref[...], b_ref[...])
    o_ref[...] = y
    ob_ref[...] = y.astype(BF16)


def _mm_res_ln(x, a, w, g, b, *, alpha):
    n, d = x.shape
    ka = a.shape[1]
    tm = _row_tile(n, 512)
    row = lambda i: (i, 0)
    fixed = lambda i: (0, 0)
    return pl.pallas_call(
        functools.partial(_mm_res_ln_kernel, alpha=alpha),
        grid=(n // tm,),
        in_specs=[pl.BlockSpec((tm, d), row), pl.BlockSpec((tm, ka), row), pl.BlockSpec((ka, d), fixed),
                  pl.BlockSpec((1, d), fixed), pl.BlockSpec((1, d), fixed)],
        out_specs=[pl.BlockSpec((tm, d), row), pl.BlockSpec((tm, d), row)],
        out_shape=[jax.ShapeDtypeStruct((n, d), F32), jax.ShapeDtypeStruct((n, d), BF16)],
        compiler_params=_cparams(("arbitrary",), 40),
        name="mm_res_ln",
    )(x, a, w, g, b)


def _ffn_ln_kernel(x_ref, wg_ref, wu_ref, wd_ref, g_ref, b_ref, o_ref, xb_ref, acc_ref, *, alpha):
    f = pl.program_id(1)

    @pl.when(f == 0)
    def _():
        xb_ref[...] = x_ref[...].astype(BF16)
        acc_ref[...] = jnp.zeros_like(acc_ref)

    xb = xb_ref[...]
    hg = _dot(xb, wg_ref[...])
    hu = _dot(xb, wu_ref[...])
    h = (hg * _sigmoid(hg) * hu).astype(BF16)
    acc_ref[...] += _dot(h, wd_ref[...])

    @pl.when(f == pl.num_programs(1) - 1)
    def _():
        o_ref[...] = _layer_norm(alpha * x_ref[...] + acc_ref[...], g_ref[...], b_ref[...])


def _ffn_ln(x, wg, wu, wd, g, b, *, alpha, tf=512):
    n, d = x.shape
    ff = wg.shape[1]
    assert ff % tf == 0
    tm = _row_tile(n, 1024)
    return pl.pallas_call(
        functools.partial(_ffn_ln_kernel, alpha=alpha),
        grid=(n // tm, ff // tf),
        in_specs=[pl.BlockSpec((tm, d), lambda i, f: (i, 0)),
                  pl.BlockSpec((d, tf), lambda i, f: (0, f)),
                  pl.BlockSpec((d, tf), lambda i, f: (0, f)),
                  pl.BlockSpec((tf, d), lambda i, f: (f, 0)),
                  pl.BlockSpec((1, d), lambda i, f: (0, 0)),
                  pl.BlockSpec((1, d), lambda i, f: (0, 0))],
        out_specs=pl.BlockSpec((tm, d), lambda i, f: (i, 0)),
        out_shape=jax.ShapeDtypeStruct((n, d), F32),
        scratch_shapes=[pltpu.VMEM((tm, d), BF16), pltpu.VMEM((tm, d), F32)],
        compiler_params=_cparams(("arbitrary", "arbitrary"), 48),
        name="ffn_ln",
    )(x, wg, wu, wd, g, b)


def _ple_kernel(x_ref, p_ref, wg_ref, wp_ref, o_ref):
    x = x_ref[...]
    gate = _sigmoid(_dot(x.astype(BF16), wg_ref[...]))
    o_ref[...] = x + gate * _dot(p_ref[...].astype(BF16), wp_ref[...])


def _ple(x, p, wg, wp):
    n, d = x.shape
    pd = p.shape[1]
    tm = _row_tile(n, 512)
    row = lambda i: (i, 0)
    fixed = lambda i: (0, 0)
    return pl.pallas_call(
        _ple_kernel,
        grid=(n // tm,),
        in_specs=[pl.BlockSpec((tm, d), row), pl.BlockSpec((tm, pd), row),
                  pl.BlockSpec((d, d), fixed), pl.BlockSpec((pd, d), fixed)],
        out_specs=pl.BlockSpec((tm, d), row),
        out_shape=jax.ShapeDtypeStruct((n, d), F32),
        compiler_params=_cparams(("arbitrary",), 40),
        name="ple",
    )(x, p, wg, wp)


def _res_ln_ple_kernel(x_ref, f_ref, p_ref, g_ref, b_ref, wg_ref, wp_ref, o_ref, *, alpha):
    z = _layer_norm(alpha * x_ref[...] + f_ref[...], g_ref[...], b_ref[...])
    gate = _sigmoid(_dot(z.astype(BF16), wg_ref[...]))
    o_ref[...] = z + gate * _dot(p_ref[...].astype(BF16), wp_ref[...])


def _res_ln_ple(x, f, p, g, b, wg, wp, *, alpha):
    n, d = x.shape
    pd = p.shape[1]
    tm = _row_tile(n, 512)
    row = lambda i: (i, 0)
    fixed = lambda i: (0, 0)
    return pl.pallas_call(
        functools.partial(_res_ln_ple_kernel, alpha=alpha),
        grid=(n // tm,),
        in_specs=[pl.BlockSpec((tm, d), row), pl.BlockSpec((tm, d), row), pl.BlockSpec((tm, pd), row),
                  pl.BlockSpec((1, d), fixed), pl.BlockSpec((1, d), fixed),
                  pl.BlockSpec((d, d), fixed), pl.BlockSpec((pd, d), fixed)],
        out_specs=pl.BlockSpec((tm, d), row),
        out_shape=jax.ShapeDtypeStruct((n, d), F32),
        compiler_params=_cparams(("arbitrary",), 40),
        name="res_ln_ple",
    )(x, f, p, g, b, wg, wp)


def _gla_kernel(q_ref, k_ref, v_ref, g_ref, a_ref, wup_ref, ba_ref, ng_ref, s0_ref, o_ref, st_out_ref, st_ref,
                *, c, valid, nbs):
    t = pl.program_id(1)

    @pl.when(t == 0)
    def _():
        st_ref[...] = s0_ref[...]

    row = lax.broadcasted_iota(I32, (c, c), 0)
    col = lax.broadcasted_iota(I32, (c, c), 1)
    causal = row >= col
    for bi in range(nbs):
        logit = _dot(a_ref[bi].astype(BF16), wup_ref[...]) + ba_ref[...]
        log_a = (jnp.minimum(logit, 0.0) - jnp.log(1.0 + jnp.exp(-jnp.abs(logit)))) * (1.0 / GLA_TAU)
        if valid < c:
            log_a = jnp.where(lax.broadcasted_iota(I32, log_a.shape, 0) < valid, log_a, 0.0)
        bcum = jnp.dot(causal.astype(F32), log_a, preferred_element_type=F32, precision=lax.Precision.HIGHEST)
        blast = bcum[c - 1:c, :]
        q = q_ref[bi] * (GLA_DK ** -0.5)
        k = k_ref[bi]
        q_dec = (q * jnp.exp(bcum)).astype(BF16)
        k_inv = (k * jnp.exp(-bcum)).astype(BF16)
        k_dec = (k * jnp.exp(blast - bcum)).astype(BF16)
        decay = jnp.exp(blast)
        v = v_ref[bi].astype(BF16)
        for h in range(GLA_HEADS):
            sk = slice(h * GLA_DK, (h + 1) * GLA_DK)
            sv = slice(h * GLA_DV, (h + 1) * GLA_DV)
            att = jnp.where(causal, _dot_nt(q_dec[:, sk], k_inv[:, sk]), 0.0).astype(BF16)
            st = st_ref[bi, h]
            o = _dot(att, v[:, sv]) + _dot_nt(q_dec[:, sk], st.astype(BF16))
            st_ref[bi, h] = st * decay[:, sk] + _dot_tn(v[:, sv], k_dec[:, sk])
            mu = jnp.mean(o, axis=-1, keepdims=True)
            oc = o - mu
            var = jnp.mean(oc * oc, axis=-1, keepdims=True)
            on = oc * lax.rsqrt(var + LN_EPS) * ng_ref[:, sv]
            gg = g_ref[bi, :, sv]
            o_ref[bi, :, sv] = (on * (gg * _sigmoid(gg))).astype(BF16)

    @pl.when(t == pl.num_programs(1) - 1)
    def _():
        st_out_ref[...] = st_ref[...]


def _gla_core(h, s0_t, wup, ba, ng, *, c, valid):
    bsz, tlen, _ = h.shape
    assert tlen % c == 0
    nbs = 4 if bsz % 4 == 0 else 1
    v_blk = (2 * GLA_HK) // GLA_HV
    state_spec = pl.BlockSpec((nbs, GLA_HEADS, GLA_DV, GLA_DK), lambda b, t: (b, 0, 0, 0))
    return pl.pallas_call(
        functools.partial(_gla_kernel, c=c, valid=valid, nbs=nbs),
        grid=(bsz // nbs, tlen // c),
        in_specs=[pl.BlockSpec((nbs, c, GLA_HK), lambda b, t: (b, t, 0)),
                  pl.BlockSpec((nbs, c, GLA_HK), lambda b, t: (b, t, 1)),
                  pl.BlockSpec((nbs, c, GLA_HV), lambda b, t: (b, t, v_blk)),
                  pl.BlockSpec((nbs, c, GLA_HV), lambda b, t: (b, t, v_blk + 1)),
                  pl.BlockSpec((nbs, c, LANE), lambda b, t: (b, t, (2 * GLA_HK + 2 * GLA_HV) // LANE)),
                  pl.BlockSpec((LANE, GLA_HK), lambda b, t: (0, 0)),
                  pl.BlockSpec((1, GLA_HK), lambda b, t: (0, 0)),
                  pl.BlockSpec((1, GLA_HV), lambda b, t: (0, 0)),
                  state_spec],
        out_specs=[pl.BlockSpec((nbs, c, GLA_HV), lambda b, t: (b, t, 0)), state_spec],
        out_shape=[jax.ShapeDtypeStruct((bsz, tlen, GLA_HV), BF16),
                   jax.ShapeDtypeStruct((bsz, GLA_HEADS, GLA_DV, GLA_DK), F32)],
        scratch_shapes=[pltpu.VMEM((nbs, GLA_HEADS, GLA_DV, GLA_DK), F32)],
        compiler_params=_cparams(("arbitrary", "arbitrary"), 32),
        name="gla_core",
    )(h, h, h, h, h, wup, ba, ng, s0_t)


def _sortable(s):
    bits = lax.bitcast_convert_type(s, I32)
    return bits ^ (lax.shift_right_arithmetic(bits, 31) & 0x7FFFFFFF)


def _lane_fold(m):
    acc = m[:, 0:LANE]
    for i in range(1, m.shape[1] // LANE):
        acc = acc + m[:, i * LANE:(i + 1) * LANE]
    return acc


def _lane_fold_max(m):
    acc = m[:, 0:LANE]
    for i in range(1, m.shape[1] // LANE):
        acc = jnp.maximum(acc, m[:, i * LANE:(i + 1) * LANE])
    return acc


def _bias_bits(sel):
    return lax.bitcast_convert_type(jnp.where(sel, 0.0, MASKED).astype(F32), I32)


def _kth_largest_key(count_ge, rows, k):
    kf = float(k)
    thr0 = jnp.where(count_ge(jnp.zeros((rows, 1), I32)) >= kf, 0, INT_MIN).astype(I32)

    def bit_body(i, thr):
        cand = thr | lax.shift_left(jnp.int32(1), 30 - i)
        return jnp.where(count_ge(cand) >= kf, cand, thr)

    return lax.fori_loop(0, 31, bit_body, thr0)


def _strict_upper(n):
    return (lax.broadcasted_iota(I32, (n, n), 0) < lax.broadcasted_iota(I32, (n, n), 1)).astype(BF16)


def _dsa_prompt_kernel(q_ref, iq_ref, iw_ref, ikw_ref, k_ref, v1_ref, o_ref,
                       keys_ref, top2_ref, iqs_ref, qg_ref, d_ref, m_ref, acc_ref, *, tq, tk, topk):
    j = pl.program_id(1)
    nkt = ((j + 1) * tq + tk - 1) // tk
    gq = DSA_GROUP * tq
    rp = 16
    for h in range(IDX_HEADS):
        iqs_ref[h * tq:(h + 1) * tq, :] = iq_ref[0, :, h * LANE:(h + 1) * LANE]
    for c in range(DSA_KV_HEADS):
        for g in range(DSA_GROUP):
            hh = c * DSA_GROUP + g
            qg_ref[c, g * tq:(g + 1) * tq, :] = q_ref[0, :, hh * DSA_HEAD_DIM:(hh + 1) * DSA_HEAD_DIM]
    ws = iw_ref[0][:, IDX_DIM:IDX_DIM + IDX_HEADS] * (IDX_HEADS ** -0.5) * (IDX_DIM ** -0.5)

    def score_tile(kt, diagonal):
        ikt = ikw_ref[0, pl.ds(pl.multiple_of(kt * tk, tk), tk), :]
        d_ref[...] = _dot_nt(iqs_ref[...], ikt)
        for r in range(tq // rp):
            s = jnp.zeros((rp, tk), F32)
            for h in range(IDX_HEADS):
                dh = d_ref[h * tq + r * rp:h * tq + (r + 1) * rp, :]
                s = s + ws[r * rp:(r + 1) * rp, h:h + 1] * jnp.maximum(dh, 0.0)
            key = _sortable(s + 0.0)
            if diagonal:
                qpos = j * tq + r * rp + lax.broadcasted_iota(I32, (rp, tk), 0)
                kpos = kt * tk + lax.broadcasted_iota(I32, (rp, tk), 1)
                key = jnp.where(kpos <= qpos, key, INT_MIN)
            keys_ref[kt, r * rp:(r + 1) * rp, :] = key
            rows = slice(r * rp, (r + 1) * rp)
            m1, m2 = top2_ref[rows, 0:LANE], top2_ref[rows, LANE:2 * LANE]
            for lt in range(tk // LANE):
                x = key[:, lt * LANE:(lt + 1) * LANE]
                m2 = jnp.maximum(m2, jnp.minimum(m1, x))
                m1 = jnp.maximum(m1, x)
            top2_ref[rows, :] = jnp.concatenate([m1, m2], axis=1)

    def score_body(kt, carry):
        score_tile(kt, False)
        return carry

    top2_ref[...] = jnp.full(top2_ref.shape, INT_MIN, I32)
    lax.fori_loop(0, nkt - 1, score_body, 0)
    score_tile(nkt - 1, True)

    def count_ge(t):
        def body(kt, acc):
            return acc + _lane_fold(jnp.where(keys_ref[kt] >= t, 1.0, 0.0))
        return jnp.sum(lax.fori_loop(0, nkt, body, jnp.zeros((tq, LANE), F32)), axis=1, keepdims=True)

    kf = float(topk)
    lo0 = jnp.min(top2_ref[:, LANE:2 * LANE], axis=1, keepdims=True)
    hi0 = jnp.max(top2_ref[:, 0:LANE], axis=1, keepdims=True) + 1

    def search_done(lo, hi, c_lo):
        row_done = jnp.logical_or(c_lo == kf, hi == lo + 1)
        return jnp.min(jnp.where(row_done, 1.0, 0.0)) > 0.0

    def search_cond(state):
        return jnp.logical_not(state[4])

    def search_body(state):
        lo, hi, c_lo, c_hi, _ = state
        for _ in range(4):
            mid = lax.shift_right_arithmetic(lo, 1) + lax.shift_right_arithmetic(hi, 1) + (lo & hi & 1)
            c = count_ge(mid)
            ok = c >= kf
            lo, c_lo = jnp.where(ok, mid, lo), jnp.where(ok, c, c_lo)
            hi, c_hi = jnp.where(ok, hi, mid), jnp.where(ok, c_hi, c)
        return lo, hi, c_lo, c_hi, search_done(lo, hi, c_lo)

    c_lo0 = count_ge(lo0)
    thr, _, c_thr, c_above, _ = lax.while_loop(
        search_cond, search_body, (lo0, hi0, c_lo0, jnp.zeros((tq, 1), F32), search_done(lo0, hi0, c_lo0)))
    need = kf - c_above
    tie_row = jnp.logical_and(c_thr > kf, thr > INT_MIN)
    any_tie = jnp.max(jnp.where(tie_row, 1.0, 0.0)) > 0.0

    @pl.when(jnp.logical_not(any_tie))
    def _():
        thr_eff = jnp.maximum(thr, INT_MIN + 1)

        def body(kt, carry):
            keys_ref[kt] = _bias_bits(keys_ref[kt] >= thr_eff)
            return carry
        lax.fori_loop(0, nkt, body, 0)

    @pl.when(any_tie)
    def _():
        upper = _strict_upper(tk)

        def body(kt, seen):
            kk = keys_ref[kt]
            eq = kk == thr
            before = seen + _dot(jnp.where(eq, 1.0, 0.0).astype(BF16), upper)
            sel = jnp.logical_and(kk > INT_MIN, jnp.logical_or(kk > thr, jnp.logical_and(eq, before < need)))
            keys_ref[kt] = _bias_bits(sel)
            return seen + jnp.sum(jnp.where(eq, 1.0, 0.0), axis=1, keepdims=True)
        lax.fori_loop(0, nkt, body, jnp.zeros((tq, 1), F32))

    m_ref[...] = jnp.full(m_ref.shape, MASKED, F32)
    acc_ref[...] = jnp.zeros_like(acc_ref)
    c2 = (DSA_HEAD_DIM ** -0.5) * 1.4426950408889634
    dh = DSA_HEAD_DIM

    def attend_body(kt, carry):
        rows = pl.ds(pl.multiple_of(kt * tk, tk), tk)
        bias = lax.bitcast_convert_type(keys_ref[kt], F32)[None]
        for c in range(DSA_KV_HEADS):
            kc = k_ref[0, rows, c * dh:(c + 1) * dh]
            v1 = v1_ref[0, rows, c * 2 * dh:(c + 1) * 2 * dh]
            x = (_dot_nt(qg_ref[c], kc).reshape(DSA_GROUP, tq, tk) + bias).reshape(gq, tk)
            m_old = m_ref[c]
            m_new = jnp.maximum(m_old, jnp.max(x, axis=1, keepdims=True))
            p = jnp.exp2((x - jnp.concatenate([m_new] * (tk // LANE), axis=1)) * c2).astype(BF16)
            alpha = jnp.exp2((m_old - m_new) * c2)
            acc_ref[c] = jnp.concatenate([alpha, alpha], axis=1) * acc_ref[c] + _dot(p, v1)
            m_ref[c] = m_new
        return carry

    lax.fori_loop(0, nkt, attend_body, 0)
    for c in range(DSA_KV_HEADS):
        oc = acc_ref[c, :, 0:dh] / acc_ref[c, :, dh:2 * dh]
        for g in range(DSA_GROUP):
            hh = c * DSA_GROUP + g
            o_ref[0, :, hh * dh:(hh + 1) * dh] = oc[g * tq:(g + 1) * tq].astype(BF16)


def _dsa_prompt(hf, hb, *, tq=256, tk=1024):
    bsz, s, _ = hf.shape
    tk = min(tk, s)
    assert s % tq == 0 and s % tk == 0 and tk % tq == 0
    topk = min(TOPK_MAX, s // 4)
    assert topk <= 2 * LANE
    hq = DSA_HEADS * DSA_HEAD_DIM
    gq = DSA_GROUP * tq
    return pl.pallas_call(
        functools.partial(_dsa_prompt_kernel, tq=tq, tk=tk, topk=topk),
        grid=(bsz, s // tq),
        in_specs=[pl.BlockSpec((1, tq, hq), lambda b, j: (b, j, DSA_Q0 // hq)),
                  pl.BlockSpec((1, tq, IDX_HEADS * LANE), lambda b, j: (b, j, DSA_IQ0 // (IDX_HEADS * LANE))),
                  pl.BlockSpec((1, tq, LANE), lambda b, j: (b, j, DSA_IKW0 // LANE)),
                  pl.BlockSpec((1, s, LANE), lambda b, j: (b, 0, DSA_IKW0 // LANE), pipeline_mode=pl.Buffered(1)),
                  pl.BlockSpec((1, s, DSA_HKV), lambda b, j: (b, 0, DSA_K0 // DSA_HKV),
                               pipeline_mode=pl.Buffered(1)),
                  pl.BlockSpec((1, s, 2 * DSA_HKV), lambda b, j: (b, 0, DSA_V1_0 // (2 * DSA_HKV)),
                               pipeline_mode=pl.Buffered(1))],
        out_specs=pl.BlockSpec((1, tq, hq), lambda b, j: (b, j, 0)),
        out_shape=jax.ShapeDtypeStruct((bsz, s, hq), BF16),
        scratch_shapes=[pltpu.VMEM((s // tk, tq, tk), I32),
                        pltpu.VMEM((tq, 2 * LANE), I32),
                        pltpu.VMEM((IDX_HEADS * tq, LANE), BF16),
                        pltpu.VMEM((DSA_KV_HEADS, gq, DSA_HEAD_DIM), BF16),
                        pltpu.VMEM((IDX_HEADS * tq, tk), F32),
                        pltpu.VMEM((DSA_KV_HEADS, gq, LANE), F32),
                        pltpu.VMEM((DSA_KV_HEADS, gq, 2 * DSA_HEAD_DIM), F32)],
        compiler_params=_cparams(("arbitrary", "arbitrary"), 56),
        name="dsa_prompt",
    )(hb, hb, hf, hb, hb, hb)


def _sample_ws(iw_ref):
    return iw_ref[0][:, IDX_DIM:IDX_DIM + IDX_HEADS] * (IDX_HEADS ** -0.5)


def _sample_scores(d, ws, t):
    s = jnp.zeros((t, d.shape[1]), F32)
    for h in range(IDX_HEADS):
        s = s + ws[:, h:h + 1] * jnp.maximum(d[h * t:(h + 1) * t] * (IDX_DIM ** -0.5), 0.0)
    return s + 0.0


def _dsa_sample_scores_kernel(pt_ref, iq_ref, iw_ref, *refs, pp, t):
    page_refs, o_ref = refs[:pp], refs[pp]
    iq_stack = jnp.concatenate([iq_ref[0, :, h * LANE:h * LANE + IDX_DIM] for h in range(IDX_HEADS)], axis=0)
    ik_t = jnp.concatenate([r[0].astype(BF16) for r in page_refs], axis=1)
    o_ref[0] = _sample_scores(_dot(iq_stack.astype(BF16), ik_t), _sample_ws(iw_ref), t)


def _dsa_sample_scores(hf, cache_ik_t, page_table, *, pp):
    bsz, t, _ = hf.shape
    n_pages = page_table.shape[1]
    assert n_pages % pp == 0
    iq_w = IDX_HEADS * LANE

    def page_spec(i):
        return pl.BlockSpec((1, IDX_DIM, PAGE_SIZE), lambda b, j, pt: (pt[b, j * pp + i], 0, 0))

    grid_spec = pltpu.PrefetchScalarGridSpec(
        num_scalar_prefetch=1, grid=(bsz, n_pages // pp),
        in_specs=[pl.BlockSpec((1, t, iq_w), lambda b, j, pt: (b, 0, DSA_IQ0 // iq_w)),
                  pl.BlockSpec((1, t, LANE), lambda b, j, pt: (b, 0, DSA_IKW0 // LANE))]
                 + [page_spec(i) for i in range(pp)],
        out_specs=pl.BlockSpec((1, t, pp * PAGE_SIZE), lambda b, j, pt: (b, 0, j)))
    return pl.pallas_call(
        functools.partial(_dsa_sample_scores_kernel, pp=pp, t=t),
        grid_spec=grid_spec,
        out_shape=jax.ShapeDtypeStruct((bsz, t, n_pages * PAGE_SIZE), F32),
        compiler_params=_cparams(("arbitrary", "arbitrary"), 32),
        name="dsa_sample_scores",
    )(page_table, hf, hf, *([cache_ik_t] * pp))


def _dsa_sample_select_kernel(sp_ref, iq_ref, iw_ref, ikw_ref, o_ref, keys_ref, *, t, past, new_w, topk, ck):
    iq_stack = jnp.concatenate([iq_ref[0, :, h * LANE:(h + 1) * LANE] for h in range(IDX_HEADS)], axis=0)
    ik_new = jnp.concatenate([ikw_ref[0], jnp.zeros((new_w - t, LANE), F32)], axis=0)
    d_new = _dot_nt(iq_stack.astype(BF16), ik_new.astype(BF16))
    s_new = _sample_scores(d_new, _sample_ws(iw_ref), t)
    row = lax.broadcasted_iota(I32, (t, new_w), 0)
    lane = lax.broadcasted_iota(I32, (t, new_w), 1)
    s_new = jnp.where(lane <= row, s_new, -jnp.inf)
    keys_ref[:, 0:past] = _sortable(sp_ref[0])
    keys_ref[:, past:past + new_w] = _sortable(s_new)
    n_ck = (past + new_w) // ck

    def counts(pred):
        acc = jnp.zeros((t, LANE), F32)
        for i in range(n_ck):
            acc = acc + _lane_fold(jnp.where(pred(keys_ref[:, i * ck:(i + 1) * ck]), 1.0, 0.0))
        return jnp.sum(acc, axis=1, keepdims=True)

    thr = _kth_largest_key(lambda cand: counts(lambda kk: kk >= cand), t, topk)
    n_gt = counts(lambda kk: kk > thr)
    n_eq = counts(lambda kk: kk == thr)
    need = float(topk) - n_gt
    tie_row = jnp.logical_and(n_eq > need, thr > NEG_INF_KEY)
    any_tie = jnp.max(jnp.where(tie_row, 1.0, 0.0)) > 0.0

    @pl.when(jnp.logical_not(any_tie))
    def _():
        thr_eff = jnp.maximum(thr, NEG_INF_KEY + 1)
        o_ref[0] = jnp.where(keys_ref[...] >= thr_eff, 1.0, 0.0)

    @pl.when(any_tie)
    def _():
        upper = _strict_upper(ck)
        seen = jnp.zeros((t, 1), F32)
        for i in range(n_ck):
            kk = keys_ref[:, i * ck:(i + 1) * ck]
            eq = kk == thr
            before = seen + _dot(jnp.where(eq, 1.0, 0.0).astype(BF16), upper)
            sel = jnp.logical_and(kk > NEG_INF_KEY, jnp.logical_or(kk > thr, jnp.logical_and(eq, before < need)))
            o_ref[0, :, i * ck:(i + 1) * ck] = jnp.where(sel, 1.0, 0.0)
            seen = seen + jnp.sum(jnp.where(eq, 1.0, 0.0), axis=1, keepdims=True)


def _dsa_sample_select(scores_past, hf, *, new_w=512, ck=512):
    bsz, t, past = scores_past.shape
    assert past % ck == 0 and new_w % ck == 0
    topk = min(TOPK_MAX, (past + t) // 4)
    iq_w = IDX_HEADS * LANE
    total = past + new_w
    return pl.pallas_call(
        functools.partial(_dsa_sample_select_kernel, t=t, past=past, new_w=new_w, topk=topk, ck=ck),
        grid=(bsz,),
        in_specs=[pl.BlockSpec((1, t, past), lambda b: (b, 0, 0)),
                  pl.BlockSpec((1, t, iq_w), lambda b: (b, 0, DSA_IQ0 // iq_w)),
                  pl.BlockSpec((1, t, LANE), lambda b: (b, 0, DSA_IKW0 // LANE)),
                  pl.BlockSpec((1, t, LANE), lambda b: (b, 0, DSA_IKW0 // LANE))],
        out_specs=pl.BlockSpec((1, t, total), lambda b: (b, 0, 0)),
        out_shape=jax.ShapeDtypeStruct((bsz, t, total), F32),
        scratch_shapes=[pltpu.VMEM((t, total), I32)],
        compiler_params=_cparams(("arbitrary",), 32),
        name="dsa_sample_select",
    )(scores_past, hf, hf, hf)


def _dsa_sample_attend_kernel(pt_ref, q_ref, kn_ref, vn_ref, selp_ref, seln_ref, *refs, pp, t, new_w):
    k_pages, v_pages = refs[:pp], refs[pp:2 * pp]
    o_ref, m_ref, l_ref, acc_ref = refs[2 * pp:]
    j = pl.program_id(1)
    gt = DSA_GROUP * t

    @pl.when(j == 0)
    def _():
        m_ref[...] = jnp.full(m_ref.shape, MASKED, F32)
        l_ref[...] = jnp.zeros_like(l_ref)
        acc_ref[...] = jnp.zeros_like(acc_ref)

    q_groups = [jnp.concatenate([q_ref[0, :, (c * DSA_GROUP + g) * DSA_HEAD_DIM:(c * DSA_GROUP + g + 1) * DSA_HEAD_DIM]
                                 for g in range(DSA_GROUP)], axis=0).astype(BF16) for c in range(DSA_KV_HEADS)]

    def attend(kcs, vcs, sel):
        n = kcs[0].shape[0]
        for c in range(DSA_KV_HEADS):
            kc, vc = kcs[c], vcs[c]
            logits = (_dot_nt(q_groups[c], kc) * (DSA_HEAD_DIM ** -0.5)).reshape(DSA_GROUP, t, n)
            masked = jnp.where(sel[None], logits, MASKED).reshape(gt, n)
            m_old = m_ref[c]
            m_new = jnp.maximum(m_old, jnp.max(masked, axis=1, keepdims=True))
            p = jnp.where(sel[None], jnp.exp(masked - m_new).reshape(DSA_GROUP, t, n), 0.0).reshape(gt, n)
            alpha = jnp.exp(m_old - m_new)
            l_ref[c] = alpha * l_ref[c] + jnp.sum(p, axis=1, keepdims=True)
            acc_ref[c] = alpha * acc_ref[c] + _dot(p.astype(BF16), vc)
            m_ref[c] = m_new

    def head_rows(pages, c):
        return jnp.concatenate([r[0, pl.ds(c, PAGE_SIZE, stride=DSA_KV_HEADS), :].astype(BF16) for r in pages], axis=0)

    attend([head_rows(k_pages, c) for c in range(DSA_KV_HEADS)],
           [head_rows(v_pages, c) for c in range(DSA_KV_HEADS)], selp_ref[0] != 0.0)

    @pl.when(j == pl.num_programs(1) - 1)
    def _():
        pad = jnp.zeros((new_w - t, DSA_HEAD_DIM), F32)

        def new_rows(ref, c):
            return jnp.concatenate([ref[0, :, c * DSA_HEAD_DIM:(c + 1) * DSA_HEAD_DIM], pad], axis=0).astype(BF16)

        attend([new_rows(kn_ref, c) for c in range(DSA_KV_HEADS)],
               [new_rows(vn_ref, c) for c in range(DSA_KV_HEADS)], seln_ref[0] != 0.0)
        for c in range(DSA_KV_HEADS):
            oc = acc_ref[c] / l_ref[c]
            for g in range(DSA_GROUP):
                hh = c * DSA_GROUP + g
                o_ref[0, :, hh * DSA_HEAD_DIM:(hh + 1) * DSA_HEAD_DIM] = oc[g * t:(g + 1) * t].astype(BF16)


def _dsa_sample_attend(hf, sel, cache_k, cache_v, page_table, *, pp, new_w=512):
    bsz, t, _ = hf.shape
    n_pages = page_table.shape[1]
    past = n_pages * PAGE_SIZE
    assert n_pages % pp == 0 and past % new_w == 0
    hq = DSA_HEADS * DSA_HEAD_DIM
    gt = DSA_GROUP * t

    def page_spec(i):
        return pl.BlockSpec((1, PAGE_SIZE * DSA_KV_HEADS, DSA_HEAD_DIM), lambda b, j, pt: (pt[b, j * pp + i], 0, 0))

    grid_spec = pltpu.PrefetchScalarGridSpec(
        num_scalar_prefetch=1, grid=(bsz, n_pages // pp),
        in_specs=[pl.BlockSpec((1, t, hq), lambda b, j, pt: (b, 0, DSA_Q0 // hq)),
                  pl.BlockSpec((1, t, DSA_HKV), lambda b, j, pt: (b, 0, DSA_K0 // DSA_HKV)),
                  pl.BlockSpec((1, t, DSA_HKV), lambda b, j, pt: (b, 0, DSA_V0 // DSA_HKV)),
                  pl.BlockSpec((1, t, pp * PAGE_SIZE), lambda b, j, pt: (b, 0, j)),
                  pl.BlockSpec((1, t, new_w), lambda b, j, pt: (b, 0, past // new_w))]
                 + [page_spec(i) for i in range(pp)] * 2,
        out_specs=pl.BlockSpec((1, t, hq), lambda b, j, pt: (b, 0, 0)),
        scratch_shapes=[pltpu.VMEM((DSA_KV_HEADS, gt, 1), F32),
                        pltpu.VMEM((DSA_KV_HEADS, gt, 1), F32),
                        pltpu.VMEM((DSA_KV_HEADS, gt, DSA_HEAD_DIM), F32)])
    return pl.pallas_call(
        functools.partial(_dsa_sample_attend_kernel, pp=pp, t=t, new_w=new_w),
        grid_spec=grid_spec,
        out_shape=jax.ShapeDtypeStruct((bsz, t, hq), BF16),
        compiler_params=_cparams(("arbitrary", "arbitrary"), 48),
        name="dsa_sample_attend",
    )(page_table, hf, hf, hf, sel, sel, *([cache_k] * pp), *([cache_v] * pp))


def _router_kernel(x_ref, wt_ref, o_ref):
    x = x_ref[...]
    wt = wt_ref[...]
    xh = x.astype(BF16)
    xl = (x - xh.astype(F32)).astype(BF16)
    wh = wt.astype(BF16)
    wl = (wt - wh.astype(F32)).astype(BF16)
    logits = _dot_nt(wh, xh) + (_dot_nt(wl, xh) + _dot_nt(wh, xl))
    sub = lax.broadcasted_iota(I32, logits.shape, 0)
    l1 = jnp.max(logits, axis=0, keepdims=True)
    i1 = jnp.min(jnp.where(logits == l1, sub, N_EXPERTS), axis=0, keepdims=True)
    rest = jnp.where(sub == i1, -jnp.inf, logits)
    l2 = jnp.max(rest, axis=0, keepdims=True)
    i2 = jnp.min(jnp.where(rest == l2, sub, N_EXPERTS), axis=0, keepdims=True)
    e2 = jnp.exp(l2 - l1)
    denom = 1.0 + e2
    o_ref[...] = jnp.where(sub == i1, 1.0 / denom, jnp.where(sub == i2, e2 / denom, 0.0))


def _router(x, w_t):
    n, d = x.shape
    tm = _row_tile(n, 512)
    return pl.pallas_call(
        _router_kernel,
        grid=(n // tm,),
        in_specs=[pl.BlockSpec((tm, d), lambda i: (i, 0)), pl.BlockSpec((N_EXPERTS, d), lambda i: (0, 0))],
        out_specs=pl.BlockSpec((N_EXPERTS, tm), lambda i: (0, i)),
        out_shape=jax.ShapeDtypeStruct((N_EXPERTS, n), F32),
        compiler_params=_cparams(("arbitrary",), 32),
        name="router",
    )(x, w_t)


MOE_ROWS = 128
MOE_CHUNK_BLOCKS = 2


def _moe_kernel(cnt_ref, xb_ref, gt_ref, wg_ref, wu_ref, wd_ref, o_ref,
                rank_ref, xg_ref, acc_ref, upper_ref, *, ts, nsub, arms):
    i, e, f = pl.program_id(0), pl.program_id(1), pl.program_id(2)
    first_f = f == 0
    last_f = f == pl.num_programs(2) - 1

    @pl.when(jnp.logical_and(i == 0, jnp.logical_and(e == 0, first_f)))
    def _():
        upper_ref[...] = _strict_upper(ts)

    @pl.when(jnp.logical_and(e == 0, first_f))
    def _():
        o_ref[...] = jnp.zeros_like(o_ref)
        for sub in range(nsub):
            chosen = jnp.where(gt_ref[:, sub * ts:(sub + 1) * ts] > 0.0, 1.0, 0.0).astype(BF16)
            rank_ref[sub] = _dot(chosen, upper_ref[...])

    for sub in range(nsub):
        tok = slice(sub * ts, (sub + 1) * ts)
        nb = (cnt_ref[(i * nsub + sub) * N_EXPERTS + e] + MOE_ROWS - 1) // MOE_ROWS
        ge = gt_ref[pl.ds(e, 1), tok]
        slot = jnp.where(ge > 0.0, rank_ref[sub, pl.ds(e, 1), :], -1.0)

        def one_hot(start, rows, slot=slot):
            r = (start + lax.broadcasted_iota(I32, (rows, 1), 0)).astype(F32)
            return slot == r

        def for_arms(fn, nb=nb):
            big = arms[0]

            def chunk(t, carry):
                fn(pl.multiple_of(t * (big * MOE_ROWS), big * MOE_ROWS), big * MOE_ROWS)
                return carry
            lax.fori_loop(0, nb // big, chunk, 0)
            for size in arms[1:]:
                start = pl.multiple_of((nb & ~(2 * size - 1)) * MOE_ROWS, MOE_ROWS)

                @pl.when((nb & size) != 0)
                def _():
                    fn(start, size * MOE_ROWS)

        @pl.when(first_f)
        def _():
            def gather(start, rows):
                hot = jnp.where(one_hot(start, rows), 1.0, 0.0).astype(BF16)
                xg_ref[sub, pl.ds(start, rows), :] = _dot(hot, xb_ref[tok, :]).astype(BF16)
                acc_ref[sub, pl.ds(start, rows), :] = jnp.zeros((rows, acc_ref.shape[2]), F32)
            for_arms(gather)

        def expert(start, rows):
            xs = xg_ref[sub, pl.ds(start, rows), :]
            hg = _dot(xs, wg_ref[0])
            hu = _dot(xs, wu_ref[0])
            h = (hg * _sigmoid(hg) * hu).astype(BF16)
            acc_ref[sub, pl.ds(start, rows), :] += _dot(h, wd_ref[0])
        for_arms(expert)

        @pl.when(last_f)
        def _():
            def scatter(start, rows):
                hot = one_hot(start, rows)
                gate = jnp.sum(jnp.where(hot, ge, 0.0), axis=1, keepdims=True)
                v = (gate * acc_ref[sub, pl.ds(start, rows), :]).astype(BF16)
                o_ref[tok, :] += _dot_tn(jnp.where(hot, 1.0, 0.0).astype(BF16), v)
            for_arms(scatter)


def _moe(xb, gates_t, wg, wu, wd, *, tf=896):
    n, d = xb.shape
    ff = wg.shape[2]
    assert ff % tf == 0
    ts = _row_tile(n, 1024)
    nsub = 2 if n % (2 * ts) == 0 else 1
    tm = ts * nsub
    assert ts % MOE_ROWS == 0
    chunk_blocks = min(MOE_CHUNK_BLOCKS, ts // MOE_ROWS)
    arms = tuple(s for s in (8, 4, 2, 1) if s <= chunk_blocks)
    assert arms[0] == chunk_blocks
    counts = jnp.sum((gates_t > 0.0).reshape(N_EXPERTS, n // ts, ts), axis=2).astype(I32).T.reshape(-1)
    grid_spec = pltpu.PrefetchScalarGridSpec(
        num_scalar_prefetch=1, grid=(n // tm, N_EXPERTS, ff // tf),
        in_specs=[pl.BlockSpec((tm, d), lambda i, e, f, c: (i, 0)),
                  pl.BlockSpec((N_EXPERTS, tm), lambda i, e, f, c: (0, i)),
                  pl.BlockSpec((1, d, tf), lambda i, e, f, c: (e, 0, f)),
                  pl.BlockSpec((1, d, tf), lambda i, e, f, c: (e, 0, f)),
                  pl.BlockSpec((1, tf, d), lambda i, e, f, c: (e, f, 0))],
        out_specs=pl.BlockSpec((tm, d), lambda i, e, f, c: (i, 0)),
        scratch_shapes=[pltpu.VMEM((nsub, N_EXPERTS, ts), F32), pltpu.VMEM((nsub, ts, d), BF16),
                        pltpu.VMEM((nsub, ts, d), F32), pltpu.VMEM((ts, ts), BF16)])
    return pl.pallas_call(
        functools.partial(_moe_kernel, ts=ts, nsub=nsub, arms=arms),
        grid_spec=grid_spec,
        out_shape=jax.ShapeDtypeStruct((n, d), F32),
        compiler_params=_cparams(("arbitrary", "arbitrary", "arbitrary"), 60),
        name="moe",
    )(counts, xb, gates_t, wg, wu, wd)


def _pad_cols(w, cols):
    return jnp.pad(w, ((0, 0), (0, cols - w.shape[1])))


def _gla_layer(x, s0, w_in, w_up, b_a, norm_g, w_out, ln_g, ln_b, *, alpha):
    bsz, t, d = x.shape
    x2 = x.reshape(bsz * t, d)
    hf = _proj(x2, _pad_cols(w_in, GLA_COLS).astype(BF16), tn=640, bf16_copy=False)
    hf = hf.reshape(bsz, t, GLA_COLS)
    c = GLA_CHUNK if t % GLA_CHUNK == 0 else t
    valid = c
    if c < GLA_CHUNK:
        hf = jnp.pad(hf, ((0, 0), (0, GLA_CHUNK - c), (0, 0)))
        c = GLA_CHUNK
    w_up_pad = jnp.pad(w_up, ((0, LANE - GLA_GATE_RANK), (0, 0))).astype(BF16)
    o, st = _gla_core(hf, jnp.swapaxes(s0, 2, 3), w_up_pad, b_a.reshape(1, GLA_HK), norm_g.reshape(1, GLA_HV),
                      c=c, valid=valid)
    o = o[:, :t].reshape(bsz * t, GLA_HV)
    y, yb = _mm_res_ln(x2, o, w_out.astype(BF16), ln_g.reshape(1, d), ln_b.reshape(1, d), alpha=alpha)
    return y.reshape(bsz, t, d), yb, jnp.swapaxes(st, 2, 3)


def _dsa_weight(w_in):
    hq = DSA_HEADS * DSA_HEAD_DIM
    iq0 = hq + 2 * DSA_HKV
    ik0 = iq0 + IDX_HEADS * IDX_DIM
    w_iq = w_in[:, iq0:ik0].reshape(-1, IDX_HEADS, IDX_DIM)
    w_iq = jnp.pad(w_iq, ((0, 0), (0, 0), (0, LANE - IDX_DIM))).reshape(-1, IDX_HEADS * LANE)
    w_v = w_in[:, hq + DSA_HKV:iq0].reshape(-1, DSA_KV_HEADS, DSA_HEAD_DIM)
    w_v1 = jnp.pad(w_v, ((0, 0), (0, 0), (0, DSA_HEAD_DIM))).reshape(-1, 2 * DSA_HKV)
    w = jnp.concatenate([w_in[:, :hq], w_iq, w_in[:, hq:iq0], w_v1, _pad_cols(w_in[:, ik0:], LANE)], axis=1)
    return w.astype(BF16)


def _dsa_project(x, w_in):
    bsz, t, d = x.shape
    hf, hb = _proj(x.reshape(bsz * t, d), _dsa_weight(w_in), tn=DSA_COLS // 5, ones_at=DSA_ONES)
    hf = hf.reshape(bsz, t, DSA_COLS)
    k = hf[:, :, DSA_K0:DSA_V0].reshape(bsz, t, DSA_KV_HEADS, DSA_HEAD_DIM)
    v = hf[:, :, DSA_V0:DSA_V1_0].reshape(bsz, t, DSA_KV_HEADS, DSA_HEAD_DIM)
    ik = hf[:, :, DSA_IKW0:DSA_IKW0 + IDX_DIM]
    return hf, hb.reshape(bsz, t, DSA_COLS), k, v, ik


def kernel(x_prompt, x_sample, state_gla, cache_k, cache_v, cache_idx_k, page_table, p_prompt, p_sample,
           ln_mix_g, ln_mix_b, ln_ffn_g, ln_ffn_b, w_in_gla, w_alpha_up, b_alpha, gla_norm_g, w_out_gla,
           w_in_dsa, w_out_dsa, w_ffn_gate, w_ffn_up, w_ffn_down, w_router, w_exp_gate, w_exp_up,
           w_exp_down, w_ple_proj, w_ple_gate):
    depth = ln_mix_g.shape[0]
    alpha = (2.0 * depth) ** 0.25
    d = x_prompt.shape[-1]
    xp, xs = x_prompt, x_sample
    gla_p, gla_s, kp, vp, ikp, ksm, vsm, iksm = [], [], [], [], [], [], [], []
    n_pages = page_table.shape[1]
    pp = 32 if n_pages % 32 == 0 else n_pages
    for i in range(depth):
        j = i // 2
        lg, lb = ln_mix_g[i], ln_mix_b[i]
        if i % 2 == 0:
            gw = (w_in_gla[j], w_alpha_up[j], b_alpha[j], gla_norm_g[j], w_out_gla[j], lg, lb)
            s0 = jnp.zeros((xp.shape[0], GLA_HEADS, GLA_DK, GLA_DV), F32)
            xp, xpb, sp = _gla_layer(xp, s0, *gw, alpha=alpha)
            xs, xsb, ss = _gla_layer(xs, state_gla[j], *gw, alpha=alpha)
            gla_p.append(sp)
            gla_s.append(ss)
        else:
            w_out = w_out_dsa[j].astype(BF16)
            hf, hb, k_p, v_p, ik_p = _dsa_project(xp, w_in_dsa[j])
            op = _dsa_prompt(hf, hb)
            bp, tp, _ = xp.shape
            xp, xpb = _mm_res_ln(xp.reshape(bp * tp, d), op.reshape(bp * tp, -1), w_out, lg.reshape(1, d),
                                 lb.reshape(1, d), alpha=alpha)
            xp = xp.reshape(bp, tp, d)
            hf, hb, k_s, v_s, ik_s = _dsa_project(xs, w_in_dsa[j])
            sc = _dsa_sample_scores(hf, jnp.swapaxes(cache_idx_k[j], 1, 2), page_table, pp=pp)
            sel = _dsa_sample_select(sc, hf)
            n_pool = cache_k.shape[1]
            rows = PAGE_SIZE * DSA_KV_HEADS
            os_ = _dsa_sample_attend(hf, sel, cache_k[j].reshape(n_pool, rows, DSA_HEAD_DIM),
                                     cache_v[j].reshape(n_pool, rows, DSA_HEAD_DIM), page_table, pp=pp)
            bs, ts, _ = xs.shape
            xs, xsb = _mm_res_ln(xs.reshape(bs * ts, d), os_.reshape(bs * ts, -1), w_out, lg.reshape(1, d),
                                 lb.reshape(1, d), alpha=alpha)
            xs = xs.reshape(bs, ts, d)
            kp.append(k_p)
            vp.append(v_p)
            ikp.append(ik_p)
            ksm.append(k_s)
            vsm.append(v_s)
            iksm.append(ik_s)
        fg, fb = ln_ffn_g[i].reshape(1, d), ln_ffn_b[i].reshape(1, d)
        outs = []
        wpg, wpp = w_ple_gate[i].astype(BF16), w_ple_proj[i].astype(BF16)
        for x, xb, p in ((xp, xpb, p_prompt[i]), (xs, xsb, p_sample[i])):
            bsz, t, _ = x.shape
            x2 = x.reshape(bsz * t, d)
            p2 = p.reshape(bsz * t, -1)
            if i % 2 == 0:
                y2 = _ffn_ln(x2, w_ffn_gate[j].astype(BF16), w_ffn_up[j].astype(BF16), w_ffn_down[j].astype(BF16),
                             fg, fb, alpha=alpha)
                y2 = _ple(y2, p2, wpg, wpp)
            else:
                gates = _router(x2, w_router[j].T)
                f2 = _moe(xb, gates, w_exp_gate[j].astype(BF16), w_exp_up[j].astype(BF16),
                          w_exp_down[j].astype(BF16))
                y2 = _res_ln_ple(x2, f2, p2, fg, fb, wpg, wpp, alpha=alpha)
            outs.append(y2.reshape(bsz, t, d))
        xp, xs = outs
    return (xp, xs, jnp.stack(gla_p), jnp.stack(gla_s), jnp.stack(kp), jnp.stack(vp), jnp.stack(ikp),
            jnp.stack(ksm), jnp.stack(vsm), jnp.stack(iksm))
```

```python
import functools

import jax
import jax.numpy as jnp
from jax import lax
from jax.experimental import pallas as pl
from jax.experimental.pallas import tpu as pltpu

F32 = jnp.float32
BF16 = jnp.bfloat16
I32 = jnp.int32

D_MODEL = 1024
LN_EPS = 1e-5
LANE = 128

GLA_HEADS = 4
GLA_DK = 128
GLA_DV = 256
GLA_GATE_RANK = 16
GLA_TAU = 16.0
GLA_CHUNK = 64
GLA_HK = GLA_HEADS * GLA_DK
GLA_HV = GLA_HEADS * GLA_DV
GLA_COLS = 2 * GLA_HK + 2 * GLA_HV + LANE

DSA_HEADS = 8
DSA_HEAD_DIM = 128
DSA_KV_HEADS = 2
DSA_GROUP = DSA_HEADS // DSA_KV_HEADS
DSA_HKV = DSA_KV_HEADS * DSA_HEAD_DIM
IDX_HEADS = 8
IDX_DIM = 64
TOPK_MAX = 256
PAGE_SIZE = 128
DSA_Q0 = 0
DSA_IQ0 = DSA_HEADS * DSA_HEAD_DIM
DSA_K0 = DSA_IQ0 + IDX_HEADS * LANE
DSA_V0 = DSA_K0 + DSA_HKV
DSA_V1_0 = DSA_V0 + DSA_HKV
DSA_IKW0 = DSA_V1_0 + 2 * DSA_HKV
DSA_COLS = DSA_IKW0 + LANE
DSA_ONES = tuple(DSA_V1_0 + (2 * c + 1) * DSA_HEAD_DIM for c in range(DSA_KV_HEADS))

N_EXPERTS = 8

NEG_INF_KEY = -2139095041
INT_MIN = -2147483648
MASKED = -1e30


def _cparams(sem, vmem_mb):
    return pltpu.CompilerParams(dimension_semantics=sem, vmem_limit_bytes=vmem_mb * 2 ** 20)


def _dot(a, b):
    return jnp.dot(a, b, preferred_element_type=F32)


def _dot_nt(a, b):
    return lax.dot_general(a, b, (((1,), (1,)), ((), ())), preferred_element_type=F32)


def _dot_tn(a, b):
    return lax.dot_general(a, b, (((0,), (0,)), ((), ())), preferred_element_type=F32)


def _sigmoid(x):
    return 1.0 / (1.0 + jnp.exp(-x))


def _layer_norm(y, g, b):
    mu = jnp.mean(y, axis=-1, keepdims=True)
    yc = y - mu
    var = jnp.mean(yc * yc, axis=-1, keepdims=True)
    return yc * lax.rsqrt(var + LN_EPS) * g + b


def _row_tile(n, pref):
    t = min(n, pref)
    assert n % t == 0
    return t


def _proj_kernel(x_ref, w_ref, o_ref, ob_ref, *, tn, ones_at):
    xb = x_ref[...].astype(BF16)
    for c in range(w_ref.shape[1] // tn):
        r = _dot(xb, w_ref[:, c * tn:(c + 1) * tn])
        o_ref[:, c * tn:(c + 1) * tn] = r
        if ob_ref is not None:
            ob_ref[:, c * tn:(c + 1) * tn] = r.astype(BF16)
    for c0 in ones_at:
        ob_ref[:, c0:c0 + LANE] = jnp.ones((x_ref.shape[0], LANE), BF16)


def _proj_f32_kernel(x_ref, w_ref, o_ref, *, tn):
    _proj_kernel(x_ref, w_ref, o_ref, None, tn=tn, ones_at=())


def _proj(x, w, *, tn, ones_at=(), bf16_copy=True):
    n, d = x.shape
    cols = w.shape[1]
    assert cols % tn == 0 and (bf16_copy or not ones_at)
    tm = _row_tile(n, 512)
    out_spec = pl.BlockSpec((tm, cols), lambda i: (i, 0))
    if bf16_copy:
        body = functools.partial(_proj_kernel, tn=tn, ones_at=ones_at)
        out_specs = [out_spec, out_spec]
        out_shape = [jax.ShapeDtypeStruct((n, cols), F32), jax.ShapeDtypeStruct((n, cols), BF16)]
    else:
        body = functools.partial(_proj_f32_kernel, tn=tn)
        out_specs, out_shape = out_spec, jax.ShapeDtypeStruct((n, cols), F32)
    return pl.pallas_call(
        body,
        grid=(n // tm,),
        in_specs=[pl.BlockSpec((tm, d), lambda i: (i, 0)), pl.BlockSpec((d, cols), lambda i: (0, 0))],
        out_specs=out_specs,
        out_shape=out_shape,
        compiler_params=_cparams(("arbitrary",), 56),
        name="proj",
    )(x, w)


def _mm_res_ln_kernel(x_ref, a_ref, w_ref, g_ref, b_ref, o_ref, ob_ref, *, alpha):
    m = _dot(a_ref[...].astype(BF16), w_ref[...])
    y = _layer_norm(alpha * x_ref[...] + m, g_ref[...], b_ref[...])
    o_ref[...] = y
    ob_ref[...] = y.astype(BF16)


def _mm_res_ln(x, a, w, g, b, *, alpha):
    n, d = x.shape
    ka = a.shape[1]
    tm = _row_tile(n, 512)
    row = lambda i: (i, 0)
    fixed = lambda i: (0, 0)
    return pl.pallas_call(
        functools.partial(_mm_res_ln_kernel, alpha=alpha),
        grid=(n // tm,),
        in_specs=[pl.BlockSpec((tm, d), row), pl.BlockSpec((tm, ka), row), pl.BlockSpec((ka, d), fixed),
                  pl.BlockSpec((1, d), fixed), pl.BlockSpec((1, d), fixed)],
        out_specs=[pl.BlockSpec((tm, d), row), pl.BlockSpec((tm, d), row)],
        out_shape=[jax.ShapeDtypeStruct((n, d), F32), jax.ShapeDtypeStruct((n, d), BF16)],
        compiler_params=_cparams(("arbitrary",), 40),
        name="mm_res_ln",
    )(x, a, w, g, b)


def _ffn_ln_ple_kernel(x_ref, wg_ref, wu_ref, wd_ref, g_ref, b_ref, p_ref, wpg_ref, wpp_ref, o_ref, xb_ref, acc_ref,
                       *, alpha):
    f = pl.program_id(1)

    @pl.when(f == 0)
    def _():
        xb_ref[...] = x_ref[...].astype(BF16)
        acc_ref[...] = jnp.zeros_like(acc_ref)

    xb = xb_ref[...]
    hg = _dot(xb, wg_ref[...])
    hu = _dot(xb, wu_ref[...])
    h = (hg * _sigmoid(hg) * hu).astype(BF16)
    acc_ref[...] += _dot(h, wd_ref[...])

    @pl.when(f == pl.num_programs(1) - 1)
    def _():
        z = _layer_norm(alpha * x_ref[...] + acc_ref[...], g_ref[...], b_ref[...])
        gate = _sigmoid(_dot(z.astype(BF16), wpg_ref[...]))
        o_ref[...] = z + gate * _dot(p_ref[...].astype(BF16), wpp_ref[...])


def _ffn_ln_ple(x, wg, wu, wd, g, b, p, wpg, wpp, *, alpha, tf=512):
    n, d = x.shape
    ff = wg.shape[1]
    pd = p.shape[1]
    assert ff % tf == 0
    tm = _row_tile(n, 1024)
    fixed = lambda i, f: (0, 0)
    return pl.pallas_call(
        functools.partial(_ffn_ln_ple_kernel, alpha=alpha),
        grid=(n // tm, ff // tf),
        in_specs=[pl.BlockSpec((tm, d), lambda i, f: (i, 0)),
                  pl.BlockSpec((d, tf), lambda i, f: (0, f)),
                  pl.BlockSpec((d, tf), lambda i, f: (0, f)),
                  pl.BlockSpec((tf, d), lambda i, f: (f, 0)),
                  pl.BlockSpec((1, d), fixed),
                  pl.BlockSpec((1, d), fixed),
                  pl.BlockSpec((tm, pd), lambda i, f: (i, 0)),
                  pl.BlockSpec((d, d), fixed),
                  pl.BlockSpec((pd, d), fixed)],
        out_specs=pl.BlockSpec((tm, d), lambda i, f: (i, 0)),
        out_shape=jax.ShapeDtypeStruct((n, d), F32),
        scratch_shapes=[pltpu.VMEM((tm, d), BF16), pltpu.VMEM((tm, d), F32)],
        compiler_params=_cparams(("arbitrary", "arbitrary"), 56),
        name="ffn_ln_ple",
    )(x, wg, wu, wd, g, b, p, wpg, wpp)


def _res_ln_ple_kernel(x_ref, f_ref, p_ref, g_ref, b_ref, wg_ref, wp_ref, o_ref, *, alpha):
    z = _layer_norm(alpha * x_ref[...] + f_ref[...], g_ref[...], b_ref[...])
    gate = _sigmoid(_dot(z.astype(BF16), wg_ref[...]))
    o_ref[...] = z + gate * _dot(p_ref[...].astype(BF16), wp_ref[...])


def _res_ln_ple(x, f, p, g, b, wg, wp, *, alpha):
    n, d = x.shape
    pd = p.shape[1]
    tm = _row_tile(n, 512)
    row = lambda i: (i, 0)
    fixed = lambda i: (0, 0)
    return pl.pallas_call(
        functools.partial(_res_ln_ple_kernel, alpha=alpha),
        grid=(n // tm,),
        in_specs=[pl.BlockSpec((tm, d), row), pl.BlockSpec((tm, d), row), pl.BlockSpec((tm, pd), row),
                  pl.BlockSpec((1, d), fixed), pl.BlockSpec((1, d), fixed),
                  pl.BlockSpec((d, d), fixed), pl.BlockSpec((pd, d), fixed)],
        out_specs=pl.BlockSpec((tm, d), row),
        out_shape=jax.ShapeDtypeStruct((n, d), F32),
        compiler_params=_cparams(("arbitrary",), 40),
        name="res_ln_ple",
    )(x, f, p, g, b, wg, wp)


def _gla_kernel(q_ref, k_ref, v_ref, g_ref, a_ref, wup_ref, ba_ref, ng_ref, s0_ref, o_ref, st_out_ref, st_ref,
                *, c, valid, nbs):
    t = pl.program_id(1)

    @pl.when(t == 0)
    def _():
        st_ref[...] = s0_ref[...]

    row = lax.broadcasted_iota(I32, (c, c), 0)
    col = lax.broadcasted_iota(I32, (c, c), 1)
    causal = row >= col
    for bi in range(nbs):
        logit = _dot(a_ref[bi].astype(BF16), wup_ref[...]) + ba_ref[...]
        log_a = (jnp.minimum(logit, 0.0) - jnp.log(1.0 + jnp.exp(-jnp.abs(logit)))) * (1.0 / GLA_TAU)
        if valid < c:
            log_a = jnp.where(lax.broadcasted_iota(I32, log_a.shape, 0) < valid, log_a, 0.0)
        bcum = jnp.dot(causal.astype(F32), log_a, preferred_element_type=F32, precision=lax.Precision.HIGHEST)
        blast = bcum[c - 1:c, :]
        q = q_ref[bi] * (GLA_DK ** -0.5)
        k = k_ref[bi]
        q_dec = (q * jnp.exp(bcum)).astype(BF16)
        k_inv = (k * jnp.exp(-bcum)).astype(BF16)
        k_dec = (k * jnp.exp(blast - bcum)).astype(BF16)
        decay = jnp.exp(blast)
        v = v_ref[bi].astype(BF16)
        for h in range(GLA_HEADS):
            sk = slice(h * GLA_DK, (h + 1) * GLA_DK)
            sv = slice(h * GLA_DV, (h + 1) * GLA_DV)
            att = jnp.where(causal, _dot_nt(q_dec[:, sk], k_inv[:, sk]), 0.0).astype(BF16)
            st = st_ref[bi, h]
            o = _dot(att, v[:, sv]) + _dot_nt(q_dec[:, sk], st.astype(BF16))
            st_ref[bi, h] = st * decay[:, sk] + _dot_tn(v[:, sv], k_dec[:, sk])
            mu = jnp.mean(o, axis=-1, keepdims=True)
            oc = o - mu
            var = jnp.mean(oc * oc, axis=-1, keepdims=True)
            on = oc * lax.rsqrt(var + LN_EPS) * ng_ref[:, sv]
            gg = g_ref[bi, :, sv]
            o_ref[bi, :, sv] = (on * (gg * _sigmoid(gg))).astype(BF16)

    @pl.when(t == pl.num_programs(1) - 1)
    def _():
        st_out_ref[...] = st_ref[...]


def _gla_core(h, s0_t, wup, ba, ng, *, c, valid):
    bsz, tlen, _ = h.shape
    assert tlen % c == 0
    nbs = 4 if bsz % 4 == 0 else 1
    v_blk = (2 * GLA_HK) // GLA_HV
    state_spec = pl.BlockSpec((nbs, GLA_HEADS, GLA_DV, GLA_DK), lambda b, t: (b, 0, 0, 0))
    return pl.pallas_call(
        functools.partial(_gla_kernel, c=c, valid=valid, nbs=nbs),
        grid=(bsz // nbs, tlen // c),
        in_specs=[pl.BlockSpec((nbs, c, GLA_HK), lambda b, t: (b, t, 0)),
                  pl.BlockSpec((nbs, c, GLA_HK), lambda b, t: (b, t, 1)),
                  pl.BlockSpec((nbs, c, GLA_HV), lambda b, t: (b, t, v_blk)),
                  pl.BlockSpec((nbs, c, GLA_HV), lambda b, t: (b, t, v_blk + 1)),
                  pl.BlockSpec((nbs, c, LANE), lambda b, t: (b, t, (2 * GLA_HK + 2 * GLA_HV) // LANE)),
                  pl.BlockSpec((LANE, GLA_HK), lambda b, t: (0, 0)),
                  pl.BlockSpec((1, GLA_HK), lambda b, t: (0, 0)),
                  pl.BlockSpec((1, GLA_HV), lambda b, t: (0, 0)),
                  state_spec],
        out_specs=[pl.BlockSpec((nbs, c, GLA_HV), lambda b, t: (b, t, 0)), state_spec],
        out_shape=[jax.ShapeDtypeStruct((bsz, tlen, GLA_HV), BF16),
                   jax.ShapeDtypeStruct((bsz, GLA_HEADS, GLA_DV, GLA_DK), F32)],
        scratch_shapes=[pltpu.VMEM((nbs, GLA_HEADS, GLA_DV, GLA_DK), F32)],
        compiler_params=_cparams(("arbitrary", "arbitrary"), 32),
        name="gla_core",
    )(h, h, h, h, h, wup, ba, ng, s0_t)


def _sortable(s):
    bits = lax.bitcast_convert_type(s, I32)
    return bits ^ (lax.shift_right_arithmetic(bits, 31) & 0x7FFFFFFF)


def _lane_fold(m):
    acc = m[:, 0:LANE]
    for i in range(1, m.shape[1] // LANE):
        acc = acc + m[:, i * LANE:(i + 1) * LANE]
    return acc


def _lane_fold_max(m):
    acc = m[:, 0:LANE]
    for i in range(1, m.shape[1] // LANE):
        acc = jnp.maximum(acc, m[:, i * LANE:(i + 1) * LANE])
    return acc


def _bias_bits(sel):
    return lax.bitcast_convert_type(jnp.where(sel, 0.0, MASKED).astype(F32), I32)


def _kth_largest_key(count_ge, rows, k):
    kf = float(k)
    thr0 = jnp.where(count_ge(jnp.zeros((rows, 1), I32)) >= kf, 0, INT_MIN).astype(I32)

    def bit_body(i, thr):
        cand = thr | lax.shift_left(jnp.int32(1), 30 - i)
        return jnp.where(count_ge(cand) >= kf, cand, thr)

    return lax.fori_loop(0, 31, bit_body, thr0)


def _strict_upper(n):
    return (lax.broadcasted_iota(I32, (n, n), 0) < lax.broadcasted_iota(I32, (n, n), 1)).astype(BF16)


def _dsa_prompt_kernel(q_ref, iq_ref, iw_ref, ikw_ref, k_ref, v1_ref, o_ref,
                       keys_ref, top2_ref, iqs_ref, qg_ref, d_ref, m_ref, acc_ref, *, tq, tk, topk):
    j = pl.program_id(1)
    nkt = ((j + 1) * tq + tk - 1) // tk
    gq = DSA_GROUP * tq
    rp = 16
    for h in range(IDX_HEADS):
        iqs_ref[h * tq:(h + 1) * tq, :] = iq_ref[0, :, h * LANE:(h + 1) * LANE]
    for c in range(DSA_KV_HEADS):
        for g in range(DSA_GROUP):
            hh = c * DSA_GROUP + g
            qg_ref[c, g * tq:(g + 1) * tq, :] = q_ref[0, :, hh * DSA_HEAD_DIM:(hh + 1) * DSA_HEAD_DIM]
    ws = iw_ref[0][:, IDX_DIM:IDX_DIM + IDX_HEADS] * (IDX_HEADS ** -0.5) * (IDX_DIM ** -0.5)

    def score_tile(kt, diagonal):
        ikt = ikw_ref[0, pl.ds(pl.multiple_of(kt * tk, tk), tk), :]
        d_ref[...] = _dot_nt(iqs_ref[...], ikt)
        for r in range(tq // rp):
            s = jnp.zeros((rp, tk), F32)
            for h in range(IDX_HEADS):
                dh = d_ref[h * tq + r * rp:h * tq + (r + 1) * rp, :]
                s = s + ws[r * rp:(r + 1) * rp, h:h + 1] * jnp.maximum(dh, 0.0)
            key = _sortable(s + 0.0)
            if diagonal:
                qpos = j * tq + r * rp + lax.broadcasted_iota(I32, (rp, tk), 0)
                kpos = kt * tk + lax.broadcasted_iota(I32, (rp, tk), 1)
                key = jnp.where(kpos <= qpos, key, INT_MIN)
            keys_ref[kt, r * rp:(r + 1) * rp, :] = key
            rows = slice(r * rp, (r + 1) * rp)
            m1, m2 = top2_ref[rows, 0:LANE], top2_ref[rows, LANE:2 * LANE]
            for lt in range(tk // LANE):
                x = key[:, lt * LANE:(lt + 1) * LANE]
                m2 = jnp.maximum(m2, jnp.minimum(m1, x))
                m1 = jnp.maximum(m1, x)
            top2_ref[rows, :] = jnp.concatenate([m1, m2], axis=1)

    def score_body(kt, carry):
        score_tile(kt, False)
        return carry

    top2_ref[...] = jnp.full(top2_ref.shape, INT_MIN, I32)
    lax.fori_loop(0, nkt - 1, score_body, 0)
    score_tile(nkt - 1, True)

    def count_ge(t):
        def body(kt, acc):
            return acc + _lane_fold(jnp.where(keys_ref[kt] >= t, 1.0, 0.0))
        return jnp.sum(lax.fori_loop(0, nkt, body, jnp.zeros((tq, LANE), F32)), axis=1, keepdims=True)

    kf = float(topk)
    lo0 = jnp.min(top2_ref[:, LANE:2 * LANE], axis=1, keepdims=True)
    hi0 = jnp.max(top2_ref[:, 0:LANE], axis=1, keepdims=True) + 1

    def search_done(lo, hi, c_lo):
        row_done = jnp.logical_or(c_lo == kf, hi == lo + 1)
        return jnp.min(jnp.where(row_done, 1.0, 0.0)) > 0.0

    def search_cond(state):
        return jnp.logical_not(state[4])

    def search_body(state):
        lo, hi, c_lo, c_hi, _ = state
        for _ in range(4):
            mid = lax.shift_right_arithmetic(lo, 1) + lax.shift_right_arithmetic(hi, 1) + (lo & hi & 1)
            c = count_ge(mid)
            ok = c >= kf
            lo, c_lo = jnp.where(ok, mid, lo), jnp.where(ok, c, c_lo)
            hi, c_hi = jnp.where(ok, hi, mid), jnp.where(ok, c_hi, c)
        return lo, hi, c_lo, c_hi, search_done(lo, hi, c_lo)

    c_lo0 = count_ge(lo0)
    thr, _, c_thr, c_above, _ = lax.while_loop(
        search_cond, search_body, (lo0, hi0, c_lo0, jnp.zeros((tq, 1), F32), search_done(lo0, hi0, c_lo0)))
    need = kf - c_above
    tie_row = jnp.logical_and(c_thr > kf, thr > INT_MIN)
    any_tie = jnp.max(jnp.where(tie_row, 1.0, 0.0)) > 0.0

    @pl.when(jnp.logical_not(any_tie))
    def _():
        thr_eff = jnp.maximum(thr, INT_MIN + 1)

        def body(kt, carry):
            keys_ref[kt] = _bias_bits(keys_ref[kt] >= thr_eff)
            return carry
        lax.fori_loop(0, nkt, body, 0)

    @pl.when(any_tie)
    def _():
        upper = _strict_upper(tk)

        def body(kt, seen):
            kk = keys_ref[kt]
            eq = kk == thr
            before = seen + _dot(jnp.where(eq, 1.0, 0.0).astype(BF16), upper)
            sel = jnp.logical_and(kk > INT_MIN, jnp.logical_or(kk > thr, jnp.logical_and(eq, before < need)))
            keys_ref[kt] = _bias_bits(sel)
            return seen + jnp.sum(jnp.where(eq, 1.0, 0.0), axis=1, keepdims=True)
        lax.fori_loop(0, nkt, body, jnp.zeros((tq, 1), F32))

    m_ref[...] = jnp.full(m_ref.shape, MASKED, F32)
    acc_ref[...] = jnp.zeros_like(acc_ref)
    c2 = (DSA_HEAD_DIM ** -0.5) * 1.4426950408889634
    dh = DSA_HEAD_DIM

    def attend_tile(kt):
        rows = pl.ds(pl.multiple_of(kt * tk, tk), tk)
        bias = lax.bitcast_convert_type(keys_ref[kt], F32)[None]
        for c in range(DSA_KV_HEADS):
            kc = k_ref[0, rows, c * dh:(c + 1) * dh]
            v1 = v1_ref[0, rows, c * 2 * dh:(c + 1) * 2 * dh]
            x = (_dot_nt(qg_ref[c], kc).reshape(DSA_GROUP, tq, tk) + bias).reshape(gq, tk)
            m_old = m_ref[c]
            m_new = jnp.maximum(m_old, jnp.max(x, axis=1, keepdims=True))
            p = jnp.exp2((x - jnp.concatenate([m_new] * (tk // LANE), axis=1)) * c2).astype(BF16)
            alpha = jnp.exp2((m_old - m_new) * c2)
            acc_ref[c] = jnp.concatenate([alpha, alpha], axis=1) * acc_ref[c] + _dot(p, v1)
            m_ref[c] = m_new

    def attend_pair(i, carry):
        attend_tile(2 * i)
        attend_tile(2 * i + 1)
        return carry

    lax.fori_loop(0, nkt // 2, attend_pair, 0)

    @pl.when(nkt % 2 == 1)
    def _():
        attend_tile(nkt - 1)

    for c in range(DSA_KV_HEADS):
        oc = acc_ref[c, :, 0:dh] / acc_ref[c, :, dh:2 * dh]
        for g in range(DSA_GROUP):
            hh = c * DSA_GROUP + g
            o_ref[0, :, hh * dh:(hh + 1) * dh] = oc[g * tq:(g + 1) * tq].astype(BF16)


def _dsa_prompt(hf, hb, *, tq=256, tk=1024):
    bsz, s, _ = hf.shape
    tk = min(tk, s)
    assert s % tq == 0 and s % tk == 0 and tk % tq == 0
    topk = min(TOPK_MAX, s // 4)
    assert topk <= 2 * LANE
    hq = DSA_HEADS * DSA_HEAD_DIM
    gq = DSA_GROUP * tq
    return pl.pallas_call(
        functools.partial(_dsa_prompt_kernel, tq=tq, tk=tk, topk=topk),
        grid=(bsz, s // tq),
        in_specs=[pl.BlockSpec((1, tq, hq), lambda b, j: (b, j, DSA_Q0 // hq)),
                  pl.BlockSpec((1, tq, IDX_HEADS * LANE), lambda b, j: (b, j, DSA_IQ0 // (IDX_HEADS * LANE))),
                  pl.BlockSpec((1, tq, LANE), lambda b, j: (b, j, DSA_IKW0 // LANE)),
                  pl.BlockSpec((1, s, LANE), lambda b, j: (b, 0, DSA_IKW0 // LANE), pipeline_mode=pl.Buffered(1)),
                  pl.BlockSpec((1, s, DSA_HKV), lambda b, j: (b, 0, DSA_K0 // DSA_HKV),
                               pipeline_mode=pl.Buffered(1)),
                  pl.BlockSpec((1, s, 2 * DSA_HKV), lambda b, j: (b, 0, DSA_V1_0 // (2 * DSA_HKV)),
                               pipeline_mode=pl.Buffered(1))],
        out_specs=pl.BlockSpec((1, tq, hq), lambda b, j: (b, j, 0)),
        out_shape=jax.ShapeDtypeStruct((bsz, s, hq), BF16),
        scratch_shapes=[pltpu.VMEM((s // tk, tq, tk), I32),
                        pltpu.VMEM((tq, 2 * LANE), I32),
                        pltpu.VMEM((IDX_HEADS * tq, LANE), BF16),
                        pltpu.VMEM((DSA_KV_HEADS, gq, DSA_HEAD_DIM), BF16),
                        pltpu.VMEM((IDX_HEADS * tq, tk), F32),
                        pltpu.VMEM((DSA_KV_HEADS, gq, LANE), F32),
                        pltpu.VMEM((DSA_KV_HEADS, gq, 2 * DSA_HEAD_DIM), F32)],
        compiler_params=_cparams(("arbitrary", "arbitrary"), 56),
        name="dsa_prompt",
    )(hb, hb, hf, hb, hb, hb)


def _sample_ws(iw_ref):
    return iw_ref[0][:, IDX_DIM:IDX_DIM + IDX_HEADS] * (IDX_HEADS ** -0.5)


def _sample_scores(d, ws, t):
    s = jnp.zeros((t, d.shape[1]), F32)
    for h in range(IDX_HEADS):
        s = s + ws[:, h:h + 1] * jnp.maximum(d[h * t:(h + 1) * t] * (IDX_DIM ** -0.5), 0.0)
    return s + 0.0


def _dsa_sample_scores_kernel(pt_ref, iq_ref, iw_ref, *refs, pp, t):
    page_refs, o_ref = refs[:pp], refs[pp]
    iq_stack = jnp.concatenate([iq_ref[0, :, h * LANE:h * LANE + IDX_DIM] for h in range(IDX_HEADS)], axis=0)
    ik_t = jnp.concatenate([r[0].astype(BF16) for r in page_refs], axis=1)
    o_ref[0] = _sample_scores(_dot(iq_stack.astype(BF16), ik_t), _sample_ws(iw_ref), t)


def _dsa_sample_scores(hf, cache_ik_t, page_table, *, pp):
    bsz, t, _ = hf.shape
    n_pages = page_table.shape[1]
    assert n_pages % pp == 0
    iq_w = IDX_HEADS * LANE

    def page_spec(i):
        return pl.BlockSpec((1, IDX_DIM, PAGE_SIZE), lambda b, j, pt: (pt[b, j * pp + i], 0, 0))

    grid_spec = pltpu.PrefetchScalarGridSpec(
        num_scalar_prefetch=1, grid=(bsz, n_pages // pp),
        in_specs=[pl.BlockSpec((1, t, iq_w), lambda b, j, pt: (b, 0, DSA_IQ0 // iq_w)),
                  pl.BlockSpec((1, t, LANE), lambda b, j, pt: (b, 0, DSA_IKW0 // LANE))]
                 + [page_spec(i) for i in range(pp)],
        out_specs=pl.BlockSpec((1, t, pp * PAGE_SIZE), lambda b, j, pt: (b, 0, j)))
    return pl.pallas_call(
        functools.partial(_dsa_sample_scores_kernel, pp=pp, t=t),
        grid_spec=grid_spec,
        out_shape=jax.ShapeDtypeStruct((bsz, t, n_pages * PAGE_SIZE), F32),
        compiler_params=_cparams(("arbitrary", "arbitrary"), 32),
        name="dsa_sample_scores",
    )(page_table, hf, hf, *([cache_ik_t] * pp))


def _dsa_sample_select_kernel(sp_ref, iq_ref, iw_ref, ikw_ref, o_ref, keys_ref, *, t, past, new_w, topk, ck):
    iq_stack = jnp.concatenate([iq_ref[0, :, h * LANE:(h + 1) * LANE] for h in range(IDX_HEADS)], axis=0)
    ik_new = jnp.concatenate([ikw_ref[0], jnp.zeros((new_w - t, LANE), F32)], axis=0)
    d_new = _dot_nt(iq_stack.astype(BF16), ik_new.astype(BF16))
    s_new = _sample_scores(d_new, _sample_ws(iw_ref), t)
    row = lax.broadcasted_iota(I32, (t, new_w), 0)
    lane = lax.broadcasted_iota(I32, (t, new_w), 1)
    s_new = jnp.where(lane <= row, s_new, -jnp.inf)
    keys_ref[:, 0:past] = _sortable(sp_ref[0])
    keys_ref[:, past:past + new_w] = _sortable(s_new)
    n_ck = (past + new_w) // ck

    def counts(pred):
        acc = jnp.zeros((t, LANE), F32)
        for i in range(n_ck):
            acc = acc + _lane_fold(jnp.where(pred(keys_ref[:, i * ck:(i + 1) * ck]), 1.0, 0.0))
        return jnp.sum(acc, axis=1, keepdims=True)

    thr = _kth_largest_key(lambda cand: counts(lambda kk: kk >= cand), t, topk)
    n_gt = counts(lambda kk: kk > thr)
    n_eq = counts(lambda kk: kk == thr)
    need = float(topk) - n_gt
    tie_row = jnp.logical_and(n_eq > need, thr > NEG_INF_KEY)
    any_tie = jnp.max(jnp.where(tie_row, 1.0, 0.0)) > 0.0

    @pl.when(jnp.logical_not(any_tie))
    def _():
        thr_eff = jnp.maximum(thr, NEG_INF_KEY + 1)
        o_ref[0] = jnp.where(keys_ref[...] >= thr_eff, 1.0, 0.0)

    @pl.when(any_tie)
    def _():
        upper = _strict_upper(ck)
        seen = jnp.zeros((t, 1), F32)
        for i in range(n_ck):
            kk = keys_ref[:, i * ck:(i + 1) * ck]
            eq = kk == thr
            before = seen + _dot(jnp.where(eq, 1.0, 0.0).astype(BF16), upper)
            sel = jnp.logical_and(kk > NEG_INF_KEY, jnp.logical_or(kk > thr, jnp.logical_and(eq, before < need)))
            o_ref[0, :, i * ck:(i + 1) * ck] = jnp.where(sel, 1.0, 0.0)
            seen = seen + jnp.sum(jnp.where(eq, 1.0, 0.0), axis=1, keepdims=True)


def _dsa_sample_select(scores_past, hf, *, new_w=512, ck=512):
    bsz, t, past = scores_past.shape
    assert past % ck == 0 and new_w % ck == 0
    topk = min(TOPK_MAX, (past + t) // 4)
    iq_w = IDX_HEADS * LANE
    total = past + new_w
    return pl.pallas_call(
        functools.partial(_dsa_sample_select_kernel, t=t, past=past, new_w=new_w, topk=topk, ck=ck),
        grid=(bsz,),
        in_specs=[pl.BlockSpec((1, t, past), lambda b: (b, 0, 0)),
                  pl.BlockSpec((1, t, iq_w), lambda b: (b, 0, DSA_IQ0 // iq_w)),
                  pl.BlockSpec((1, t, LANE), lambda b: (b, 0, DSA_IKW0 // LANE)),
                  pl.BlockSpec((1, t, LANE), lambda b: (b, 0, DSA_IKW0 // LANE))],
        out_specs=pl.BlockSpec((1, t, total), lambda b: (b, 0, 0)),
        out_shape=jax.ShapeDtypeStruct((bsz, t, total), F32),
        scratch_shapes=[pltpu.VMEM((t, total), I32)],
        compiler_params=_cparams(("arbitrary",), 32),
        name="dsa_sample_select",
    )(scores_past, hf, hf, hf)


def _dsa_sample_attend_kernel(pt_ref, q_ref, kn_ref, vn_ref, selp_ref, seln_ref, *refs, pp, t, new_w):
    k_pages, v_pages = refs[:pp], refs[pp:2 * pp]
    o_ref, m_ref, l_ref, acc_ref = refs[2 * pp:]
    j = pl.program_id(1)
    gt = DSA_GROUP * t

    @pl.when(j == 0)
    def _():
        m_ref[...] = jnp.full(m_ref.shape, MASKED, F32)
        l_ref[...] = jnp.zeros_like(l_ref)
        acc_ref[...] = jnp.zeros_like(acc_ref)

    q_groups = [jnp.concatenate([q_ref[0, :, (c * DSA_GROUP + g) * DSA_HEAD_DIM:(c * DSA_GROUP + g + 1) * DSA_HEAD_DIM]
                                 for g in range(DSA_GROUP)], axis=0).astype(BF16) for c in range(DSA_KV_HEADS)]

    def attend(kcs, vcs, sel):
        n = kcs[0].shape[0]
        for c in range(DSA_KV_HEADS):
            kc, vc = kcs[c], vcs[c]
            logits = (_dot_nt(q_groups[c], kc) * (DSA_HEAD_DIM ** -0.5)).reshape(DSA_GROUP, t, n)
            masked = jnp.where(sel[None], logits, MASKED).reshape(gt, n)
            m_old = m_ref[c]
            m_new = jnp.maximum(m_old, jnp.max(masked, axis=1, keepdims=True))
            p = jnp.where(sel[None], jnp.exp(masked - m_new).reshape(DSA_GROUP, t, n), 0.0).reshape(gt, n)
            alpha = jnp.exp(m_old - m_new)
            l_ref[c] = alpha * l_ref[c] + jnp.sum(p, axis=1, keepdims=True)
            acc_ref[c] = alpha * acc_ref[c] + _dot(p.astype(BF16), vc)
            m_ref[c] = m_new

    def head_rows(pages, c):
        return jnp.concatenate([r[0, pl.ds(c, PAGE_SIZE, stride=DSA_KV_HEADS), :].astype(BF16) for r in pages], axis=0)

    attend([head_rows(k_pages, c) for c in range(DSA_KV_HEADS)],
           [head_rows(v_pages, c) for c in range(DSA_KV_HEADS)], selp_ref[0] != 0.0)

    @pl.when(j == pl.num_programs(1) - 1)
    def _():
        pad = jnp.zeros((new_w - t, DSA_HEAD_DIM), F32)

        def new_rows(ref, c):
            return jnp.concatenate([ref[0, :, c * DSA_HEAD_DIM:(c + 1) * DSA_HEAD_DIM], pad], axis=0).astype(BF16)

        attend([new_rows(kn_ref, c) for c in range(DSA_KV_HEADS)],
               [new_rows(vn_ref, c) for c in range(DSA_KV_HEADS)], seln_ref[0] != 0.0)
        for c in range(DSA_KV_HEADS):
            oc = acc_ref[c] / l_ref[c]
            for g in range(DSA_GROUP):
                hh = c * DSA_GROUP + g
                o_ref[0, :, hh * DSA_HEAD_DIM:(hh + 1) * DSA_HEAD_DIM] = oc[g * t:(g + 1) * t].astype(BF16)


def _dsa_sample_attend(hf, sel, cache_k, cache_v, page_table, *, pp, new_w=512):
    bsz, t, _ = hf.shape
    n_pages = page_table.shape[1]
    past = n_pages * PAGE_SIZE
    assert n_pages % pp == 0 and past % new_w == 0
    hq = DSA_HEADS * DSA_HEAD_DIM
    gt = DSA_GROUP * t

    def page_spec(i):
        return pl.BlockSpec((1, PAGE_SIZE * DSA_KV_HEADS, DSA_HEAD_DIM), lambda b, j, pt: (pt[b, j * pp + i], 0, 0))

    grid_spec = pltpu.PrefetchScalarGridSpec(
        num_scalar_prefetch=1, grid=(bsz, n_pages // pp),
        in_specs=[pl.BlockSpec((1, t, hq), lambda b, j, pt: (b, 0, DSA_Q0 // hq)),
                  pl.BlockSpec((1, t, DSA_HKV), lambda b, j, pt: (b, 0, DSA_K0 // DSA_HKV)),
                  pl.BlockSpec((1, t, DSA_HKV), lambda b, j, pt: (b, 0, DSA_V0 // DSA_HKV)),
                  pl.BlockSpec((1, t, pp * PAGE_SIZE), lambda b, j, pt: (b, 0, j)),
                  pl.BlockSpec((1, t, new_w), lambda b, j, pt: (b, 0, past // new_w))]
                 + [page_spec(i) for i in range(pp)] * 2,
        out_specs=pl.BlockSpec((1, t, hq), lambda b, j, pt: (b, 0, 0)),
        scratch_shapes=[pltpu.VMEM((DSA_KV_HEADS, gt, 1), F32),
                        pltpu.VMEM((DSA_KV_HEADS, gt, 1), F32),
                        pltpu.VMEM((DSA_KV_HEADS, gt, DSA_HEAD_DIM), F32)])
    return pl.pallas_call(
        functools.partial(_dsa_sample_attend_kernel, pp=pp, t=t, new_w=new_w),
        grid_spec=grid_spec,
        out_shape=jax.ShapeDtypeStruct((bsz, t, hq), BF16),
        compiler_params=_cparams(("arbitrary", "arbitrary"), 48),
        name="dsa_sample_attend",
    )(page_table, hf, hf, hf, sel, sel, *([cache_k] * pp), *([cache_v] * pp))


def _router_kernel(x_ref, wt_ref, o_ref):
    x = x_ref[...]
    wt = wt_ref[...]
    xh = x.astype(BF16)
    xl = (x - xh.astype(F32)).astype(BF16)
    wh = wt.astype(BF16)
    wl = (wt - wh.astype(F32)).astype(BF16)
    logits = _dot_nt(wh, xh) + (_dot_nt(wl, xh) + _dot_nt(wh, xl))
    sub = lax.broadcasted_iota(I32, logits.shape, 0)
    l1 = jnp.max(logits, axis=0, keepdims=True)
    i1 = jnp.min(jnp.where(logits == l1, sub, N_EXPERTS), axis=0, keepdims=True)
    rest = jnp.where(sub == i1, -jnp.inf, logits)
    l2 = jnp.max(rest, axis=0, keepdims=True)
    i2 = jnp.min(jnp.where(rest == l2, sub, N_EXPERTS), axis=0, keepdims=True)
    e2 = jnp.exp(l2 - l1)
    denom = 1.0 + e2
    o_ref[...] = jnp.where(sub == i1, 1.0 / denom, jnp.where(sub == i2, e2 / denom, 0.0))


def _router(x, w_t):
    n, d = x.shape
    tm = _row_tile(n, 512)
    return pl.pallas_call(
        _router_kernel,
        grid=(n // tm,),
        in_specs=[pl.BlockSpec((tm, d), lambda i: (i, 0)), pl.BlockSpec((N_EXPERTS, d), lambda i: (0, 0))],
        out_specs=pl.BlockSpec((N_EXPERTS, tm), lambda i: (0, i)),
        out_shape=jax.ShapeDtypeStruct((N_EXPERTS, n), F32),
        compiler_params=_cparams(("arbitrary",), 32),
        name="router",
    )(x, w_t)


MOE_ROWS = 128
MOE_CHUNK_BLOCKS = 2


def _moe_kernel(cnt_ref, xb_ref, gt_ref, wg_ref, wu_ref, wd_ref, o_ref,
                rank_ref, xg_ref, acc_ref, upper_ref, *, ts, nsub, arms):
    i, e, f = pl.program_id(0), pl.program_id(1), pl.program_id(2)
    first_f = f == 0
    last_f = f == pl.num_programs(2) - 1

    @pl.when(jnp.logical_and(i == 0, jnp.logical_and(e == 0, first_f)))
    def _():
        upper_ref[...] = _strict_upper(ts)

    @pl.when(jnp.logical_and(e == 0, first_f))
    def _():
        o_ref[...] = jnp.zeros_like(o_ref)
        for sub in range(nsub):
            chosen = jnp.where(gt_ref[:, sub * ts:(sub + 1) * ts] > 0.0, 1.0, 0.0).astype(BF16)
            rank_ref[sub] = _dot(chosen, upper_ref[...])

    for sub in range(nsub):
        tok = slice(sub * ts, (sub + 1) * ts)
        nb = (cnt_ref[(i * nsub + sub) * N_EXPERTS + e] + MOE_ROWS - 1) // MOE_ROWS
        ge = gt_ref[pl.ds(e, 1), tok]
        slot = jnp.where(ge > 0.0, rank_ref[sub, pl.ds(e, 1), :], -1.0)

        def one_hot(start, rows, slot=slot):
            r = (start + lax.broadcasted_iota(I32, (rows, 1), 0)).astype(F32)
            return slot == r

        def for_arms(fn, nb=nb):
            big = arms[0]

            def chunk(t, carry):
                fn(pl.multiple_of(t * (big * MOE_ROWS), big * MOE_ROWS), big * MOE_ROWS)
                return carry
            lax.fori_loop(0, nb // big, chunk, 0)
            for size in arms[1:]:
                start = pl.multiple_of((nb & ~(2 * size - 1)) * MOE_ROWS, MOE_ROWS)

                @pl.when((nb & size) != 0)
                def _():
                    fn(start, size * MOE_ROWS)

        @pl.when(first_f)
        def _():
            def gather(start, rows):
                hot = jnp.where(one_hot(start, rows), 1.0, 0.0).astype(BF16)
                xg_ref[sub, pl.ds(start, rows), :] = _dot(hot, xb_ref[tok, :]).astype(BF16)
                acc_ref[sub, pl.ds(start, rows), :] = jnp.zeros((rows, acc_ref.shape[2]), F32)
            for_arms(gather)

        def expert(start, rows):
            xs = xg_ref[sub, pl.ds(start, rows), :]
            hg = _dot(xs, wg_ref[0])
            hu = _dot(xs, wu_ref[0])
            h = (hg * _sigmoid(hg) * hu).astype(BF16)
            acc_ref[sub, pl.ds(start, rows), :] += _dot(h, wd_ref[0])
        for_arms(expert)

        @pl.when(last_f)
        def _():
            def scatter(start, rows):
                hot = one_hot(start, rows)
                gate = jnp.sum(jnp.where(hot, ge, 0.0), axis=1, keepdims=True)
                v = (gate * acc_ref[sub, pl.ds(start, rows), :]).astype(BF16)
                o_ref[tok, :] += _dot_tn(jnp.where(hot, 1.0, 0.0).astype(BF16), v)
            for_arms(scatter)


def _moe(xb, gates_t, wg, wu, wd, *, tf=896):
    n, d = xb.shape
    ff = wg.shape[2]
    assert ff % tf == 0
    ts = _row_tile(n, 1024)
    nsub = 2 if n % (2 * ts) == 0 else 1
    tm = ts * nsub
    assert ts % MOE_ROWS == 0
    chunk_blocks = min(MOE_CHUNK_BLOCKS, ts // MOE_ROWS)
    arms = tuple(s for s in (8, 4, 2, 1) if s <= chunk_blocks)
    assert arms[0] == chunk_blocks
    counts = jnp.sum((gates_t > 0.0).reshape(N_EXPERTS, n // ts, ts), axis=2).astype(I32).T.reshape(-1)
    grid_spec = pltpu.PrefetchScalarGridSpec(
        num_scalar_prefetch=1, grid=(n // tm, N_EXPERTS, ff // tf),
        in_specs=[pl.BlockSpec((tm, d), lambda i, e, f, c: (i, 0)),
                  pl.BlockSpec((N_EXPERTS, tm), lambda i, e, f, c: (0, i)),
                  pl.BlockSpec((1, d, tf), lambda i, e, f, c: (e, 0, f)),
                  pl.BlockSpec((1, d, tf), lambda i, e, f, c: (e, 0, f)),
                  pl.BlockSpec((1, tf, d), lambda i, e, f, c: (e, f, 0))],
        out_specs=pl.BlockSpec((tm, d), lambda i, e, f, c: (i, 0)),
        scratch_shapes=[pltpu.VMEM((nsub, N_EXPERTS, ts), F32), pltpu.VMEM((nsub, ts, d), BF16),
                        pltpu.VMEM((nsub, ts, d), F32), pltpu.VMEM((ts, ts), BF16)])
    return pl.pallas_call(
        functools.partial(_moe_kernel, ts=ts, nsub=nsub, arms=arms),
        grid_spec=grid_spec,
        out_shape=jax.ShapeDtypeStruct((n, d), F32),
        compiler_params=_cparams(("arbitrary", "arbitrary", "arbitrary"), 60),
        name="moe",
    )(counts, xb, gates_t, wg, wu, wd)


def _pad_cols(w, cols):
    return jnp.pad(w, ((0, 0), (0, cols - w.shape[1])))


def _gla_layer(x, s0, w_in, w_up, b_a, norm_g, w_out, ln_g, ln_b, *, alpha):
    bsz, t, d = x.shape
    x2 = x.reshape(bsz * t, d)
    hf = _proj(x2, _pad_cols(w_in, GLA_COLS).astype(BF16), tn=640, bf16_copy=False)
    hf = hf.reshape(bsz, t, GLA_COLS)
    c = GLA_CHUNK if t % GLA_CHUNK == 0 else t
    valid = c
    if c < GLA_CHUNK:
        hf = jnp.pad(hf, ((0, 0), (0, GLA_CHUNK - c), (0, 0)))
        c = GLA_CHUNK
    w_up_pad = jnp.pad(w_up, ((0, LANE - GLA_GATE_RANK), (0, 0))).astype(BF16)
    o, st = _gla_core(hf, jnp.swapaxes(s0, 2, 3), w_up_pad, b_a.reshape(1, GLA_HK), norm_g.reshape(1, GLA_HV),
                      c=c, valid=valid)
    o = o[:, :t].reshape(bsz * t, GLA_HV)
    y, yb = _mm_res_ln(x2, o, w_out.astype(BF16), ln_g.reshape(1, d), ln_b.reshape(1, d), alpha=alpha)
    return y.reshape(bsz, t, d), yb, jnp.swapaxes(st, 2, 3)


def _dsa_weight(w_in):
    hq = DSA_HEADS * DSA_HEAD_DIM
    iq0 = hq + 2 * DSA_HKV
    ik0 = iq0 + IDX_HEADS * IDX_DIM
    w_iq = w_in[:, iq0:ik0].reshape(-1, IDX_HEADS, IDX_DIM)
    w_iq = jnp.pad(w_iq, ((0, 0), (0, 0), (0, LANE - IDX_DIM))).reshape(-1, IDX_HEADS * LANE)
    w_v = w_in[:, hq + DSA_HKV:iq0].reshape(-1, DSA_KV_HEADS, DSA_HEAD_DIM)
    w_v1 = jnp.pad(w_v, ((0, 0), (0, 0), (0, DSA_HEAD_DIM))).reshape(-1, 2 * DSA_HKV)
    w = jnp.concatenate([w_in[:, :hq], w_iq, w_in[:, hq:iq0], w_v1, _pad_cols(w_in[:, ik0:], LANE)], axis=1)
    return w.astype(BF16)


def _dsa_project(x, w_in):
    bsz, t, d = x.shape
    hf, hb = _proj(x.reshape(bsz * t, d), _dsa_weight(w_in), tn=DSA_COLS // 5, ones_at=DSA_ONES)
    hf = hf.reshape(bsz, t, DSA_COLS)
    k = hf[:, :, DSA_K0:DSA_V0].reshape(bsz, t, DSA_KV_HEADS, DSA_HEAD_DIM)
    v = hf[:, :, DSA_V0:DSA_V1_0].reshape(bsz, t, DSA_KV_HEADS, DSA_HEAD_DIM)
    ik = hf[:, :, DSA_IKW0:DSA_IKW0 + IDX_DIM]
    return hf, hb.reshape(bsz, t, DSA_COLS), k, v, ik


def kernel(x_prompt, x_sample, state_gla, cache_k, cache_v, cache_idx_k, page_table, p_prompt, p_sample,
           ln_mix_g, ln_mix_b, ln_ffn_g, ln_ffn_b, w_in_gla, w_alpha_up, b_alpha, gla_norm_g, w_out_gla,
           w_in_dsa, w_out_dsa, w_ffn_gate, w_ffn_up, w_ffn_down, w_router, w_exp_gate, w_exp_up,
           w_exp_down, w_ple_proj, w_ple_gate):
    depth = ln_mix_g.shape[0]
    alpha = (2.0 * depth) ** 0.25
    d = x_prompt.shape[-1]
    xp, xs = x_prompt, x_sample
    gla_p, gla_s, kp, vp, ikp, ksm, vsm, iksm = [], [], [], [], [], [], [], []
    n_pages = page_table.shape[1]
    pp = 32 if n_pages % 32 == 0 else n_pages
    for i in range(depth):
        j = i // 2
        lg, lb = ln_mix_g[i], ln_mix_b[i]
        if i % 2 == 0:
            gw = (w_in_gla[j], w_alpha_up[j], b_alpha[j], gla_norm_g[j], w_out_gla[j], lg, lb)
            s0 = jnp.zeros((xp.shape[0], GLA_HEADS, GLA_DK, GLA_DV), F32)
            xp, xpb, sp = _gla_layer(xp, s0, *gw, alpha=alpha)
            xs, xsb, ss = _gla_layer(xs, state_gla[j], *gw, alpha=alpha)
            gla_p.append(sp)
            gla_s.append(ss)
        else:
            w_out = w_out_dsa[j].astype(BF16)
            hf, hb, k_p, v_p, ik_p = _dsa_project(xp, w_in_dsa[j])
            op = _dsa_prompt(hf, hb)
            bp, tp, _ = xp.shape
            xp, xpb = _mm_res_ln(xp.reshape(bp * tp, d), op.reshape(bp * tp, -1), w_out, lg.reshape(1, d),
                                 lb.reshape(1, d), alpha=alpha)
            xp = xp.reshape(bp, tp, d)
            hf, hb, k_s, v_s, ik_s = _dsa_project(xs, w_in_dsa[j])
            sc = _dsa_sample_scores(hf, jnp.swapaxes(cache_idx_k[j], 1, 2), page_table, pp=pp)
            sel = _dsa_sample_select(sc, hf)
            n_pool = cache_k.shape[1]
            rows = PAGE_SIZE * DSA_KV_HEADS
            os_ = _dsa_sample_attend(hf, sel, cache_k[j].reshape(n_pool, rows, DSA_HEAD_DIM),
                                     cache_v[j].reshape(n_pool, rows, DSA_HEAD_DIM), page_table, pp=pp)
            bs, ts, _ = xs.shape
            xs, xsb = _mm_res_ln(xs.reshape(bs * ts, d), os_.reshape(bs * ts, -1), w_out, lg.reshape(1, d),
                                 lb.reshape(1, d), alpha=alpha)
            xs = xs.reshape(bs, ts, d)
            kp.append(k_p)
            vp.append(v_p)
            ikp.append(ik_p)
            ksm.append(k_s)
            vsm.append(v_s)
            iksm.append(ik_s)
        fg, fb = ln_ffn_g[i].reshape(1, d), ln_ffn_b[i].reshape(1, d)
        outs = []
        wpg, wpp = w_ple_gate[i].astype(BF16), w_ple_proj[i].astype(BF16)
        for x, xb, p in ((xp, xpb, p_prompt[i]), (xs, xsb, p_sample[i])):
            bsz, t, _ = x.shape
            x2 = x.reshape(bsz * t, d)
            p2 = p.reshape(bsz * t, -1)
            if i % 2 == 0:
                y2 = _ffn_ln_ple(x2, w_ffn_gate[j].astype(BF16), w_ffn_up[j].astype(BF16),
                                 w_ffn_down[j].astype(BF16), fg, fb, p2, wpg, wpp, alpha=alpha)
            else:
                gates = _router(x2, w_router[j].T)
                f2 = _moe(xb, gates, w_exp_gate[j].astype(BF16), w_exp_up[j].astype(BF16),
                          w_exp_down[j].astype(BF16))
                y2 = _res_ln_ple(x2, f2, p2, fg, fb, wpg, wpp, alpha=alpha)
            outs.append(y2.reshape(bsz, t, d))
        xp, xs = outs
    return (xp, xs, jnp.stack(gla_p), jnp.stack(gla_s), jnp.stack(kp), jnp.stack(vp), jnp.stack(ikp),
            jnp.stack(ksm), jnp.stack(vsm), jnp.stack(iksm))
```

```python
import functools

import jax
import jax.numpy as jnp
from jax import lax
from jax.experimental import pallas as pl
from jax.experimental.pallas import tpu as pltpu

F32 = jnp.float32
BF16 = jnp.bfloat16
I32 = jnp.int32

D_MODEL = 1024
LN_EPS = 1e-5
LANE = 128

GLA_HEADS = 4
GLA_DK = 128
GLA_DV = 256
GLA_GATE_RANK = 16
GLA_TAU = 16.0
GLA_CHUNK = 64
GLA_HK = GLA_HEADS * GLA_DK
GLA_HV = GLA_HEADS * GLA_DV
GLA_COLS = 2 * GLA_HK + 2 * GLA_HV + LANE

DSA_HEADS = 8
DSA_HEAD_DIM = 128
DSA_KV_HEADS = 2
DSA_GROUP = DSA_HEADS // DSA_KV_HEADS
DSA_HKV = DSA_KV_HEADS * DSA_HEAD_DIM
IDX_HEADS = 8
IDX_DIM = 64
TOPK_MAX = 256
PAGE_SIZE = 128
DSA_Q0 = 0
DSA_IQ0 = DSA_HEADS * DSA_HEAD_DIM
DSA_K0 = DSA_IQ0 + IDX_HEADS * LANE
DSA_V0 = DSA_K0 + DSA_HKV
DSA_V1_0 = DSA_V0 + DSA_HKV
DSA_IKW0 = DSA_V1_0 + 2 * DSA_HKV
DSA_COLS = DSA_IKW0 + LANE
DSA_ONES = tuple(DSA_V1_0 + (2 * c + 1) * DSA_HEAD_DIM for c in range(DSA_KV_HEADS))

N_EXPERTS = 8

NEG_INF_KEY = -2139095041
INT_MIN = -2147483648
MASKED = -1e30


def _cparams(sem, vmem_mb):
    return pltpu.CompilerParams(dimension_semantics=sem, vmem_limit_bytes=vmem_mb * 2 ** 20)


def _dot(a, b):
    return jnp.dot(a, b, preferred_element_type=F32)


def _dot_nt(a, b):
    return lax.dot_general(a, b, (((1,), (1,)), ((), ())), preferred_element_type=F32)


def _dot_tn(a, b):
    return lax.dot_general(a, b, (((0,), (0,)), ((), ())), preferred_element_type=F32)


def _sigmoid(x):
    return 1.0 / (1.0 + jnp.exp(-x))


def _layer_norm(y, g, b):
    mu = jnp.mean(y, axis=-1, keepdims=True)
    yc = y - mu
    var = jnp.mean(yc * yc, axis=-1, keepdims=True)
    return yc * lax.rsqrt(var + LN_EPS) * g + b


def _row_tile(n, pref):
    t = min(n, pref)
    assert n % t == 0
    return t


def _proj_kernel(x_ref, w_ref, o_ref, ob_ref, *, tn, ones_at):
    xb = x_ref[...].astype(BF16)
    for c in range(w_ref.shape[1] // tn):
        r = _dot(xb, w_ref[:, c * tn:(c + 1) * tn])
        o_ref[:, c * tn:(c + 1) * tn] = r
        if ob_ref is not None:
            ob_ref[:, c * tn:(c + 1) * tn] = r.astype(BF16)
    for c0 in ones_at:
        ob_ref[:, c0:c0 + LANE] = jnp.ones((x_ref.shape[0], LANE), BF16)


def _proj_f32_kernel(x_ref, w_ref, o_ref, *, tn):
    _proj_kernel(x_ref, w_ref, o_ref, None, tn=tn, ones_at=())


def _proj(x, w, *, tn, ones_at=(), bf16_copy=True):
    n, d = x.shape
    cols = w.shape[1]
    assert cols % tn == 0 and (bf16_copy or not ones_at)
    tm = _row_tile(n, 512)
    out_spec = pl.BlockSpec((tm, cols), lambda i: (i, 0))
    if bf16_copy:
        body = functools.partial(_proj_kernel, tn=tn, ones_at=ones_at)
        out_specs = [out_spec, out_spec]
        out_shape = [jax.ShapeDtypeStruct((n, cols), F32), jax.ShapeDtypeStruct((n, cols), BF16)]
    else:
        body = functools.partial(_proj_f32_kernel, tn=tn)
        out_specs, out_shape = out_spec, jax.ShapeDtypeStruct((n, cols), F32)
    return pl.pallas_call(
        body,
        grid=(n // tm,),
        in_specs=[pl.BlockSpec((tm, d), lambda i: (i, 0)), pl.BlockSpec((d, cols), lambda i: (0, 0))],
        out_specs=out_specs,
        out_shape=out_shape,
        compiler_params=_cparams(("arbitrary",), 56),
        name="proj",
    )(x, w)


def _mm_res_ln_kernel(x_ref, a_ref, w_ref, g_ref, b_ref, o_ref, ob_ref, *, alpha):
    m = _dot(a_ref[...].astype(BF16), w_ref[...])
    y = _layer_norm(alpha * x_ref[...] + m, g_ref[...], b_ref[...])
    o_ref[...] = y
    ob_ref[...] = y.astype(BF16)


def _mm_res_ln(x, a, w, g, b, *, alpha):
    n, d = x.shape
    ka = a.shape[1]
    tm = _row_tile(n, 512)
    row = lambda i: (i, 0)
    fixed = lambda i: (0, 0)
    return pl.pallas_call(
        functools.partial(_mm_res_ln_kernel, alpha=alpha),
        grid=(n // tm,),
        in_specs=[pl.BlockSpec((tm, d), row), pl.BlockSpec((tm, ka), row), pl.BlockSpec((ka, d), fixed),
                  pl.BlockSpec((1, d), fixed), pl.BlockSpec((1, d), fixed)],
        out_specs=[pl.BlockSpec((tm, d), row), pl.BlockSpec((tm, d), row)],
        out_shape=[jax.ShapeDtypeStruct((n, d), F32), jax.ShapeDtypeStruct((n, d), BF16)],
        compiler_params=_cparams(("arbitrary",), 40),
        name="mm_res_ln",
    )(x, a, w, g, b)


def _ffn_ln_ple_kernel(x_ref, wg_ref, wu_ref, wd_ref, g_ref, b_ref, p_ref, wpg_ref, wpp_ref, o_ref, xb_ref, acc_ref,
                       *, alpha):
    f = pl.program_id(1)

    @pl.when(f == 0)
    def _():
        xb_ref[...] = x_ref[...].astype(BF16)
        acc_ref[...] = jnp.zeros_like(acc_ref)

    xb = xb_ref[...]
    hg = _dot(xb, wg_ref[...])
    hu = _dot(xb, wu_ref[...])
    h = (hg * _sigmoid(hg) * hu).astype(BF16)
    acc_ref[...] += _dot(h, wd_ref[...])

    @pl.when(f == pl.num_programs(1) - 1)
    def _():
        z = _layer_norm(alpha * x_ref[...] + acc_ref[...], g_ref[...], b_ref[...])
        gate = _sigmoid(_dot(z.astype(BF16), wpg_ref[...]))
        o_ref[...] = z + gate * _dot(p_ref[...].astype(BF16), wpp_ref[...])


def _ffn_ln_ple(x, wg, wu, wd, g, b, p, wpg, wpp, *, alpha, tf=512):
    n, d = x.shape
    ff = wg.shape[1]
    pd = p.shape[1]
    assert ff % tf == 0
    tm = _row_tile(n, 1024)
    fixed = lambda i, f: (0, 0)
    return pl.pallas_call(
        functools.partial(_ffn_ln_ple_kernel, alpha=alpha),
        grid=(n // tm, ff // tf),
        in_specs=[pl.BlockSpec((tm, d), lambda i, f: (i, 0)),
                  pl.BlockSpec((d, tf), lambda i, f: (0, f)),
                  pl.BlockSpec((d, tf), lambda i, f: (0, f)),
                  pl.BlockSpec((tf, d), lambda i, f: (f, 0)),
                  pl.BlockSpec((1, d), fixed),
                  pl.BlockSpec((1, d), fixed),
                  pl.BlockSpec((tm, pd), lambda i, f: (i, 0)),
                  pl.BlockSpec((d, d), fixed),
                  pl.BlockSpec((pd, d), fixed)],
        out_specs=pl.BlockSpec((tm, d), lambda i, f: (i, 0)),
        out_shape=jax.ShapeDtypeStruct((n, d), F32),
        scratch_shapes=[pltpu.VMEM((tm, d), BF16), pltpu.VMEM((tm, d), F32)],
        compiler_params=_cparams(("arbitrary", "arbitrary"), 56),
        name="ffn_ln_ple",
    )(x, wg, wu, wd, g, b, p, wpg, wpp)


def _res_ln_ple_kernel(x_ref, f_ref, p_ref, g_ref, b_ref, wg_ref, wp_ref, o_ref, *, alpha):
    z = _layer_norm(alpha * x_ref[...] + f_ref[...], g_ref[...], b_ref[...])
    gate = _sigmoid(_dot(z.astype(BF16), wg_ref[...]))
    o_ref[...] = z + gate * _dot(p_ref[...].astype(BF16), wp_ref[...])


def _res_ln_ple(x, f, p, g, b, wg, wp, *, alpha):
    n, d = x.shape
    pd = p.shape[1]
    tm = _row_tile(n, 512)
    row = lambda i: (i, 0)
    fixed = lambda i: (0, 0)
    return pl.pallas_call(
        functools.partial(_res_ln_ple_kernel, alpha=alpha),
        grid=(n // tm,),
        in_specs=[pl.BlockSpec((tm, d), row), pl.BlockSpec((tm, d), row), pl.BlockSpec((tm, pd), row),
                  pl.BlockSpec((1, d), fixed), pl.BlockSpec((1, d), fixed),
                  pl.BlockSpec((d, d), fixed), pl.BlockSpec((pd, d), fixed)],
        out_specs=pl.BlockSpec((tm, d), row),
        out_shape=jax.ShapeDtypeStruct((n, d), F32),
        compiler_params=_cparams(("arbitrary",), 40),
        name="res_ln_ple",
    )(x, f, p, g, b, wg, wp)


def _gla_kernel(q_ref, k_ref, v_ref, g_ref, a_ref, wup_ref, ba_ref, ng_ref, s0_ref, o_ref, st_out_ref, st_ref,
                *, c, valid, nbs):
    t = pl.program_id(1)

    @pl.when(t == 0)
    def _():
        st_ref[...] = s0_ref[...]

    row = lax.broadcasted_iota(I32, (c, c), 0)
    col = lax.broadcasted_iota(I32, (c, c), 1)
    causal = row >= col
    for bi in range(nbs):
        logit = _dot(a_ref[bi].astype(BF16), wup_ref[...]) + ba_ref[...]
        log_a = (jnp.minimum(logit, 0.0) - jnp.log(1.0 + jnp.exp(-jnp.abs(logit)))) * (1.0 / GLA_TAU)
        if valid < c:
            log_a = jnp.where(lax.broadcasted_iota(I32, log_a.shape, 0) < valid, log_a, 0.0)
        bcum = jnp.dot(causal.astype(F32), log_a, preferred_element_type=F32, precision=lax.Precision.HIGHEST)
        blast = bcum[c - 1:c, :]
        q = q_ref[bi] * (GLA_DK ** -0.5)
        k = k_ref[bi]
        q_dec = (q * jnp.exp(bcum)).astype(BF16)
        k_inv = (k * jnp.exp(-bcum)).astype(BF16)
        k_dec = (k * jnp.exp(blast - bcum)).astype(BF16)
        decay = jnp.exp(blast)
        v = v_ref[bi].astype(BF16)
        for h in range(GLA_HEADS):
            sk = slice(h * GLA_DK, (h + 1) * GLA_DK)
            sv = slice(h * GLA_DV, (h + 1) * GLA_DV)
            att = jnp.where(causal, _dot_nt(q_dec[:, sk], k_inv[:, sk]), 0.0).astype(BF16)
            st = st_ref[bi, h]
            o = _dot(att, v[:, sv]) + _dot_nt(q_dec[:, sk], st.astype(BF16))
            st_ref[bi, h] = st * decay[:, sk] + _dot_tn(v[:, sv], k_dec[:, sk])
            mu = jnp.mean(o, axis=-1, keepdims=True)
            oc = o - mu
            var = jnp.mean(oc * oc, axis=-1, keepdims=True)
            on = oc * lax.rsqrt(var + LN_EPS) * ng_ref[:, sv]
            gg = g_ref[bi, :, sv]
            o_ref[bi, :, sv] = (on * (gg * _sigmoid(gg))).astype(BF16)

    @pl.when(t == pl.num_programs(1) - 1)
    def _():
        st_out_ref[...] = st_ref[...]


def _gla_core(h, s0_t, wup, ba, ng, *, c, valid):
    bsz, tlen, _ = h.shape
    assert tlen % c == 0
    nbs = 4 if bsz % 4 == 0 else 1
    v_blk = (2 * GLA_HK) // GLA_HV
    state_spec = pl.BlockSpec((nbs, GLA_HEADS, GLA_DV, GLA_DK), lambda b, t: (b, 0, 0, 0))
    return pl.pallas_call(
        functools.partial(_gla_kernel, c=c, valid=valid, nbs=nbs),
        grid=(bsz // nbs, tlen // c),
        in_specs=[pl.BlockSpec((nbs, c, GLA_HK), lambda b, t: (b, t, 0)),
                  pl.BlockSpec((nbs, c, GLA_HK), lambda b, t: (b, t, 1)),
                  pl.BlockSpec((nbs, c, GLA_HV), lambda b, t: (b, t, v_blk)),
                  pl.BlockSpec((nbs, c, GLA_HV), lambda b, t: (b, t, v_blk + 1)),
                  pl.BlockSpec((nbs, c, LANE), lambda b, t: (b, t, (2 * GLA_HK + 2 * GLA_HV) // LANE)),
                  pl.BlockSpec((LANE, GLA_HK), lambda b, t: (0, 0)),
                  pl.BlockSpec((1, GLA_HK), lambda b, t: (0, 0)),
                  pl.BlockSpec((1, GLA_HV), lambda b, t: (0, 0)),
                  state_spec],
        out_specs=[pl.BlockSpec((nbs, c, GLA_HV), lambda b, t: (b, t, 0)), state_spec],
        out_shape=[jax.ShapeDtypeStruct((bsz, tlen, GLA_HV), BF16),
                   jax.ShapeDtypeStruct((bsz, GLA_HEADS, GLA_DV, GLA_DK), F32)],
        scratch_shapes=[pltpu.VMEM((nbs, GLA_HEADS, GLA_DV, GLA_DK), F32)],
        compiler_params=_cparams(("arbitrary", "arbitrary"), 32),
        name="gla_core",
    )(h, h, h, h, h, wup, ba, ng, s0_t)


def _sortable(s):
    bits = lax.bitcast_convert_type(s, I32)
    return bits ^ (lax.shift_right_arithmetic(bits, 31) & 0x7FFFFFFF)


def _lane_fold(m):
    acc = m[:, 0:LANE]
    for i in range(1, m.shape[1] // LANE):
        acc = acc + m[:, i * LANE:(i + 1) * LANE]
    return acc


def _lane_fold_max(m):
    acc = m[:, 0:LANE]
    for i in range(1, m.shape[1] // LANE):
        acc = jnp.maximum(acc, m[:, i * LANE:(i + 1) * LANE])
    return acc


def _bias_bits(sel):
    return lax.bitcast_convert_type(jnp.where(sel, 0.0, MASKED).astype(F32), I32)


def _kth_largest_key(count_ge, rows, k):
    kf = float(k)
    thr0 = jnp.where(count_ge(jnp.zeros((rows, 1), I32)) >= kf, 0, INT_MIN).astype(I32)

    def bit_body(i, thr):
        cand = thr | lax.shift_left(jnp.int32(1), 30 - i)
        return jnp.where(count_ge(cand) >= kf, cand, thr)

    return lax.fori_loop(0, 31, bit_body, thr0)


def _strict_upper(n):
    return (lax.broadcasted_iota(I32, (n, n), 0) < lax.broadcasted_iota(I32, (n, n), 1)).astype(BF16)


def _dsa_prompt_kernel(q_ref, iq_ref, iw_ref, ikw_ref, k_ref, v1_ref, o_ref,
                       keys_ref, top2_ref, iqs_ref, qg_ref, d_ref, m_ref, acc_ref, *, tq, tk, topk):
    j = pl.program_id(1)
    nkt = ((j + 1) * tq + tk - 1) // tk
    gq = DSA_GROUP * tq
    rp = 16
    for h in range(IDX_HEADS):
        iqs_ref[h * tq:(h + 1) * tq, :] = iq_ref[0, :, h * LANE:(h + 1) * LANE]
    for c in range(DSA_KV_HEADS):
        for g in range(DSA_GROUP):
            hh = c * DSA_GROUP + g
            qg_ref[c, g * tq:(g + 1) * tq, :] = q_ref[0, :, hh * DSA_HEAD_DIM:(hh + 1) * DSA_HEAD_DIM]
    ws = iw_ref[0][:, IDX_DIM:IDX_DIM + IDX_HEADS] * (IDX_HEADS ** -0.5) * (IDX_DIM ** -0.5)

    def score_tile(kt, diagonal):
        ikt = ikw_ref[0, pl.ds(pl.multiple_of(kt * tk, tk), tk), :]
        d_ref[...] = _dot_nt(iqs_ref[...], ikt)
        for r in range(tq // rp):
            s = jnp.zeros((rp, tk), F32)
            for h in range(IDX_HEADS):
                dh = d_ref[h * tq + r * rp:h * tq + (r + 1) * rp, :]
                s = s + ws[r * rp:(r + 1) * rp, h:h + 1] * jnp.maximum(dh, 0.0)
            key = _sortable(s + 0.0)
            if diagonal:
                qpos = j * tq + r * rp + lax.broadcasted_iota(I32, (rp, tk), 0)
                kpos = kt * tk + lax.broadcasted_iota(I32, (rp, tk), 1)
                key = jnp.where(kpos <= qpos, key, INT_MIN)
            keys_ref[kt, r * rp:(r + 1) * rp, :] = key
            rows = slice(r * rp, (r + 1) * rp)
            m1, m2 = top2_ref[rows, 0:LANE], top2_ref[rows, LANE:2 * LANE]
            for lt in range(tk // LANE):
                x = key[:, lt * LANE:(lt + 1) * LANE]
                m2 = jnp.maximum(m2, jnp.minimum(m1, x))
                m1 = jnp.maximum(m1, x)
            top2_ref[rows, :] = jnp.concatenate([m1, m2], axis=1)

    def score_body(kt, carry):
        score_tile(kt, False)
        return carry

    top2_ref[...] = jnp.full(top2_ref.shape, INT_MIN, I32)
    lax.fori_loop(0, nkt - 1, score_body, 0)
    score_tile(nkt - 1, True)

    def count_ge(t):
        def body(kt, acc):
            return acc + _lane_fold(jnp.where(keys_ref[kt] >= t, 1.0, 0.0))
        return jnp.sum(lax.fori_loop(0, nkt, body, jnp.zeros((tq, LANE), F32)), axis=1, keepdims=True)

    kf = float(topk)
    lo0 = jnp.min(top2_ref[:, LANE:2 * LANE], axis=1, keepdims=True)
    hi0 = jnp.max(top2_ref[:, 0:LANE], axis=1, keepdims=True) + 1

    def search_done(lo, hi, c_lo):
        row_done = jnp.logical_or(c_lo == kf, hi == lo + 1)
        return jnp.min(jnp.where(row_done, 1.0, 0.0)) > 0.0

    def search_cond(state):
        return jnp.logical_not(state[4])

    def search_body(state):
        lo, hi, c_lo, c_hi, _ = state
        for _ in range(4):
            mid = lax.shift_right_arithmetic(lo, 1) + lax.shift_right_arithmetic(hi, 1) + (lo & hi & 1)
            c = count_ge(mid)
            ok = c >= kf
            lo, c_lo = jnp.where(ok, mid, lo), jnp.where(ok, c, c_lo)
            hi, c_hi = jnp.where(ok, hi, mid), jnp.where(ok, c_hi, c)
        return lo, hi, c_lo, c_hi, search_done(lo, hi, c_lo)

    c_lo0 = count_ge(lo0)
    thr, _, c_thr, c_above, _ = lax.while_loop(
        search_cond, search_body, (lo0, hi0, c_lo0, jnp.zeros((tq, 1), F32), search_done(lo0, hi0, c_lo0)))
    need = kf - c_above
    tie_row = jnp.logical_and(c_thr > kf, thr > INT_MIN)
    any_tie = jnp.max(jnp.where(tie_row, 1.0, 0.0)) > 0.0

    @pl.when(jnp.logical_not(any_tie))
    def _():
        thr_eff = jnp.maximum(thr, INT_MIN + 1)

        def body(kt, carry):
            keys_ref[kt] = _bias_bits(keys_ref[kt] >= thr_eff)
            return carry
        lax.fori_loop(0, nkt, body, 0)

    @pl.when(any_tie)
    def _():
        upper = _strict_upper(tk)

        def body(kt, seen):
            kk = keys_ref[kt]
            eq = kk == thr
            before = seen + _dot(jnp.where(eq, 1.0, 0.0).astype(BF16), upper)
            sel = jnp.logical_and(kk > INT_MIN, jnp.logical_or(kk > thr, jnp.logical_and(eq, before < need)))
            keys_ref[kt] = _bias_bits(sel)
            return seen + jnp.sum(jnp.where(eq, 1.0, 0.0), axis=1, keepdims=True)
        lax.fori_loop(0, nkt, body, jnp.zeros((tq, 1), F32))

    m_ref[...] = jnp.full(m_ref.shape, MASKED, F32)
    acc_ref[...] = jnp.zeros_like(acc_ref)
    c2 = (DSA_HEAD_DIM ** -0.5) * 1.4426950408889634
    dh = DSA_HEAD_DIM

    def attend_tile(kt):
        rows = pl.ds(pl.multiple_of(kt * tk, tk), tk)
        bias = lax.bitcast_convert_type(keys_ref[kt], F32)[None]
        for c in range(DSA_KV_HEADS):
            kc = k_ref[0, rows, c * dh:(c + 1) * dh]
            v1 = v1_ref[0, rows, c * 2 * dh:(c + 1) * 2 * dh]
            x = (_dot_nt(qg_ref[c], kc).reshape(DSA_GROUP, tq, tk) + bias).reshape(gq, tk)
            m_old = m_ref[c]
            m_new = jnp.maximum(m_old, jnp.max(x, axis=1, keepdims=True))
            p = jnp.exp2((x - jnp.concatenate([m_new] * (tk // LANE), axis=1)) * c2).astype(BF16)
            alpha = jnp.exp2((m_old - m_new) * c2)
            acc_ref[c] = jnp.concatenate([alpha, alpha], axis=1) * acc_ref[c] + _dot(p, v1)
            m_ref[c] = m_new

    def attend_pair(i, carry):
        attend_tile(2 * i)
        attend_tile(2 * i + 1)
        return carry

    lax.fori_loop(0, nkt // 2, attend_pair, 0)

    @pl.when(nkt % 2 == 1)
    def _():
        attend_tile(nkt - 1)

    for c in range(DSA_KV_HEADS):
        oc = acc_ref[c, :, 0:dh] / acc_ref[c, :, dh:2 * dh]
        for g in range(DSA_GROUP):
            hh = c * DSA_GROUP + g
            o_ref[0, :, hh * dh:(hh + 1) * dh] = oc[g * tq:(g + 1) * tq].astype(BF16)


def _dsa_prompt(hf, hb, *, tq=256, tk=1024):
    bsz, s, _ = hf.shape
    tk = min(tk, s)
    assert s % tq == 0 and s % tk == 0 and tk % tq == 0
    topk = min(TOPK_MAX, s // 4)
    assert topk <= 2 * LANE
    hq = DSA_HEADS * DSA_HEAD_DIM
    gq = DSA_GROUP * tq
    return pl.pallas_call(
        functools.partial(_dsa_prompt_kernel, tq=tq, tk=tk, topk=topk),
        grid=(bsz, s // tq),
        in_specs=[pl.BlockSpec((1, tq, hq), lambda b, j: (b, j, DSA_Q0 // hq)),
                  pl.BlockSpec((1, tq, IDX_HEADS * LANE), lambda b, j: (b, j, DSA_IQ0 // (IDX_HEADS * LANE))),
                  pl.BlockSpec((1, tq, LANE), lambda b, j: (b, j, DSA_IKW0 // LANE)),
                  pl.BlockSpec((1, s, LANE), lambda b, j: (b, 0, DSA_IKW0 // LANE), pipeline_mode=pl.Buffered(1)),
                  pl.BlockSpec((1, s, DSA_HKV), lambda b, j: (b, 0, DSA_K0 // DSA_HKV),
                               pipeline_mode=pl.Buffered(1)),
                  pl.BlockSpec((1, s, 2 * DSA_HKV), lambda b, j: (b, 0, DSA_V1_0 // (2 * DSA_HKV)),
                               pipeline_mode=pl.Buffered(1))],
        out_specs=pl.BlockSpec((1, tq, hq), lambda b, j: (b, j, 0)),
        out_shape=jax.ShapeDtypeStruct((bsz, s, hq), BF16),
        scratch_shapes=[pltpu.VMEM((s // tk, tq, tk), I32),
                        pltpu.VMEM((tq, 2 * LANE), I32),
                        pltpu.VMEM((IDX_HEADS * tq, LANE), BF16),
                        pltpu.VMEM((DSA_KV_HEADS, gq, DSA_HEAD_DIM), BF16),
                        pltpu.VMEM((IDX_HEADS * tq, tk), F32),
                        pltpu.VMEM((DSA_KV_HEADS, gq, LANE), F32),
                        pltpu.VMEM((DSA_KV_HEADS, gq, 2 * DSA_HEAD_DIM), F32)],
        compiler_params=_cparams(("arbitrary", "arbitrary"), 56),
        name="dsa_prompt",
    )(hb, hb, hf, hb, hb, hb)


def _sample_ws(iw_ref):
    return iw_ref[0][:, IDX_DIM:IDX_DIM + IDX_HEADS] * (IDX_HEADS ** -0.5)


def _sample_scores(d, ws, t):
    s = jnp.zeros((t, d.shape[1]), F32)
    for h in range(IDX_HEADS):
        s = s + ws[:, h:h + 1] * jnp.maximum(d[h * t:(h + 1) * t] * (IDX_DIM ** -0.5), 0.0)
    return s + 0.0


def _dsa_sample_scores_kernel(pt_ref, iq_ref, iw_ref, *refs, pp, t):
    page_refs, o_ref = refs[:pp], refs[pp]
    iq_stack = jnp.concatenate([iq_ref[0, :, h * LANE:h * LANE + IDX_DIM] for h in range(IDX_HEADS)], axis=0)
    ik_t = jnp.concatenate([r[0].astype(BF16) for r in page_refs], axis=1)
    o_ref[0] = _sample_scores(_dot(iq_stack.astype(BF16), ik_t), _sample_ws(iw_ref), t)


def _dsa_sample_scores(hf, cache_ik_t, page_table, *, pp):
    bsz, t, _ = hf.shape
    n_pages = page_table.shape[1]
    assert n_pages % pp == 0
    iq_w = IDX_HEADS * LANE

    def page_spec(i):
        return pl.BlockSpec((1, IDX_DIM, PAGE_SIZE), lambda b, j, pt: (pt[b, j * pp + i], 0, 0))

    grid_spec = pltpu.PrefetchScalarGridSpec(
        num_scalar_prefetch=1, grid=(bsz, n_pages // pp),
        in_specs=[pl.BlockSpec((1, t, iq_w), lambda b, j, pt: (b, 0, DSA_IQ0 // iq_w)),
                  pl.BlockSpec((1, t, LANE), lambda b, j, pt: (b, 0, DSA_IKW0 // LANE))]
                 + [page_spec(i) for i in range(pp)],
        out_specs=pl.BlockSpec((1, t, pp * PAGE_SIZE), lambda b, j, pt: (b, 0, j)))
    return pl.pallas_call(
        functools.partial(_dsa_sample_scores_kernel, pp=pp, t=t),
        grid_spec=grid_spec,
        out_shape=jax.ShapeDtypeStruct((bsz, t, n_pages * PAGE_SIZE), F32),
        compiler_params=_cparams(("arbitrary", "arbitrary"), 32),
        name="dsa_sample_scores",
    )(page_table, hf, hf, *([cache_ik_t] * pp))


def _dsa_sample_select_kernel(sp_ref, iq_ref, iw_ref, ikw_ref, o_ref, keys_ref, *, t, past, new_w, topk, ck):
    iq_stack = jnp.concatenate([iq_ref[0, :, h * LANE:(h + 1) * LANE] for h in range(IDX_HEADS)], axis=0)
    ik_new = jnp.concatenate([ikw_ref[0], jnp.zeros((new_w - t, LANE), F32)], axis=0)
    d_new = _dot_nt(iq_stack.astype(BF16), ik_new.astype(BF16))
    s_new = _sample_scores(d_new, _sample_ws(iw_ref), t)
    row = lax.broadcasted_iota(I32, (t, new_w), 0)
    lane = lax.broadcasted_iota(I32, (t, new_w), 1)
    s_new = jnp.where(lane <= row, s_new, -jnp.inf)
    keys_ref[:, 0:past] = _sortable(sp_ref[0])
    keys_ref[:, past:past + new_w] = _sortable(s_new)
    n_ck = (past + new_w) // ck

    def counts(pred):
        acc = jnp.zeros((t, LANE), F32)
        for i in range(n_ck):
            acc = acc + _lane_fold(jnp.where(pred(keys_ref[:, i * ck:(i + 1) * ck]), 1.0, 0.0))
        return jnp.sum(acc, axis=1, keepdims=True)

    thr = _kth_largest_key(lambda cand: counts(lambda kk: kk >= cand), t, topk)
    n_gt = counts(lambda kk: kk > thr)
    n_eq = counts(lambda kk: kk == thr)
    need = float(topk) - n_gt
    tie_row = jnp.logical_and(n_eq > need, thr > NEG_INF_KEY)
    any_tie = jnp.max(jnp.where(tie_row, 1.0, 0.0)) > 0.0

    @pl.when(jnp.logical_not(any_tie))
    def _():
        thr_eff = jnp.maximum(thr, NEG_INF_KEY + 1)
        o_ref[0] = jnp.where(keys_ref[...] >= thr_eff, 1.0, 0.0)

    @pl.when(any_tie)
    def _():
        upper = _strict_upper(ck)
        seen = jnp.zeros((t, 1), F32)
        for i in range(n_ck):
            kk = keys_ref[:, i * ck:(i + 1) * ck]
            eq = kk == thr
            before = seen + _dot(jnp.where(eq, 1.0, 0.0).astype(BF16), upper)
            sel = jnp.logical_and(kk > NEG_INF_KEY, jnp.logical_or(kk > thr, jnp.logical_and(eq, before < need)))
            o_ref[0, :, i * ck:(i + 1) * ck] = jnp.where(sel, 1.0, 0.0)
            seen = seen + jnp.sum(jnp.where(eq, 1.0, 0.0), axis=1, keepdims=True)


def _dsa_sample_select(scores_past, hf, *, new_w=512, ck=512):
    bsz, t, past = scores_past.shape
    assert past % ck == 0 and new_w % ck == 0
    topk = min(TOPK_MAX, (past + t) // 4)
    iq_w = IDX_HEADS * LANE
    total = past + new_w
    return pl.pallas_call(
        functools.partial(_dsa_sample_select_kernel, t=t, past=past, new_w=new_w, topk=topk, ck=ck),
        grid=(bsz,),
        in_specs=[pl.BlockSpec((1, t, past), lambda b: (b, 0, 0)),
                  pl.BlockSpec((1, t, iq_w), lambda b: (b, 0, DSA_IQ0 // iq_w)),
                  pl.BlockSpec((1, t, LANE), lambda b: (b, 0, DSA_IKW0 // LANE)),
                  pl.BlockSpec((1, t, LANE), lambda b: (b, 0, DSA_IKW0 // LANE))],
        out_specs=pl.BlockSpec((1, t, total), lambda b: (b, 0, 0)),
        out_shape=jax.ShapeDtypeStruct((bsz, t, total), F32),
        scratch_shapes=[pltpu.VMEM((t, total), I32)],
        compiler_params=_cparams(("arbitrary",), 32),
        name="dsa_sample_select",
    )(scores_past, hf, hf, hf)


def _dsa_sample_attend_kernel(pt_ref, q_ref, kn_ref, vn_ref, selp_ref, seln_ref, *refs, pp, t, new_w):
    k_pages, v_pages = refs[:pp], refs[pp:2 * pp]
    o_ref, m_ref, l_ref, acc_ref = refs[2 * pp:]
    j = pl.program_id(1)
    gt = DSA_GROUP * t

    @pl.when(j == 0)
    def _():
        m_ref[...] = jnp.full(m_ref.shape, MASKED, F32)
        l_ref[...] = jnp.zeros_like(l_ref)
        acc_ref[...] = jnp.zeros_like(acc_ref)

    q_groups = [jnp.concatenate([q_ref[0, :, (c * DSA_GROUP + g) * DSA_HEAD_DIM:(c * DSA_GROUP + g + 1) * DSA_HEAD_DIM]
                                 for g in range(DSA_GROUP)], axis=0).astype(BF16) for c in range(DSA_KV_HEADS)]

    def attend(kcs, vcs, sel):
        n = kcs[0].shape[0]
        for c in range(DSA_KV_HEADS):
            kc, vc = kcs[c], vcs[c]
            logits = (_dot_nt(q_groups[c], kc) * (DSA_HEAD_DIM ** -0.5)).reshape(DSA_GROUP, t, n)
            masked = jnp.where(sel[None], logits, MASKED).reshape(gt, n)
            m_old = m_ref[c]
            m_new = jnp.maximum(m_old, jnp.max(masked, axis=1, keepdims=True))
            p = jnp.where(sel[None], jnp.exp(masked - m_new).reshape(DSA_GROUP, t, n), 0.0).reshape(gt, n)
            alpha = jnp.exp(m_old - m_new)
            l_ref[c] = alpha * l_ref[c] + jnp.sum(p, axis=1, keepdims=True)
            acc_ref[c] = alpha * acc_ref[c] + _dot(p.astype(BF16), vc)
            m_ref[c] = m_new

    def head_rows(pages, c):
        return jnp.concatenate([r[0, pl.ds(c, PAGE_SIZE, stride=DSA_KV_HEADS), :].astype(BF16) for r in pages], axis=0)

    attend([head_rows(k_pages, c) for c in range(DSA_KV_HEADS)],
           [head_rows(v_pages, c) for c in range(DSA_KV_HEADS)], selp_ref[0] != 0.0)

    @pl.when(j == pl.num_programs(1) - 1)
    def _():
        pad = jnp.zeros((new_w - t, DSA_HEAD_DIM), F32)

        def new_rows(ref, c):
            return jnp.concatenate([ref[0, :, c * DSA_HEAD_DIM:(c + 1) * DSA_HEAD_DIM], pad], axis=0).astype(BF16)

        attend([new_rows(kn_ref, c) for c in range(DSA_KV_HEADS)],
               [new_rows(vn_ref, c) for c in range(DSA_KV_HEADS)], seln_ref[0] != 0.0)
        for c in range(DSA_KV_HEADS):
            oc = acc_ref[c] / l_ref[c]
            for g in range(DSA_GROUP):
                hh = c * DSA_GROUP + g
                o_ref[0, :, hh * DSA_HEAD_DIM:(hh + 1) * DSA_HEAD_DIM] = oc[g * t:(g + 1) * t].astype(BF16)


def _dsa_sample_attend(hf, sel, cache_k, cache_v, page_table, *, pp, new_w=512):
    bsz, t, _ = hf.shape
    n_pages = page_table.shape[1]
    past = n_pages * PAGE_SIZE
    assert n_pages % pp == 0 and past % new_w == 0
    hq = DSA_HEADS * DSA_HEAD_DIM
    gt = DSA_GROUP * t

    def page_spec(i):
        return pl.BlockSpec((1, PAGE_SIZE * DSA_KV_HEADS, DSA_HEAD_DIM), lambda b, j, pt: (pt[b, j * pp + i], 0, 0))

    grid_spec = pltpu.PrefetchScalarGridSpec(
        num_scalar_prefetch=1, grid=(bsz, n_pages // pp),
        in_specs=[pl.BlockSpec((1, t, hq), lambda b, j, pt: (b, 0, DSA_Q0 // hq)),
                  pl.BlockSpec((1, t, DSA_HKV), lambda b, j, pt: (b, 0, DSA_K0 // DSA_HKV)),
                  pl.BlockSpec((1, t, DSA_HKV), lambda b, j, pt: (b, 0, DSA_V0 // DSA_HKV)),
                  pl.BlockSpec((1, t, pp * PAGE_SIZE), lambda b, j, pt: (b, 0, j)),
                  pl.BlockSpec((1, t, new_w), lambda b, j, pt: (b, 0, past // new_w))]
                 + [page_spec(i) for i in range(pp)] * 2,
        out_specs=pl.BlockSpec((1, t, hq), lambda b, j, pt: (b, 0, 0)),
        scratch_shapes=[pltpu.VMEM((DSA_KV_HEADS, gt, 1), F32),
                        pltpu.VMEM((DSA_KV_HEADS, gt, 1), F32),
                        pltpu.VMEM((DSA_KV_HEADS, gt, DSA_HEAD_DIM), F32)])
    return pl.pallas_call(
        functools.partial(_dsa_sample_attend_kernel, pp=pp, t=t, new_w=new_w),
        grid_spec=grid_spec,
        out_shape=jax.ShapeDtypeStruct((bsz, t, hq), BF16),
        compiler_params=_cparams(("arbitrary", "arbitrary"), 48),
        name="dsa_sample_attend",
    )(page_table, hf, hf, hf, sel, sel, *([cache_k] * pp), *([cache_v] * pp))


def _router_kernel(x_ref, wt_ref, o_ref):
    x = x_ref[...]
    wt = wt_ref[...]
    xh = x.astype(BF16)
    xl = (x - xh.astype(F32)).astype(BF16)
    wh = wt.astype(BF16)
    wl = (wt - wh.astype(F32)).astype(BF16)
    logits = _dot_nt(wh, xh) + (_dot_nt(wl, xh) + _dot_nt(wh, xl))
    sub = lax.broadcasted_iota(I32, logits.shape, 0)
    l1 = jnp.max(logits, axis=0, keepdims=True)
    i1 = jnp.min(jnp.where(logits == l1, sub, N_EXPERTS), axis=0, keepdims=True)
    rest = jnp.where(sub == i1, -jnp.inf, logits)
    l2 = jnp.max(rest, axis=0, keepdims=True)
    i2 = jnp.min(jnp.where(rest == l2, sub, N_EXPERTS), axis=0, keepdims=True)
    e2 = jnp.exp(l2 - l1)
    denom = 1.0 + e2
    o_ref[...] = jnp.where(sub == i1, 1.0 / denom, jnp.where(sub == i2, e2 / denom, 0.0))


def _router(x, w_t):
    n, d = x.shape
    tm = _row_tile(n, 512)
    return pl.pallas_call(
        _router_kernel,
        grid=(n // tm,),
        in_specs=[pl.BlockSpec((tm, d), lambda i: (i, 0)), pl.BlockSpec((N_EXPERTS, d), lambda i: (0, 0))],
        out_specs=pl.BlockSpec((N_EXPERTS, tm), lambda i: (0, i)),
        out_shape=jax.ShapeDtypeStruct((N_EXPERTS, n), F32),
        compiler_params=_cparams(("arbitrary",), 32),
        name="router",
    )(x, w_t)


MOE_ROWS = 32
MOE_CHUNK_BLOCKS = 9


def _moe_kernel(cnt_ref, xb_ref, gt_ref, wg_ref, wu_ref, wd_ref, o_ref,
                rank_ref, xg_ref, acc_ref, upper_ref, *, ts, nsub, chunk_blocks, arms):
    i, e, f = pl.program_id(0), pl.program_id(1), pl.program_id(2)
    first_f = f == 0
    last_f = f == pl.num_programs(2) - 1

    @pl.when(jnp.logical_and(i == 0, jnp.logical_and(e == 0, first_f)))
    def _():
        upper_ref[...] = _strict_upper(ts)

    @pl.when(jnp.logical_and(e == 0, first_f))
    def _():
        o_ref[...] = jnp.zeros_like(o_ref)
        for sub in range(nsub):
            chosen = jnp.where(gt_ref[:, sub * ts:(sub + 1) * ts] > 0.0, 1.0, 0.0).astype(BF16)
            rank_ref[sub] = _dot(chosen, upper_ref[...])

    for sub in range(nsub):
        tok = slice(sub * ts, (sub + 1) * ts)
        nb = (cnt_ref[(i * nsub + sub) * N_EXPERTS + e] + MOE_ROWS - 1) // MOE_ROWS
        ge = gt_ref[pl.ds(e, 1), tok]
        slot = jnp.where(ge > 0.0, rank_ref[sub, pl.ds(e, 1), :], -1.0)

        def one_hot(start, rows, slot=slot):
            r = (start + lax.broadcasted_iota(I32, (rows, 1), 0)).astype(F32)
            return slot == r

        def for_arms(fn, nb=nb):
            n_chunks = nb // chunk_blocks
            rest = nb - n_chunks * chunk_blocks

            def chunk(t, carry):
                fn(pl.multiple_of(t * (chunk_blocks * MOE_ROWS), MOE_ROWS), chunk_blocks * MOE_ROWS)
                return carry
            lax.fori_loop(0, n_chunks, chunk, 0)
            for size in arms:
                start = pl.multiple_of((n_chunks * chunk_blocks + (rest & ~(2 * size - 1))) * MOE_ROWS, MOE_ROWS)

                @pl.when((rest & size) != 0)
                def _():
                    fn(start, size * MOE_ROWS)

        @pl.when(first_f)
        def _():
            def gather(start, rows):
                hot = jnp.where(one_hot(start, rows), 1.0, 0.0).astype(BF16)
                xg_ref[sub, pl.ds(start, rows), :] = _dot(hot, xb_ref[tok, :]).astype(BF16)
                acc_ref[sub, pl.ds(start, rows), :] = jnp.zeros((rows, acc_ref.shape[2]), F32)
            for_arms(gather)

        def expert(start, rows):
            xs = xg_ref[sub, pl.ds(start, rows), :]
            hg = _dot(xs, wg_ref[0])
            hu = _dot(xs, wu_ref[0])
            h = (hg * _sigmoid(hg) * hu).astype(BF16)
            acc_ref[sub, pl.ds(start, rows), :] += _dot(h, wd_ref[0])
        for_arms(expert)

        @pl.when(last_f)
        def _():
            def scatter(start, rows):
                hot = one_hot(start, rows)
                gate = jnp.sum(jnp.where(hot, ge, 0.0), axis=1, keepdims=True)
                v = (gate * acc_ref[sub, pl.ds(start, rows), :]).astype(BF16)
                o_ref[tok, :] += _dot_tn(jnp.where(hot, 1.0, 0.0).astype(BF16), v)
            for_arms(scatter)


def _moe(xb, gates_t, wg, wu, wd, *, tf=896):
    n, d = xb.shape
    ff = wg.shape[2]
    assert ff % tf == 0
    ts = _row_tile(n, 1024)
    nsub = 2 if n % (2 * ts) == 0 else 1
    tm = ts * nsub
    assert ts % MOE_ROWS == 0
    chunk_blocks = min(MOE_CHUNK_BLOCKS, ts // MOE_ROWS)
    arms = tuple(s for s in (32, 16, 8, 4, 2, 1) if s < chunk_blocks)
    counts = jnp.sum((gates_t > 0.0).reshape(N_EXPERTS, n // ts, ts), axis=2).astype(I32).T.reshape(-1)
    grid_spec = pltpu.PrefetchScalarGridSpec(
        num_scalar_prefetch=1, grid=(n // tm, N_EXPERTS, ff // tf),
        in_specs=[pl.BlockSpec((tm, d), lambda i, e, f, c: (i, 0)),
                  pl.BlockSpec((N_EXPERTS, tm), lambda i, e, f, c: (0, i)),
                  pl.BlockSpec((1, d, tf), lambda i, e, f, c: (e, 0, f)),
                  pl.BlockSpec((1, d, tf), lambda i, e, f, c: (e, 0, f)),
                  pl.BlockSpec((1, tf, d), lambda i, e, f, c: (e, f, 0))],
        out_specs=pl.BlockSpec((tm, d), lambda i, e, f, c: (i, 0)),
        scratch_shapes=[pltpu.VMEM((nsub, N_EXPERTS, ts), F32), pltpu.VMEM((nsub, ts, d), BF16),
                        pltpu.VMEM((nsub, ts, d), F32), pltpu.VMEM((ts, ts), BF16)])
    return pl.pallas_call(
        functools.partial(_moe_kernel, ts=ts, nsub=nsub, chunk_blocks=chunk_blocks, arms=arms),
        grid_spec=grid_spec,
        out_shape=jax.ShapeDtypeStruct((n, d), F32),
        compiler_params=_cparams(("arbitrary", "arbitrary", "arbitrary"), 60),
        name="moe",
    )(counts, xb, gates_t, wg, wu, wd)


def _pad_cols(w, cols):
    return jnp.pad(w, ((0, 0), (0, cols - w.shape[1])))


def _gla_layer(x, s0, w_in, w_up, b_a, norm_g, w_out, ln_g, ln_b, *, alpha):
    bsz, t, d = x.shape
    x2 = x.reshape(bsz * t, d)
    hf = _proj(x2, _pad_cols(w_in, GLA_COLS).astype(BF16), tn=640, bf16_copy=False)
    hf = hf.reshape(bsz, t, GLA_COLS)
    c = GLA_CHUNK if t % GLA_CHUNK == 0 else t
    valid = c
    if c < GLA_CHUNK:
        hf = jnp.pad(hf, ((0, 0), (0, GLA_CHUNK - c), (0, 0)))
        c = GLA_CHUNK
    w_up_pad = jnp.pad(w_up, ((0, LANE - GLA_GATE_RANK), (0, 0))).astype(BF16)
    o, st = _gla_core(hf, jnp.swapaxes(s0, 2, 3), w_up_pad, b_a.reshape(1, GLA_HK), norm_g.reshape(1, GLA_HV),
                      c=c, valid=valid)
    o = o[:, :t].reshape(bsz * t, GLA_HV)
    y, yb = _mm_res_ln(x2, o, w_out.astype(BF16), ln_g.reshape(1, d), ln_b.reshape(1, d), alpha=alpha)
    return y.reshape(bsz, t, d), yb, jnp.swapaxes(st, 2, 3)


def _dsa_weight(w_in):
    hq = DSA_HEADS * DSA_HEAD_DIM
    iq0 = hq + 2 * DSA_HKV
    ik0 = iq0 + IDX_HEADS * IDX_DIM
    w_iq = w_in[:, iq0:ik0].reshape(-1, IDX_HEADS, IDX_DIM)
    w_iq = jnp.pad(w_iq, ((0, 0), (0, 0), (0, LANE - IDX_DIM))).reshape(-1, IDX_HEADS * LANE)
    w_v = w_in[:, hq + DSA_HKV:iq0].reshape(-1, DSA_KV_HEADS, DSA_HEAD_DIM)
    w_v1 = jnp.pad(w_v, ((0, 0), (0, 0), (0, DSA_HEAD_DIM))).reshape(-1, 2 * DSA_HKV)
    w = jnp.concatenate([w_in[:, :hq], w_iq, w_in[:, hq:iq0], w_v1, _pad_cols(w_in[:, ik0:], LANE)], axis=1)
    return w.astype(BF16)


def _dsa_project(x, w_in):
    bsz, t, d = x.shape
    hf, hb = _proj(x.reshape(bsz * t, d), _dsa_weight(w_in), tn=DSA_COLS // 5, ones_at=DSA_ONES)
    hf = hf.reshape(bsz, t, DSA_COLS)
    k = hf[:, :, DSA_K0:DSA_V0].reshape(bsz, t, DSA_KV_HEADS, DSA_HEAD_DIM)
    v = hf[:, :, DSA_V0:DSA_V1_0].reshape(bsz, t, DSA_KV_HEADS, DSA_HEAD_DIM)
    ik = hf[:, :, DSA_IKW0:DSA_IKW0 + IDX_DIM]
    return hf, hb.reshape(bsz, t, DSA_COLS), k, v, ik


def kernel(x_prompt, x_sample, state_gla, cache_k, cache_v, cache_idx_k, page_table, p_prompt, p_sample,
           ln_mix_g, ln_mix_b, ln_ffn_g, ln_ffn_b, w_in_gla, w_alpha_up, b_alpha, gla_norm_g, w_out_gla,
           w_in_dsa, w_out_dsa, w_ffn_gate, w_ffn_up, w_ffn_down, w_router, w_exp_gate, w_exp_up,
           w_exp_down, w_ple_proj, w_ple_gate):
    depth = ln_mix_g.shape[0]
    alpha = (2.0 * depth) ** 0.25
    d = x_prompt.shape[-1]
    xp, xs = x_prompt, x_sample
    gla_p, gla_s, kp, vp, ikp, ksm, vsm, iksm = [], [], [], [], [], [], [], []
    n_pages = page_table.shape[1]
    pp = 32 if n_pages % 32 == 0 else n_pages
    for i in range(depth):
        j = i // 2
        lg, lb = ln_mix_g[i], ln_mix_b[i]
        if i % 2 == 0:
            gw = (w_in_gla[j], w_alpha_up[j], b_alpha[j], gla_norm_g[j], w_out_gla[j], lg, lb)
            s0 = jnp.zeros((xp.shape[0], GLA_HEADS, GLA_DK, GLA_DV), F32)
            xp, xpb, sp = _gla_layer(xp, s0, *gw, alpha=alpha)
            xs, xsb, ss = _gla_layer(xs, state_gla[j], *gw, alpha=alpha)
            gla_p.append(sp)
            gla_s.append(ss)
        else:
            w_out = w_out_dsa[j].astype(BF16)
            hf, hb, k_p, v_p, ik_p = _dsa_project(xp, w_in_dsa[j])
            op = _dsa_prompt(hf, hb)
            bp, tp, _ = xp.shape
            xp, xpb = _mm_res_ln(xp.reshape(bp * tp, d), op.reshape(bp * tp, -1), w_out, lg.reshape(1, d),
                                 lb.reshape(1, d), alpha=alpha)
            xp = xp.reshape(bp, tp, d)
            hf, hb, k_s, v_s, ik_s = _dsa_project(xs, w_in_dsa[j])
            sc = _dsa_sample_scores(hf, jnp.swapaxes(cache_idx_k[j], 1, 2), page_table, pp=pp)
            sel = _dsa_sample_select(sc, hf)
            n_pool = cache_k.shape[1]
            rows = PAGE_SIZE * DSA_KV_HEADS
            os_ = _dsa_sample_attend(hf, sel, cache_k[j].reshape(n_pool, rows, DSA_HEAD_DIM),
                                     cache_v[j].reshape(n_pool, rows, DSA_HEAD_DIM), page_table, pp=pp)
            bs, ts, _ = xs.shape
            xs, xsb = _mm_res_ln(xs.reshape(bs * ts, d), os_.reshape(bs * ts, -1), w_out, lg.reshape(1, d),
                                 lb.reshape(1, d), alpha=alpha)
            xs = xs.reshape(bs, ts, d)
            kp.append(k_p)
            vp.append(v_p)
            ikp.append(ik_p)
            ksm.append(k_s)
            vsm.append(v_s)
            iksm.append(ik_s)
        fg, fb = ln_ffn_g[i].reshape(1, d), ln_ffn_b[i].reshape(1, d)
        outs = []
        wpg, wpp = w_ple_gate[i].astype(BF16), w_ple_proj[i].astype(BF16)
        for x, xb, p in ((xp, xpb, p_prompt[i]), (xs, xsb, p_sample[i])):
            bsz, t, _ = x.shape
            x2 = x.reshape(bsz * t, d)
            p2 = p.reshape(bsz * t, -1)
            if i % 2 == 0:
                y2 = _ffn_ln_ple(x2, w_ffn_gate[j].astype(BF16), w_ffn_up[j].astype(BF16),
                                 w_ffn_down[j].astype(BF16), fg, fb, p2, wpg, wpp, alpha=alpha)
            else:
                gates = _router(x2, w_router[j].T)
                f2 = _moe(xb, gates, w_exp_gate[j].astype(BF16), w_exp_up[j].astype(BF16),
                          w_exp_down[j].astype(BF16))
                y2 = _res_ln_ple(x2, f2, p2, fg, fb, wpg, wpp, alpha=alpha)
            outs.append(y2.reshape(bsz, t, d))
        xp, xs = outs
    return (xp, xs, jnp.stack(gla_p), jnp.stack(gla_s), jnp.stack(kp), jnp.stack(vp), jnp.stack(ikp),
            jnp.stack(ksm), jnp.stack(vsm), jnp.stack(iksm))
```
